```python
import math
import jax, jax.numpy as jnp
from jax import lax
import numpy as np

D_MODEL = 1024
BATCH = 4
SEQ = 4096
DEPTH = 2

D_RNN = 1024
RNN_BLOCKS = 4
RNN_BW = D_RNN // RNN_BLOCKS
CONV_W = 4
LRU_C = 8.0
ATT_GROUPS = ((128, 1), (512, 4), (2048, 16))
N_GROUPS = len(ATT_GROUPS)
ATT_HEADS = 8
ATT_HEAD_DIM = 64
ATT_W = ATT_HEADS * ATT_HEAD_DIM
ATT_BLOCK = 128
OFF_XR = 0
OFF_YR = OFF_XR + D_RNN
OFF_Q = OFF_YR + D_RNN
OFF_K = OFF_Q + N_GROUPS * ATT_W
OFF_V = OFF_K + N_GROUPS * ATT_W
OFF_GA = OFF_V + N_GROUPS * ATT_W
OFF_GB = OFF_GA + D_MODEL
N_IN = OFF_GB + D_MODEL
N_EXPERTS = 32
TOP_K = 4
D_FF = D_MODEL
SWIGLU_ALPHA = 1.702
SWIGLU_LIMIT = 7.0
MOE_BLOCK = 128
PLE_DIM = 256
ALPHA = (2.0 * DEPTH) ** 0.25
BETA = (8.0 * DEPTH) ** -0.25
LN_EPS = 1e-5

kernel_name = 'hybrid_rglru_dilated_attn_moe_deepnorm'


def _layer_norm(x, g, b):
    xf = x.astype(jnp.float32)
    mu = jnp.mean(xf, axis=-1, keepdims=True)
    var = jnp.mean(jnp.square(xf - mu), axis=-1, keepdims=True)
    y = (xf - mu) * lax.rsqrt(var + LN_EPS)
    return (y * g.astype(jnp.float32) + b.astype(jnp.float32)).astype(x.dtype)


def _causal_conv(x, w, b):
    C = x.shape[-1]
    y = lax.conv_general_dilated(x, w[:, None, :].astype(x.dtype), window_strides=(1,),
                                 padding=((CONV_W - 1, 0),),
                                 dimension_numbers=('NWC', 'WIO', 'NWC'),
                                 feature_group_count=C)
    return y + b.astype(x.dtype)


def _rg_lru(xc, w_rg, b_rg, w_ig, b_ig, lam):
    B, S, C = xc.shape
    xf = xc.astype(jnp.float32)
    xb = xf.reshape(B, S, RNN_BLOCKS, RNN_BW)
    r = jax.nn.sigmoid(jnp.einsum('bsnc,ncd->bsnd', xb, w_rg.astype(jnp.float32)).reshape(B, S, C) + b_rg)
    i = jax.nn.sigmoid(jnp.einsum('bsnc,ncd->bsnd', xb, w_ig.astype(jnp.float32)).reshape(B, S, C) + b_ig)
    log_a = -LRU_C * r * jax.nn.softplus(-lam.astype(jnp.float32))
    a = jnp.exp(log_a)
    bx = jnp.sqrt(-jnp.expm1(2.0 * log_a)) * (i * xf)

    def combine(left, right):
        a1, b1 = left
        a2, b2 = right
        return a1 * a2, a2 * b1 + b2

    _, h = lax.associative_scan(combine, (a, bx), axis=1)
    return h.astype(xc.dtype)


def _dilated_window_attention(q, k, v, dilation, n_back):
    B, S, H, Dh = q.shape
    L = S // dilation
    nb = -(-L // ATT_BLOCK)
    Lp = nb * ATT_BLOCK

    def split(t):
        t = t.reshape(B, L, dilation, H, Dh).transpose(0, 2, 1, 3, 4).reshape(B * dilation, L, H, Dh)
        t = jnp.pad(t, ((0, 0), (0, Lp - L), (0, 0), (0, 0)))
        return t.reshape(B * dilation, nb, ATT_BLOCK, H, Dh)

    def with_prev(t):
        prev = jnp.pad(t[:, :-1], ((0, 0), (1, 0), (0, 0), (0, 0), (0, 0)))
        return jnp.concatenate([prev, t], axis=2)

    qb = split(q)
    kw = with_prev(split(k))
    vw = with_prev(split(v))
    s = jnp.einsum('bnqhd,bnkhd->bnhqk', qb, kw,
                   preferred_element_type=jnp.float32) * (Dh ** -0.5)
    qi = jnp.arange(ATT_BLOCK)[:, None]
    kj = jnp.arange(2 * ATT_BLOCK)[None, :]
    diff = ATT_BLOCK + qi - kj
    kpos = (jnp.arange(nb) * ATT_BLOCK - ATT_BLOCK)[:, None, None] + kj[None]
    valid = (diff >= 0)[None] & (diff <= n_back)[None] & (kpos >= 0)
    s = jnp.where(valid[None, :, None], s, -jnp.inf)
    lse = jax.nn.logsumexp(s, axis=-1)
    pr = jnp.exp(s - lse[..., None])
    o = jnp.einsum('bnhqk,bnkhd->bnqhd', pr.astype(vw.dtype), vw)
    o = o.reshape(B * dilation, Lp, H, Dh)[:, :L]
    o = o.reshape(B, dilation, L, H, Dh).transpose(0, 2, 1, 3, 4).reshape(B, S, H, Dh)
    lse = lse.transpose(0, 1, 3, 2).reshape(B * dilation, Lp, H)[:, :L]
    lse = lse.reshape(B, dilation, L, H).transpose(0, 2, 1, 3).reshape(B, S, H)
    return o, lse


def _mixer(u, w_in, conv_w, conv_b, w_rg, b_rg, w_ig, b_ig, lru_lambda, w_rnn_out, w_att_out, w_out):
    B, S, _ = u.shape
    z = u @ w_in
    xr, yr, q_all, k_all, v_all, ga, gb = jnp.split(
        z, [OFF_YR, OFF_Q, OFF_K, OFF_V, OFF_GA, OFF_GB], axis=-1)
    h = _rg_lru(_causal_conv(xr, conv_w, conv_b), w_rg, b_rg, w_ig, b_ig, lru_lambda)
    y_a = (jax.nn.gelu(yr) * h) @ w_rnn_out
    outs, lses = [], []
    for gi, (window, dil) in enumerate(ATT_GROUPS):
        sl = slice(gi * ATT_W, (gi + 1) * ATT_W)
        shp = (B, S, ATT_HEADS, ATT_HEAD_DIM)
        o, lse = _dilated_window_attention(q_all[..., sl].reshape(shp), k_all[..., sl].reshape(shp),
                                           v_all[..., sl].reshape(shp), dil, window // dil)
        outs.append(o)
        lses.append(lse)
    wts = jax.nn.softmax(jnp.stack(lses), axis=0)
    o = jnp.einsum('gbsh,gbshd->bshd', wts.astype(u.dtype), jnp.stack(outs))
    y_b = o.reshape(B, S, ATT_W) @ w_att_out
    merged = jax.nn.sigmoid(ga) * y_a + jax.nn.sigmoid(gb) * y_b
    return merged @ w_out


def _moe(x, w_router, b_router, w_gate, b_gate, w_up, b_up, w_down, b_down):
    B, S, D = x.shape
    T = B * S
    xt = x.reshape(T, D)
    logits = (xt @ w_router).astype(jnp.float32) + b_router.astype(jnp.float32)
    top_v, top_e = lax.top_k(logits, TOP_K)
    gates = jax.nn.softmax(top_v, axis=-1)
    flat_e = top_e.reshape(-1)
    flat_g = gates.reshape(-1)
    A = T * TOP_K
    order = jnp.argsort(flat_e)
    e_sorted = flat_e[order]
    counts = jnp.bincount(flat_e, length=N_EXPERTS)
    padded = (counts + MOE_BLOCK - 1) // MOE_BLOCK * MOE_BLOCK
    pend = jnp.cumsum(padded)
    pstart = pend - padded
    cstart = jnp.cumsum(counts) - counts
    dest = pstart[e_sorted] + (jnp.arange(A) - cstart[e_sorted])
    n_blk = -(-A // MOE_BLOCK) + N_EXPERTS
    P = n_blk * MOE_BLOCK
    buf_tok = jnp.zeros((P,), jnp.int32).at[dest].set((order // TOP_K).astype(jnp.int32))
    buf_w = jnp.zeros((P,), jnp.float32).at[dest].set(flat_g[order])
    blk_expert = jnp.minimum(jnp.searchsorted(pend, jnp.arange(n_blk) * MOE_BLOCK, side='right'),
                             N_EXPERTS - 1)
    xb = xt[buf_tok].reshape(n_blk, MOE_BLOCK, D)

    def expert_block(args):
        xe, e = args
        g = xe @ w_gate[e] + b_gate[e]
        up = xe @ w_up[e] + b_up[e]
        g = jnp.minimum(g, SWIGLU_LIMIT)
        up = jnp.clip(up, -SWIGLU_LIMIT, SWIGLU_LIMIT)
        hdn = (up + 1.0) * (g * jax.nn.sigmoid(SWIGLU_ALPHA * g))
        return hdn @ w_down[e] + b_down[e]

    yb = lax.map(expert_block, (xb, blk_expert)).reshape(P, D)
    y = jnp.zeros((T, D), jnp.float32).at[buf_tok].add(yb.astype(jnp.float32) * buf_w[:, None])
    return y.astype(x.dtype).reshape(B, S, D)


def setup_inputs(seed: int = 0) -> dict:
    key = jax.random.key(seed)
    ks = jax.random.split(key, 32)
    L = DEPTH
    nrm = lambda k, shape, scale: jax.random.normal(k, shape, jnp.float32) * scale
    col_scale = jnp.ones((N_IN,), jnp.float32).at[OFF_V:OFF_GA].set(BETA)
    u = jax.random.uniform(ks[10], (L, D_RNN), jnp.float32, 0.9, 0.999)
    s = u ** (1.0 / LRU_C)
    lru_lambda = jnp.log(s) - jnp.log1p(-s)
    return {
        'x': nrm(ks[0], (BATCH, SEQ, D_MODEL), 1.0),
        'p': nrm(ks[1], (DEPTH, BATCH, SEQ, PLE_DIM), 1.0),
        'w_in': nrm(ks[2], (L, D_MODEL, N_IN), D_MODEL ** -0.5) * col_scale,
        'conv_w': nrm(ks[3], (L, CONV_W, D_RNN), CONV_W ** -0.5),
        'conv_b': nrm(ks[4], (L, D_RNN), 0.01),
        'w_rg': nrm(ks[5], (L, RNN_BLOCKS, RNN_BW, RNN_BW), RNN_BW ** -0.5),
        'b_rg': nrm(ks[6], (L, D_RNN), 0.01),
        'w_ig': nrm(ks[7], (L, RNN_BLOCKS, RNN_BW, RNN_BW), RNN_BW ** -0.5),
        'b_ig': nrm(ks[8], (L, D_RNN), 0.01),
        'lru_lambda': lru_lambda,
        'w_rnn_out': nrm(ks[11], (L, D_RNN, D_MODEL), D_RNN ** -0.5),
        'w_att_out': nrm(ks[12], (L, ATT_W, D_MODEL), ATT_W ** -0.5),
        'w_out': nrm(ks[13], (L, D_MODEL, D_MODEL), BETA * D_MODEL ** -0.5),
        'ln1_g': 1.0 + nrm(ks[14], (L, D_MODEL), 0.02),
        'ln1_b': nrm(ks[15], (L, D_MODEL), 0.02),
        'w_router': nrm(ks[16], (L, D_MODEL, N_EXPERTS), D_MODEL ** -0.5),
        'b_router': nrm(ks[17], (L, N_EXPERTS), 0.01),
        'w_gate': nrm(ks[18], (L, N_EXPERTS, D_MODEL, D_FF), BETA * D_MODEL ** -0.5),
        'b_gate': nrm(ks[19], (L, N_EXPERTS, D_FF), 0.01),
        'w_up': nrm(ks[20], (L, N_EXPERTS, D_MODEL, D_FF), BETA * D_MODEL ** -0.5),
        'b_up': nrm(ks[21], (L, N_EXPERTS, D_FF), 0.01),
        'w_down': nrm(ks[22], (L, N_EXPERTS, D_FF, D_MODEL), BETA * D_FF ** -0.5),
        'b_down': nrm(ks[23], (L, N_EXPERTS, D_MODEL), 0.01),
        'ln2_g': 1.0 + nrm(ks[24], (L, D_MODEL), 0.02),
        'ln2_b': nrm(ks[25], (L, D_MODEL), 0.02),
        'w_ple': nrm(ks[26], (L, PLE_DIM, D_MODEL), BETA * PLE_DIM ** -0.5),
        'w_ple_gate': nrm(ks[27], (L, D_MODEL, D_MODEL), D_MODEL ** -0.5),
        'b_ple_gate': nrm(ks[28], (L, D_MODEL), 0.01),
        'ln3_g': 1.0 + nrm(ks[29], (L, D_MODEL), 0.02),
        'ln3_b': nrm(ks[30], (L, D_MODEL), 0.02),
    }


def reference(x, p, w_in, conv_w, conv_b, w_rg, b_rg, w_ig, b_ig, lru_lambda, w_rnn_out,
              w_att_out, w_out, ln1_g, ln1_b, w_router, b_router, w_gate, b_gate, w_up, b_up,
              w_down, b_down, ln2_g, ln2_b, w_ple, w_ple_gate, b_ple_gate, ln3_g, ln3_b):
    for i in range(DEPTH):
        h = _mixer(x, w_in[i], conv_w[i], conv_b[i], w_rg[i], b_rg[i], w_ig[i], b_ig[i],
                   lru_lambda[i], w_rnn_out[i], w_att_out[i], w_out[i])
        x = _layer_norm(ALPHA * x + h, ln1_g[i], ln1_b[i])
        h = _moe(x, w_router[i], b_router[i], w_gate[i], b_gate[i], w_up[i], b_up[i],
                 w_down[i], b_down[i])
        x = _layer_norm(ALPHA * x + h, ln2_g[i], ln2_b[i])
        ple = (p[i] @ w_ple[i]) * jax.nn.sigmoid(x @ w_ple_gate[i] + b_ple_gate[i])
        x = _layer_norm(ALPHA * x + ple, ln3_g[i], ln3_b[i])
    return x
```

```python
import functools

import jax
import jax.numpy as jnp
from jax import lax
from jax.experimental import pallas as pl
from jax.experimental.pallas import tpu as pltpu

F32 = jnp.float32
BF16 = jnp.bfloat16
I32 = jnp.int32

D_MODEL = 1024
DEPTH = 2
D_RNN = 1024
RNN_BLOCKS = 4
RNN_BW = D_RNN // RNN_BLOCKS
CONV_W = 4
LRU_C = 8.0
ATT_GROUPS = ((128, 1), (512, 4), (2048, 16))
N_GROUPS = len(ATT_GROUPS)
ATT_HEADS = 8
ATT_HEAD_DIM = 64
ATT_W = ATT_HEADS * ATT_HEAD_DIM
ATT_BLOCK = 128
OFF_XR = 0
OFF_YR = OFF_XR + D_RNN
OFF_Q = OFF_YR + D_RNN
OFF_K = OFF_Q + N_GROUPS * ATT_W
OFF_V = OFF_K + N_GROUPS * ATT_W
OFF_GA = OFF_V + N_GROUPS * ATT_W
OFF_GB = OFF_GA + D_MODEL
N_IN = OFF_GB + D_MODEL
N_EXPERTS = 32
TOP_K = 4
D_FF = D_MODEL
SWIGLU_ALPHA = 1.702
SWIGLU_LIMIT = 7.0
PLE_DIM = 256
ALPHA = (2.0 * DEPTH) ** 0.25
LN_EPS = 1e-5

LANES = 128
SUBLANES = 8
COLB = ATT_W
N_COLB = N_IN // COLB
MOE_ROWS = 256
VMEM_LIMIT = 56 * 1024 * 1024
NEG_BIG = -1e30


def _cparams(sem):
    return pltpu.CompilerParams(dimension_semantics=sem, vmem_limit_bytes=VMEM_LIMIT)


def _layer_norm(v, g, b):
    mu = jnp.mean(v, axis=-1, keepdims=True)
    c = v - mu
    var = jnp.mean(c * c, axis=-1, keepdims=True)
    return c * lax.rsqrt(var + LN_EPS) * g + b


def _sigmoid(v):
    return 1.0 / (1.0 + jnp.exp(-v))


def _inproj_body(x_ref, w_ref, o_ref):
    o_ref[...] = jnp.dot(x_ref[...].astype(BF16), w_ref[...],
                         preferred_element_type=F32).astype(o_ref.dtype)


def _in_proj(x2d, w_bf16):
    t, d = x2d.shape
    n = w_bf16.shape[1]
    tm = min(512, t)
    tn = n // 4
    return pl.pallas_call(
        _inproj_body,
        grid=(t // tm, n // tn),
        in_specs=[pl.BlockSpec((tm, d), lambda i, j: (i, 0)),
                  pl.BlockSpec((d, tn), lambda i, j: (0, j))],
        out_specs=pl.BlockSpec((tm, tn), lambda i, j: (i, j)),
        out_shape=jax.ShapeDtypeStruct((t, n), BF16),
        compiler_params=_cparams(("parallel", "parallel")),
        name="in_proj",
    )(x2d, w_bf16)


def _lru_body(xr_ref, yr_ref, cw_ref, cb_ref, wrg_ref, brg_ref, wig_ref, big_ref, lam_ref,
              wout_ref, o_ref, tail_ref, h_ref):
    s = pl.program_id(1)

    @pl.when(s == 0)
    def _():
        tail_ref[...] = jnp.zeros_like(tail_ref)
        h_ref[...] = jnp.zeros_like(h_ref)

    xr = xr_ref[...].astype(F32)
    ts = xr.shape[0]
    xe = jnp.concatenate([tail_ref[...], xr], axis=0)
    cw = cw_ref[...]
    xc = (cb_ref[...] + cw[3:4] * xr + cw[2:3] * xe[7:7 + ts]
          + cw[1:2] * xe[6:6 + ts] + cw[0:1] * xe[5:5 + ts])
    tail_ref[...] = xr[ts - 8:]

    xcb = xc.astype(BF16)

    def gate(w_ref, b_ref):
        parts = [jnp.dot(xcb[:, n * RNN_BW:(n + 1) * RNN_BW], w_ref[n],
                         preferred_element_type=F32) for n in range(RNN_BLOCKS)]
        return _sigmoid(jnp.concatenate(parts, axis=-1) + b_ref[...])

    r = gate(wrg_ref, brg_ref)
    i = gate(wig_ref, big_ref)
    nlam = -lam_ref[...]
    softplus = jnp.maximum(nlam, 0.0) + jnp.log1p(jnp.exp(-jnp.abs(nlam)))
    log_a = (-LRU_C) * r * softplus
    a = jnp.exp(log_a)
    bx = jnp.sqrt(jnp.tanh(-log_a) * (1.0 + a * a)) * (i * xc)

    row = lax.broadcasted_iota(I32, (ts, 1), 0)
    k = 1
    while k < ts:
        keep = row >= k
        a_prev = pltpu.roll(a, k, 0)
        b_prev = pltpu.roll(bx, k, 0)
        bx = jnp.where(keep, a * b_prev, 0.0) + bx
        a = jnp.where(keep, a * a_prev, a)
        k *= 2
    h = a * h_ref[0:1] + bx
    h_ref[0:1] = h[ts - 1:ts]

    yr = yr_ref[...].astype(F32)
    gelu = 0.5 * yr * (1.0 + jnp.tanh(0.7978845608028654 * (yr + 0.044715 * (yr * yr * yr))))
    o_ref[...] = jnp.dot((gelu * h).astype(BF16), wout_ref[...],
                         preferred_element_type=F32).astype(o_ref.dtype)


def _lru_branch(z, batch, seq, conv_w, conv_b, w_rg, b_rg, w_ig, b_ig, lam, w_rnn_out):
    ts = min(256, seq)
    ns = seq // ts
    c = D_RNN
    row = lambda v: v.reshape(1, c)
    full = lambda shape: pl.BlockSpec(shape, lambda b, s: (0,) * len(shape))
    return pl.pallas_call(
        _lru_body,
        grid=(batch, ns),
        in_specs=[pl.BlockSpec((ts, c), lambda b, s: (b * ns + s, OFF_XR // c)),
                  pl.BlockSpec((ts, c), lambda b, s: (b * ns + s, OFF_YR // c)),
                  full((CONV_W, c)), full((1, c)),
                  full((RNN_BLOCKS, RNN_BW, RNN_BW)), full((1, c)),
                  full((RNN_BLOCKS, RNN_BW, RNN_BW)), full((1, c)),
                  full((1, c)), full((c, D_MODEL))],
        out_specs=pl.BlockSpec((ts, D_MODEL), lambda b, s: (b * ns + s, 0)),
        out_shape=jax.ShapeDtypeStruct((batch * seq, D_MODEL), BF16),
        scratch_shapes=[pltpu.VMEM((8, c), F32), pltpu.VMEM((8, c), F32)],
        compiler_params=_cparams(("arbitrary", "arbitrary")),
        name="lru_branch",
    )(z, z, conv_w, row(conv_b), w_rg.astype(BF16), row(b_rg), w_ig.astype(BF16), row(b_ig),
      row(lam), w_rnn_out.astype(BF16))


def _attn_body(n_back, q_ref, kp_ref, kc_ref, vp_ref, vc_ref, o_ref, lse_ref):
    n = pl.program_id(2)
    blk = ATT_BLOCK
    q = q_ref[0]
    k = jnp.concatenate([kp_ref[0], kc_ref[0]], axis=0)
    v = jnp.concatenate([vp_ref[0], vc_ref[0]], axis=0)
    qi = lax.broadcasted_iota(I32, (blk, 2 * blk), 0)
    kj = lax.broadcasted_iota(I32, (blk, 2 * blk), 1)
    diff = blk + qi - kj
    valid = (diff >= 0) & (diff <= n_back) & ((kj >= blk) | (n > 0))
    lane = lax.broadcasted_iota(I32, (blk, LANES), 1)
    lse_tile = jnp.zeros((blk, LANES), F32)
    outs = []
    for h in range(ATT_HEADS):
        sl = slice(h * ATT_HEAD_DIM, (h + 1) * ATT_HEAD_DIM)
        s = lax.dot_general(q[:, sl], k[:, sl], (((1,), (1,)), ((), ())),
                            preferred_element_type=F32) * (ATT_HEAD_DIM ** -0.5)
        s = jnp.where(valid, s, NEG_BIG)
        m = jnp.max(s, axis=-1, keepdims=True)
        p = jnp.exp(s - m)
        l = jnp.sum(p, axis=-1, keepdims=True)
        o = jnp.dot(p.astype(BF16), v[:, sl], preferred_element_type=F32)
        outs.append(o / l)
        lse_tile = jnp.where(lane == h, m + jnp.log(l), lse_tile)
    o_ref[0] = jnp.concatenate(outs, axis=-1).astype(o_ref.dtype)
    lse_ref[0] = lse_tile


def _attention_group(z, batch, seq, gi):
    window, dil = ATT_GROUPS[gi]
    n_back = window // dil
    l = seq // dil
    nb = l // ATT_BLOCK
    zv = z.reshape(batch, l, dil * N_IN)
    qc, kc, vc = OFF_Q // COLB + gi, OFF_K // COLB + gi, OFF_V // COLB + gi
    blk = (1, ATT_BLOCK, COLB)

    def cur(c):
        return pl.BlockSpec(blk, lambda b, r, n: (b, n, r * N_COLB + c))

    def prev(c):
        return pl.BlockSpec(blk, lambda b, r, n: (b, jnp.maximum(n - 1, 0), r * N_COLB + c))

    o, lse = pl.pallas_call(
        functools.partial(_attn_body, n_back),
        grid=(batch, dil, nb),
        in_specs=[cur(qc), prev(kc), cur(kc), prev(vc), cur(vc)],
        out_specs=[pl.BlockSpec(blk, lambda b, r, n: (b, n, r)),
                   pl.BlockSpec((1, ATT_BLOCK, LANES), lambda b, r, n: (b, n, r))],
        out_shape=[jax.ShapeDtypeStruct((batch, l, dil * ATT_W), BF16),
                   jax.ShapeDtypeStruct((batch, l, dil * LANES), F32)],
        compiler_params=_cparams(("parallel", "parallel", "arbitrary")),
        name=f"attn_g{gi}",
    )(zv, zv, zv, zv, zv)
    return o.reshape(batch * seq, ATT_W), lse.reshape(batch * seq, LANES)


def _merge_body(o0_ref, o1_ref, o2_ref, l0_ref, l1_ref, l2_ref, ya_ref, ga0_ref, ga1_ref, gb0_ref,
                gb1_ref, x_ref, watt_ref, wout_ref, g_ref, b_ref, wr_ref, br_ref,
                x1_ref, route_ref, cnt_ref, carry_ref):
    i = pl.program_id(0)

    @pl.when(i == 0)
    def _():
        carry_ref[...] = jnp.zeros_like(carry_ref)

    ts = x_ref.shape[0]
    lses = [l0_ref[...], l1_ref[...], l2_ref[...]]
    mx = jnp.maximum(jnp.maximum(lses[0], lses[1]), lses[2])
    es = [jnp.exp(v - mx) for v in lses]
    den = es[0] + es[1] + es[2]
    er = lax.broadcasted_iota(I32, (LANES, ATT_W), 0)
    ec = lax.broadcasted_iota(I32, (LANES, ATT_W), 1)
    expand = jnp.where(ec // ATT_HEAD_DIM == er, 1.0, 0.0).astype(BF16)
    o = jnp.zeros((ts, ATT_W), F32)
    for e, o_ref in zip(es, (o0_ref, o1_ref, o2_ref)):
        w = e / den
        w_hi = w.astype(BF16)
        w_lo = (w - w_hi.astype(F32)).astype(BF16)
        wx = (jnp.dot(w_hi, expand, preferred_element_type=F32)
              + jnp.dot(w_lo, expand, preferred_element_type=F32))
        o = o + wx * o_ref[...].astype(F32)
    y_b = jnp.dot(o.astype(BF16), watt_ref[...], preferred_element_type=F32)
    ga = jnp.concatenate([ga0_ref[...], ga1_ref[...]], axis=-1).astype(F32)
    gb = jnp.concatenate([gb0_ref[...], gb1_ref[...]], axis=-1).astype(F32)
    merged = _sigmoid(ga) * ya_ref[...].astype(F32) + _sigmoid(gb) * y_b
    hmix = jnp.dot(merged.astype(BF16), wout_ref[...], preferred_element_type=F32)
    x1 = _layer_norm(ALPHA * x_ref[...] + hmix, g_ref[...], b_ref[...])
    x1_ref[...] = x1

    wr = wr_ref[...]
    wr_hi = wr.astype(BF16)
    wr_lo = (wr - wr_hi.astype(F32)).astype(BF16)
    x_hi = x1.astype(BF16)
    x_lo = (x1 - x_hi.astype(F32)).astype(BF16)
    logits = (jnp.dot(x_hi, wr_hi, preferred_element_type=F32)
              + jnp.dot(x_lo, wr_hi, preferred_element_type=F32)
              + jnp.dot(x_hi, wr_lo, preferred_element_type=F32)) + br_ref[...]

    el = lax.broadcasted_iota(I32, (ts, N_EXPERTS), 1)
    work = logits
    vals, idxs, hots = [], [], []
    for _ in range(TOP_K):
        m = jnp.max(work, axis=-1, keepdims=True)
        idx = jnp.min(jnp.where(work == m, el, N_EXPERTS), axis=-1, keepdims=True)
        hot = el == idx
        vals.append(m)
        idxs.append(idx)
        hots.append(hot)
        work = jnp.where(hot, NEG_BIG, work)
    exps = [jnp.exp(v - vals[0]) for v in vals]
    gden = exps[0] + exps[1] + exps[2] + exps[3]

    cnt = jnp.zeros((ts, N_EXPERTS), F32)
    for hot in hots:
        cnt = cnt + jnp.where(hot, 1.0, 0.0)
    tr = lax.broadcasted_iota(I32, (ts, ts), 0)
    tc = lax.broadcasted_iota(I32, (ts, ts), 1)
    tri = jnp.where(tc < tr, 1.0, 0.0).astype(BF16)
    before = jnp.dot(tri, cnt.astype(BF16), preferred_element_type=F32) + carry_ref[...]
    carry_ref[...] = carry_ref[...] + jnp.sum(cnt, axis=0, keepdims=True)
    cnt_ref[...] = carry_ref[...]

    lane = lax.broadcasted_iota(I32, (ts, LANES), 1)
    route = jnp.zeros((ts, LANES), I32)
    for kk in range(TOP_K):
        rank = jnp.sum(jnp.where(hots[kk], before, 0.0), axis=-1, keepdims=True).astype(I32)
        gate_bits = pltpu.bitcast(exps[kk] / gden, I32)
        route = jnp.where(lane == kk, idxs[kk], route)
        route = jnp.where(lane == TOP_K + kk, rank, route)
        route = jnp.where(lane == 2 * TOP_K + kk, gate_bits, route)
    route_ref[...] = route


def _merge(outs, lses, y_a, z, x2d, w_att_out, w_out, ln_g, ln_b, w_router, b_router):
    t = x2d.shape[0]
    ts = min(256, t)
    row = lambda v: v.reshape(1, -1)
    tile = lambda w: pl.BlockSpec((ts, w), lambda i: (i, 0))
    zcol = lambda c: pl.BlockSpec((ts, COLB), lambda i: (i, c))
    full = lambda shape: pl.BlockSpec(shape, lambda i: (0,) * len(shape))
    ga_c, gb_c = OFF_GA // COLB, OFF_GB // COLB
    return pl.pallas_call(
        _merge_body,
        grid=(t // ts,),
        in_specs=[tile(ATT_W)] * 3 + [tile(LANES)] * 3 + [tile(D_MODEL)]
                 + [zcol(ga_c), zcol(ga_c + 1), zcol(gb_c), zcol(gb_c + 1)]
                 + [tile(D_MODEL), full((ATT_W, D_MODEL)), full((D_MODEL, D_MODEL)),
                    full((1, D_MODEL)), full((1, D_MODEL)), full((D_MODEL, N_EXPERTS)),
                    full((1, N_EXPERTS))],
        out_specs=[tile(D_MODEL), tile(LANES), full((1, N_EXPERTS))],
        out_shape=[jax.ShapeDtypeStruct((t, D_MODEL), F32),
                   jax.ShapeDtypeStruct((t, LANES), I32),
                   jax.ShapeDtypeStruct((1, N_EXPERTS), F32)],
        scratch_shapes=[pltpu.VMEM((1, N_EXPERTS), F32)],
        compiler_params=_cparams(("arbitrary",)),
        name="merge_ln1_router",
    )(*outs, *lses, y_a, z, z, z, z, x2d, w_att_out.astype(BF16), w_out.astype(BF16),
      row(ln_g), row(ln_b), w_router, row(b_router))


DISPATCH_CHUNK = 128


def _dispatch_body(dest_ref, fill_ref, x_hbm, zero_hbm, xs_hbm, sem, fill_sem):
    c = pl.program_id(0)
    n = pl.num_programs(0)
    ch = DISPATCH_CHUNK
    rows = ch * TOP_K

    def tok(j, carry):
        tkn = c * ch + j
        for kk in range(TOP_K):
            pltpu.make_async_copy(x_hbm.at[pl.ds(tkn, 1)],
                                  xs_hbm.at[pl.ds(dest_ref[tkn * TOP_K + kk], 1)], sem).start()
        return carry
    lax.fori_loop(0, ch, tok, 0)

    def wait_chunk():
        pltpu.make_async_copy(x_hbm.at[pl.ds(0, rows)], xs_hbm.at[pl.ds(0, rows)], sem).wait()

    @pl.when(c > 0)
    def _():
        wait_chunk()

    @pl.when(c == n - 1)
    def _():
        wait_chunk()

        def zero_fill(wait):
            def act(lo, size):
                cp = pltpu.make_async_copy(zero_hbm.at[pl.ds(0, size)], xs_hbm.at[pl.ds(lo, size)],
                                           fill_sem)
                cp.wait() if wait else cp.start()

            def per_expert(e, carry):
                lo = fill_ref[2 * e]
                head = (-lo) & (SUBLANES - 1)
                for r in range(SUBLANES - 1):
                    @pl.when(r < head)
                    def _(r=r):
                        act(lo + r, 1)
                lo = pl.multiple_of(lo + head, SUBLANES)
                pad = fill_ref[2 * e + 1] - lo
                size = MOE_ROWS // 2
                while size >= SUBLANES:
                    @pl.when((pad & size) != 0)
                    def _(lo=lo, size=size):
                        act(lo, size)
                    lo = pl.multiple_of(lo + (pad & size), SUBLANES)
                    size //= 2
                return carry
            lax.fori_loop(0, N_EXPERTS, per_expert, 0)

            def per_block(b, carry):
                act(pl.multiple_of(b * MOE_ROWS, MOE_ROWS), MOE_ROWS)
                return carry
            lax.fori_loop(fill_ref[2 * N_EXPERTS], fill_ref[2 * N_EXPERTS + 1], per_block, 0)

        zero_fill(False)
        zero_fill(True)


def _dispatch(x1, dest_flat, fill_bounds, n_rows):
    t, d = x1.shape
    zero = jnp.zeros((MOE_ROWS, d), F32)
    return pl.pallas_call(
        _dispatch_body,
        grid_spec=pltpu.PrefetchScalarGridSpec(
            num_scalar_prefetch=2, grid=(t // DISPATCH_CHUNK,),
            in_specs=[pl.BlockSpec(memory_space=pl.ANY), pl.BlockSpec(memory_space=pl.ANY)],
            out_specs=pl.BlockSpec(memory_space=pl.ANY),
            scratch_shapes=[pltpu.SemaphoreType.DMA, pltpu.SemaphoreType.DMA]),
        out_shape=jax.ShapeDtypeStruct((n_rows, d), F32),
        compiler_params=_cparams(("arbitrary",)),
        name="moe_dispatch",
    )(dest_flat, fill_bounds, x1, zero)


def _expert_body(be_ref, nvb_ref, xs_ref, wg_ref, bg_ref, wu_ref, bu_ref, wd_ref, bd_ref, y_ref,
                 wg_s, wu_s, wd_s):
    j = pl.program_id(0)
    e = be_ref[j]
    e_prev = be_ref[jnp.maximum(j - 1, 0)]

    @pl.when((j == 0) | (e != e_prev))
    def _():
        wg_s[...] = wg_ref[0].astype(BF16)
        wu_s[...] = wu_ref[0].astype(BF16)
        wd_s[...] = wd_ref[0].astype(BF16)

    @pl.when(j >= nvb_ref[0])
    def _():
        y_ref[...] = jnp.zeros_like(y_ref)

    @pl.when(j < nvb_ref[0])
    def _():
        x = xs_ref[...].astype(BF16)
        g = jnp.dot(x, wg_s[...], preferred_element_type=F32) + bg_ref[0]
        up = jnp.dot(x, wu_s[...], preferred_element_type=F32) + bu_ref[0]
        g = jnp.minimum(g, SWIGLU_LIMIT)
        up = jnp.clip(up, -SWIGLU_LIMIT, SWIGLU_LIMIT)
        hdn = (up + 1.0) * (g * _sigmoid(SWIGLU_ALPHA * g))
        y_ref[...] = jnp.dot(hdn.astype(BF16), wd_s[...], preferred_element_type=F32) + bd_ref[0]


def _experts(xs, blk_expert, n_valid_blocks, w_gate, b_gate, w_up, b_up, w_down, b_down):
    p, d = xs.shape
    n_blk = p // MOE_ROWS
    ne = w_gate.shape[0]
    rows = lambda j, be, nv: (jnp.minimum(j, nv[0] - 1), 0)
    wspec = lambda a, b: pl.BlockSpec((1, a, b), lambda j, be, nv: (be[j], 0, 0))
    return pl.pallas_call(
        _expert_body,
        grid_spec=pltpu.PrefetchScalarGridSpec(
            num_scalar_prefetch=2, grid=(n_blk,),
            in_specs=[pl.BlockSpec((MOE_ROWS, d), rows),
                      wspec(d, D_FF), wspec(1, D_FF), wspec(d, D_FF), wspec(1, D_FF),
                      wspec(D_FF, d), wspec(1, d)],
            out_specs=pl.BlockSpec((MOE_ROWS, d), lambda j, be, nv: (j, 0)),
            scratch_shapes=[pltpu.VMEM((d, D_FF), BF16), pltpu.VMEM((d, D_FF), BF16),
                            pltpu.VMEM((D_FF, d), BF16)]),
        out_shape=jax.ShapeDtypeStruct((p, d), F32),
        compiler_params=_cparams(("arbitrary",)),
        name="moe_experts",
    )(blk_expert, n_valid_blocks, xs, w_gate, b_gate.reshape(ne, 1, D_FF), w_up,
      b_up.reshape(ne, 1, D_FF), w_down, b_down.reshape(ne, 1, d))


def _combine_body(dest_ref, yb_hbm, route_ref, x1_ref, p_ref, wple_ref, wpg_ref, bpg_ref,
                  g2_ref, b2_ref, g3_ref, b3_ref, o_ref, buf, sem):
    i = pl.program_id(0)
    n = pl.num_programs(0)
    ts = x1_ref.shape[0]

    def issue(tile, slot):
        def tok(j, carry):
            a = (tile * ts + j) * TOP_K
            for kk in range(TOP_K):
                pltpu.make_async_copy(yb_hbm.at[pl.ds(dest_ref[a + kk], 1)],
                                      buf.at[slot, kk, pl.ds(j, 1)], sem.at[slot]).start()
            return carry
        lax.fori_loop(0, ts, tok, 0)

    @pl.when(i == 0)
    def _():
        issue(0, 0)

    @pl.when(i + 1 < n)
    def _():
        issue(i + 1, (i + 1) % 2)

    slot = i % 2
    pltpu.make_async_copy(buf.at[slot], buf.at[slot], sem.at[slot]).wait()

    route = route_ref[...]
    y = jnp.zeros((ts, D_MODEL), F32)
    for kk in range(TOP_K):
        gate = pltpu.bitcast(route[:, 2 * TOP_K + kk:2 * TOP_K + kk + 1], F32)
        y = y + gate * buf[slot, kk]
    x2 = _layer_norm(ALPHA * x1_ref[...] + y, g2_ref[...], b2_ref[...])
    emb = jnp.dot(p_ref[...].astype(BF16), wple_ref[...], preferred_element_type=F32)
    gate_in = jnp.dot(x2.astype(BF16), wpg_ref[...], preferred_element_type=F32) + bpg_ref[...]
    ple = emb * _sigmoid(gate_in)
    o_ref[...] = _layer_norm(ALPHA * x2 + ple, g3_ref[...], b3_ref[...])


def _combine(dest_flat, yb, route, x1, p2d, w_ple, w_ple_gate, b_ple_gate, ln2_g, ln2_b, ln3_g, ln3_b):
    t, d = x1.shape
    ts = min(256, t)
    row = lambda v: v.reshape(1, -1)
    tile = lambda w: pl.BlockSpec((ts, w), lambda i, dst: (i, 0))
    full = lambda shape: pl.BlockSpec(shape, lambda i, dst: (0,) * len(shape))
    return pl.pallas_call(
        _combine_body,
        grid_spec=pltpu.PrefetchScalarGridSpec(
            num_scalar_prefetch=1, grid=(t // ts,),
            in_specs=[pl.BlockSpec(memory_space=pl.ANY), tile(LANES), tile(d), tile(PLE_DIM),
                      full((PLE_DIM, d)), full((d, d)), full((1, d)),
                      full((1, d)), full((1, d)), full((1, d)), full((1, d))],
            out_specs=tile(d),
            scratch_shapes=[pltpu.VMEM((2, TOP_K, ts, d), F32), pltpu.SemaphoreType.DMA((2,))]),
        out_shape=jax.ShapeDtypeStruct((t, d), F32),
        compiler_params=_cparams(("arbitrary",)),
        name="moe_combine_ple",
    )(dest_flat, yb, route, x1, p2d, w_ple.astype(BF16), w_ple_gate.astype(BF16), row(b_ple_gate),
      row(ln2_g), row(ln2_b), row(ln3_g), row(ln3_b))


def _routing_tables(route, counts, t):
    counts = counts.reshape(N_EXPERTS).astype(I32)
    padded = (counts + MOE_ROWS - 1) // MOE_ROWS * MOE_ROWS
    pend = jnp.cumsum(padded)
    pstart = pend - padded
    top_e = route[:, 0:TOP_K]
    rank = route[:, TOP_K:2 * TOP_K]
    dest = (pstart[top_e] + rank).reshape(t * TOP_K)
    n_blk = t * TOP_K // MOE_ROWS + N_EXPERTS
    n_used = pend[N_EXPERTS - 1] // MOE_ROWS
    fill = jnp.concatenate([jnp.stack([pstart + counts, pend], axis=1).reshape(2 * N_EXPERTS),
                            jnp.stack([n_used, jnp.asarray(n_blk, I32)])]).astype(I32)
    blk_expert = jnp.minimum(
        jnp.searchsorted(pend, jnp.arange(n_blk, dtype=I32) * MOE_ROWS, side='right'),
        N_EXPERTS - 1).astype(I32)
    n_valid = (pend[N_EXPERTS - 1] // MOE_ROWS).reshape(1).astype(I32)
    return dest, fill, blk_expert, n_valid, n_blk * MOE_ROWS


def _layer(x2d, p2d, batch, seq, w):
    t = batch * seq
    z = _in_proj(x2d, w['w_in'].astype(BF16))
    y_a = _lru_branch(z, batch, seq, w['conv_w'], w['conv_b'], w['w_rg'], w['b_rg'], w['w_ig'],
                      w['b_ig'], w['lru_lambda'], w['w_rnn_out'])
    outs, lses = [], []
    for gi in range(N_GROUPS):
        o, lse = _attention_group(z, batch, seq, gi)
        outs.append(o)
        lses.append(lse)
    x1, route, counts = _merge(outs, lses, y_a, z, x2d, w['w_att_out'], w['w_out'], w['ln1_g'],
                               w['ln1_b'], w['w_router'], w['b_router'])
    dest, fill, blk_expert, n_valid, n_rows = _routing_tables(route, counts, t)
    xs = _dispatch(x1, dest, fill, n_rows)
    yb = _experts(xs, blk_expert, n_valid, w['w_gate'], w['b_gate'], w['w_up'], w['b_up'],
                  w['w_down'], w['b_down'])
    return _combine(dest, yb, route, x1, p2d, w['w_ple'], w['w_ple_gate'], w['b_ple_gate'],
                    w['ln2_g'], w['ln2_b'], w['ln3_g'], w['ln3_b'])


_WEIGHT_NAMES = ('w_in', 'conv_w', 'conv_b', 'w_rg', 'b_rg', 'w_ig', 'b_ig', 'lru_lambda',
                 'w_rnn_out', 'w_att_out', 'w_out', 'ln1_g', 'ln1_b', 'w_router', 'b_router',
                 'w_gate', 'b_gate', 'w_up', 'b_up', 'w_down', 'b_down', 'ln2_g', 'ln2_b',
                 'w_ple', 'w_ple_gate', 'b_ple_gate', 'ln3_g', 'ln3_b')


def kernel(x, p, w_in, conv_w, conv_b, w_rg, b_rg, w_ig, b_ig, lru_lambda, w_rnn_out, w_att_out,
           w_out, ln1_g, ln1_b, w_router, b_router, w_gate, b_gate, w_up, b_up, w_down, b_down,
           ln2_g, ln2_b, w_ple, w_ple_gate, b_ple_gate, ln3_g, ln3_b):
    stacked = dict(zip(_WEIGHT_NAMES, (
        w_in, conv_w, conv_b, w_rg, b_rg, w_ig, b_ig, lru_lambda, w_rnn_out, w_att_out, w_out,
        ln1_g, ln1_b, w_router, b_router, w_gate, b_gate, w_up, b_up, w_down, b_down, ln2_g,
        ln2_b, w_ple, w_ple_gate, b_ple_gate, ln3_g, ln3_b)))
    batch, seq, d = x.shape
    x2d = x.reshape(batch * seq, d)
    for i in range(p.shape[0]):
        layer_w = {k: v[i] for k, v in stacked.items()}
        x2d = _layer(x2d, p[i].reshape(batch * seq, PLE_DIM), batch, seq, layer_w)
    return x2d.reshape(batch, seq, d)
```

```python
import functools

import jax
import jax.numpy as jnp
from jax import lax
from jax.experimental import pallas as pl
from jax.experimental.pallas import tpu as pltpu

F32 = jnp.float32
BF16 = jnp.bfloat16
I32 = jnp.int32

D_MODEL = 1024
DEPTH = 2
D_RNN = 1024
RNN_BLOCKS = 4
RNN_BW = D_RNN // RNN_BLOCKS
CONV_W = 4
LRU_C = 8.0
ATT_GROUPS = ((128, 1), (512, 4), (2048, 16))
N_GROUPS = len(ATT_GROUPS)
ATT_HEADS = 8
ATT_HEAD_DIM = 64
ATT_W = ATT_HEADS * ATT_HEAD_DIM
ATT_BLOCK = 128
OFF_XR = 0
OFF_YR = OFF_XR + D_RNN
OFF_Q = OFF_YR + D_RNN
OFF_K = OFF_Q + N_GROUPS * ATT_W
OFF_V = OFF_K + N_GROUPS * ATT_W
OFF_GA = OFF_V + N_GROUPS * ATT_W
OFF_GB = OFF_GA + D_MODEL
N_IN = OFF_GB + D_MODEL
N_EXPERTS = 32
TOP_K = 4
D_FF = D_MODEL
SWIGLU_ALPHA = 1.702
SWIGLU_LIMIT = 7.0
PLE_DIM = 256
ALPHA = (2.0 * DEPTH) ** 0.25
LN_EPS = 1e-5

LANES = 128
SUBLANES = 8
COLB = ATT_W
N_COLB = N_IN // COLB
MOE_ROWS = 256
VMEM_LIMIT = 56 * 1024 * 1024
NEG_BIG = -1e30


def _cparams(sem):
    return pltpu.CompilerParams(dimension_semantics=sem, vmem_limit_bytes=VMEM_LIMIT)


def _layer_norm(v, g, b):
    mu = jnp.mean(v, axis=-1, keepdims=True)
    c = v - mu
    var = jnp.mean(c * c, axis=-1, keepdims=True)
    return c * lax.rsqrt(var + LN_EPS) * g + b


def _sigmoid(v):
    return 1.0 / (1.0 + jnp.exp(-v))


def _inproj_body(x_ref, w_ref, o_ref):
    o_ref[...] = jnp.dot(x_ref[...].astype(BF16), w_ref[...],
                         preferred_element_type=F32).astype(o_ref.dtype)


def _in_proj(x2d, w_bf16):
    t, d = x2d.shape
    n = w_bf16.shape[1]
    tm = min(512, t)
    tn = n // 4
    return pl.pallas_call(
        _inproj_body,
        grid=(t // tm, n // tn),
        in_specs=[pl.BlockSpec((tm, d), lambda i, j: (i, 0)),
                  pl.BlockSpec((d, tn), lambda i, j: (0, j))],
        out_specs=pl.BlockSpec((tm, tn), lambda i, j: (i, j)),
        out_shape=jax.ShapeDtypeStruct((t, n), BF16),
        compiler_params=_cparams(("parallel", "parallel")),
        name="in_proj",
    )(x2d, w_bf16)


def _lru_body(xr_ref, yr_ref, cw_ref, cb_ref, wrg_ref, brg_ref, wig_ref, big_ref, lam_ref,
              wout_ref, o_ref, tail_ref, h_ref):
    s = pl.program_id(1)

    @pl.when(s == 0)
    def _():
        tail_ref[...] = jnp.zeros_like(tail_ref)
        h_ref[...] = jnp.zeros_like(h_ref)

    xr = xr_ref[...].astype(F32)
    ts = xr.shape[0]
    xe = jnp.concatenate([tail_ref[...], xr], axis=0)
    cw = cw_ref[...]
    xc = (cb_ref[...] + cw[3:4] * xr + cw[2:3] * xe[7:7 + ts]
          + cw[1:2] * xe[6:6 + ts] + cw[0:1] * xe[5:5 + ts])
    tail_ref[...] = xr[ts - 8:]

    xcb = xc.astype(BF16)

    def gate(w_ref, b_ref):
        parts = [jnp.dot(xcb[:, n * RNN_BW:(n + 1) * RNN_BW], w_ref[n],
                         preferred_element_type=F32) for n in range(RNN_BLOCKS)]
        return _sigmoid(jnp.concatenate(parts, axis=-1) + b_ref[...])

    r = gate(wrg_ref, brg_ref)
    i = gate(wig_ref, big_ref)
    nlam = -lam_ref[...]
    softplus = jnp.maximum(nlam, 0.0) + jnp.log1p(jnp.exp(-jnp.abs(nlam)))
    log_a = (-LRU_C) * r * softplus
    a = jnp.exp(log_a)
    bx = jnp.sqrt(jnp.tanh(-log_a) * (1.0 + a * a)) * (i * xc)

    row = lax.broadcasted_iota(I32, (ts, 1), 0)
    k = 1
    while k < ts:
        keep = row >= k
        a_prev = pltpu.roll(a, k, 0)
        b_prev = pltpu.roll(bx, k, 0)
        bx = jnp.where(keep, a * b_prev, 0.0) + bx
        a = jnp.where(keep, a * a_prev, a)
        k *= 2
    h = a * h_ref[0:1] + bx
    h_ref[0:1] = h[ts - 1:ts]

    yr = yr_ref[...].astype(F32)
    gelu = 0.5 * yr * (1.0 + jnp.tanh(0.7978845608028654 * (yr + 0.044715 * (yr * yr * yr))))
    o_ref[...] = jnp.dot((gelu * h).astype(BF16), wout_ref[...],
                         preferred_element_type=F32).astype(o_ref.dtype)


def _lru_branch(z, batch, seq, conv_w, conv_b, w_rg, b_rg, w_ig, b_ig, lam, w_rnn_out):
    ts = min(256, seq)
    ns = seq // ts
    c = D_RNN
    row = lambda v: v.reshape(1, c)
    full = lambda shape: pl.BlockSpec(shape, lambda b, s: (0,) * len(shape))
    return pl.pallas_call(
        _lru_body,
        grid=(batch, ns),
        in_specs=[pl.BlockSpec((ts, c), lambda b, s: (b * ns + s, OFF_XR // c)),
                  pl.BlockSpec((ts, c), lambda b, s: (b * ns + s, OFF_YR // c)),
                  full((CONV_W, c)), full((1, c)),
                  full((RNN_BLOCKS, RNN_BW, RNN_BW)), full((1, c)),
                  full((RNN_BLOCKS, RNN_BW, RNN_BW)), full((1, c)),
                  full((1, c)), full((c, D_MODEL))],
        out_specs=pl.BlockSpec((ts, D_MODEL), lambda b, s: (b * ns + s, 0)),
        out_shape=jax.ShapeDtypeStruct((batch * seq, D_MODEL), BF16),
        scratch_shapes=[pltpu.VMEM((8, c), F32), pltpu.VMEM((8, c), F32)],
        compiler_params=_cparams(("arbitrary", "arbitrary")),
        name="lru_branch",
    )(z, z, conv_w, row(conv_b), w_rg.astype(BF16), row(b_rg), w_ig.astype(BF16), row(b_ig),
      row(lam), w_rnn_out.astype(BF16))


def _attn_body(n_back, q_ref, kp_ref, kc_ref, vp_ref, vc_ref, o_ref, lse_ref):
    n = pl.program_id(2)
    blk = ATT_BLOCK
    q = q_ref[0]
    k = jnp.concatenate([kp_ref[0], kc_ref[0]], axis=0)
    v = jnp.concatenate([vp_ref[0], vc_ref[0]], axis=0)
    qi = lax.broadcasted_iota(I32, (blk, 2 * blk), 0)
    kj = lax.broadcasted_iota(I32, (blk, 2 * blk), 1)
    diff = blk + qi - kj
    valid = (diff >= 0) & (diff <= n_back) & ((kj >= blk) | (n > 0))
    lane = lax.broadcasted_iota(I32, (blk, LANES), 1)
    lse_tile = jnp.zeros((blk, LANES), F32)
    outs = []
    for h in range(ATT_HEADS):
        sl = slice(h * ATT_HEAD_DIM, (h + 1) * ATT_HEAD_DIM)
        s = lax.dot_general(q[:, sl], k[:, sl], (((1,), (1,)), ((), ())),
                            preferred_element_type=F32) * (ATT_HEAD_DIM ** -0.5)
        s = jnp.where(valid, s, NEG_BIG)
        m = jnp.max(s, axis=-1, keepdims=True)
        p = jnp.exp(s - m)
        l = jnp.sum(p, axis=-1, keepdims=True)
        o = jnp.dot(p.astype(BF16), v[:, sl], preferred_element_type=F32)
        outs.append(o / l)
        lse_tile = jnp.where(lane == h, m + jnp.log(l), lse_tile)
    o_ref[0] = jnp.concatenate(outs, axis=-1).astype(o_ref.dtype)
    lse_ref[0] = lse_tile


def _attention_group(z, batch, seq, gi):
    window, dil = ATT_GROUPS[gi]
    n_back = window // dil
    l = seq // dil
    nb = l // ATT_BLOCK
    zv = z.reshape(batch, l, dil * N_IN)
    qc, kc, vc = OFF_Q // COLB + gi, OFF_K // COLB + gi, OFF_V // COLB + gi
    blk = (1, ATT_BLOCK, COLB)

    def cur(c):
        return pl.BlockSpec(blk, lambda b, r, n: (b, n, r * N_COLB + c))

    def prev(c):
        return pl.BlockSpec(blk, lambda b, r, n: (b, jnp.maximum(n - 1, 0), r * N_COLB + c))

    o, lse = pl.pallas_call(
        functools.partial(_attn_body, n_back),
        grid=(batch, dil, nb),
        in_specs=[cur(qc), prev(kc), cur(kc), prev(vc), cur(vc)],
        out_specs=[pl.BlockSpec(blk, lambda b, r, n: (b, n, r)),
                   pl.BlockSpec((1, ATT_BLOCK, LANES), lambda b, r, n: (b, n, r))],
        out_shape=[jax.ShapeDtypeStruct((batch, l, dil * ATT_W), BF16),
                   jax.ShapeDtypeStruct((batch, l, dil * LANES), F32)],
        compiler_params=_cparams(("parallel", "parallel", "arbitrary")),
        name=f"attn_g{gi}",
    )(zv, zv, zv, zv, zv)
    return o.reshape(batch * seq, ATT_W), lse.reshape(batch * seq, LANES)


def _merge_body(o0_ref, o1_ref, o2_ref, l0_ref, l1_ref, l2_ref, ya_ref, ga0_ref, ga1_ref, gb0_ref,
                gb1_ref, x_ref, watt_ref, wout_ref, g_ref, b_ref, wr_ref, br_ref,
                x1_ref, route_ref, cnt_ref, carry_ref):
    i = pl.program_id(0)

    @pl.when(i == 0)
    def _():
        carry_ref[...] = jnp.zeros_like(carry_ref)

    ts = x_ref.shape[0]
    lses = [l0_ref[...], l1_ref[...], l2_ref[...]]
    mx = jnp.maximum(jnp.maximum(lses[0], lses[1]), lses[2])
    es = [jnp.exp(v - mx) for v in lses]
    den = es[0] + es[1] + es[2]
    er = lax.broadcasted_iota(I32, (LANES, ATT_W), 0)
    ec = lax.broadcasted_iota(I32, (LANES, ATT_W), 1)
    expand = jnp.where(ec // ATT_HEAD_DIM == er, 1.0, 0.0).astype(BF16)
    o = jnp.zeros((ts, ATT_W), F32)
    for e, o_ref in zip(es, (o0_ref, o1_ref, o2_ref)):
        w = e / den
        w_hi = w.astype(BF16)
        w_lo = (w - w_hi.astype(F32)).astype(BF16)
        wx = (jnp.dot(w_hi, expand, preferred_element_type=F32)
              + jnp.dot(w_lo, expand, preferred_element_type=F32))
        o = o + wx * o_ref[...].astype(F32)
    y_b = jnp.dot(o.astype(BF16), watt_ref[...], preferred_element_type=F32)
    ga = jnp.concatenate([ga0_ref[...], ga1_ref[...]], axis=-1).astype(F32)
    gb = jnp.concatenate([gb0_ref[...], gb1_ref[...]], axis=-1).astype(F32)
    merged = _sigmoid(ga) * ya_ref[...].astype(F32) + _sigmoid(gb) * y_b
    hmix = jnp.dot(merged.astype(BF16), wout_ref[...], preferred_element_type=F32)
    x1 = _layer_norm(ALPHA * x_ref[...] + hmix, g_ref[...], b_ref[...])
    x1_ref[...] = x1

    wr = wr_ref[...]
    wr_hi = wr.astype(BF16)
    wr_lo = (wr - wr_hi.astype(F32)).astype(BF16)
    x_hi = x1.astype(BF16)
    x_lo = (x1 - x_hi.astype(F32)).astype(BF16)
    logits = (jnp.dot(x_hi, wr_hi, preferred_element_type=F32)
              + jnp.dot(x_lo, wr_hi, preferred_element_type=F32)
              + jnp.dot(x_hi, wr_lo, preferred_element_type=F32)) + br_ref[...]

    el = lax.broadcasted_iota(I32, (ts, N_EXPERTS), 1)
    work = logits
    vals, idxs, hots = [], [], []
    for _ in range(TOP_K):
        m = jnp.max(work, axis=-1, keepdims=True)
        idx = jnp.min(jnp.where(work == m, el, N_EXPERTS), axis=-1, keepdims=True)
        hot = el == idx
        vals.append(m)
        idxs.append(idx)
        hots.append(hot)
        work = jnp.where(hot, NEG_BIG, work)
    exps = [jnp.exp(v - vals[0]) for v in vals]
    gden = exps[0] + exps[1] + exps[2] + exps[3]

    cnt = jnp.zeros((ts, N_EXPERTS), F32)
    for hot in hots:
        cnt = cnt + jnp.where(hot, 1.0, 0.0)
    tr = lax.broadcasted_iota(I32, (ts, ts), 0)
    tc = lax.broadcasted_iota(I32, (ts, ts), 1)
    tri = jnp.where(tc < tr, 1.0, 0.0).astype(BF16)
    before = jnp.dot(tri, cnt.astype(BF16), preferred_element_type=F32) + carry_ref[...]
    carry_ref[...] = carry_ref[...] + jnp.sum(cnt, axis=0, keepdims=True)
    cnt_ref[...] = carry_ref[...]

    lane = lax.broadcasted_iota(I32, (ts, LANES), 1)
    route = jnp.zeros((ts, LANES), I32)
    for kk in range(TOP_K):
        rank = jnp.sum(jnp.where(hots[kk], before, 0.0), axis=-1, keepdims=True).astype(I32)
        gate_bits = pltpu.bitcast(exps[kk] / gden, I32)
        route = jnp.where(lane == kk, idxs[kk], route)
        route = jnp.where(lane == TOP_K + kk, rank, route)
        route = jnp.where(lane == 2 * TOP_K + kk, gate_bits, route)
    route_ref[...] = route


def _merge(outs, lses, y_a, z, x2d, w_att_out, w_out, ln_g, ln_b, w_router, b_router):
    t = x2d.shape[0]
    ts = min(256, t)
    row = lambda v: v.reshape(1, -1)
    tile = lambda w: pl.BlockSpec((ts, w), lambda i: (i, 0))
    zcol = lambda c: pl.BlockSpec((ts, COLB), lambda i: (i, c))
    full = lambda shape: pl.BlockSpec(shape, lambda i: (0,) * len(shape))
    ga_c, gb_c = OFF_GA // COLB, OFF_GB // COLB
    return pl.pallas_call(
        _merge_body,
        grid=(t // ts,),
        in_specs=[tile(ATT_W)] * 3 + [tile(LANES)] * 3 + [tile(D_MODEL)]
                 + [zcol(ga_c), zcol(ga_c + 1), zcol(gb_c), zcol(gb_c + 1)]
                 + [tile(D_MODEL), full((ATT_W, D_MODEL)), full((D_MODEL, D_MODEL)),
                    full((1, D_MODEL)), full((1, D_MODEL)), full((D_MODEL, N_EXPERTS)),
                    full((1, N_EXPERTS))],
        out_specs=[tile(D_MODEL), tile(LANES), full((1, N_EXPERTS))],
        out_shape=[jax.ShapeDtypeStruct((t, D_MODEL), F32),
                   jax.ShapeDtypeStruct((t, LANES), I32),
                   jax.ShapeDtypeStruct((1, N_EXPERTS), F32)],
        scratch_shapes=[pltpu.VMEM((1, N_EXPERTS), F32)],
        compiler_params=_cparams(("arbitrary",)),
        name="merge_ln1_router",
    )(*outs, *lses, y_a, z, z, z, z, x2d, w_att_out.astype(BF16), w_out.astype(BF16),
      row(ln_g), row(ln_b), w_router, row(b_router))


DISPATCH_CHUNK = 256


def _dispatch_body(dest_ref, fill_ref, x_ref, zero_hbm, xs_hbm, stage, sem, fill_sem):
    c = pl.program_id(0)
    n = pl.num_programs(0)
    ch = DISPATCH_CHUNK
    slot = c % 2
    stage[slot] = x_ref[...]

    def tok(j, carry):
        tkn = c * ch + j
        for kk in range(TOP_K):
            pltpu.make_async_copy(stage.at[slot, pl.ds(j, 1)],
                                  xs_hbm.at[pl.ds(dest_ref[tkn * TOP_K + kk], 1)],
                                  sem.at[slot]).start()
        return carry
    lax.fori_loop(0, ch, tok, 0)

    def wait_chunk(s):
        for _ in range(ch * TOP_K // MOE_ROWS):
            pltpu.make_async_copy(zero_hbm, xs_hbm.at[pl.ds(0, MOE_ROWS)], sem.at[s]).wait()

    @pl.when(c > 0)
    def _():
        wait_chunk(1 - slot)

    @pl.when(c == n - 1)
    def _():
        wait_chunk(slot)

        def zero_fill(wait):
            def act(lo, size):
                cp = pltpu.make_async_copy(zero_hbm.at[pl.ds(0, size)], xs_hbm.at[pl.ds(lo, size)],
                                           fill_sem)
                cp.wait() if wait else cp.start()

            def per_expert(e, carry):
                lo = fill_ref[2 * e]
                head = (-lo) & (SUBLANES - 1)
                for r in range(SUBLANES - 1):
                    @pl.when(r < head)
                    def _(r=r):
                        act(lo + r, 1)
                lo = pl.multiple_of(lo + head, SUBLANES)
                pad = fill_ref[2 * e + 1] - lo
                size = MOE_ROWS // 2
                while size >= SUBLANES:
                    @pl.when((pad & size) != 0)
                    def _(lo=lo, size=size):
                        act(lo, size)
                    lo = pl.multiple_of(lo + (pad & size), SUBLANES)
                    size //= 2
                return carry
            lax.fori_loop(0, N_EXPERTS, per_expert, 0)

            def per_block(b, carry):
                act(pl.multiple_of(b * MOE_ROWS, MOE_ROWS), MOE_ROWS)
                return carry
            lax.fori_loop(fill_ref[2 * N_EXPERTS], fill_ref[2 * N_EXPERTS + 1], per_block, 0)

        zero_fill(False)
        zero_fill(True)


def _dispatch(x1, dest_flat, fill_bounds, n_rows):
    t, d = x1.shape
    ch = DISPATCH_CHUNK
    zero = jnp.zeros((MOE_ROWS, d), F32)
    return pl.pallas_call(
        _dispatch_body,
        grid_spec=pltpu.PrefetchScalarGridSpec(
            num_scalar_prefetch=2, grid=(t // ch,),
            in_specs=[pl.BlockSpec((ch, d), lambda c, dst, fl: (c, 0)),
                      pl.BlockSpec(memory_space=pl.ANY)],
            out_specs=pl.BlockSpec(memory_space=pl.ANY),
            scratch_shapes=[pltpu.VMEM((2, ch, d), F32), pltpu.SemaphoreType.DMA((2,)),
                            pltpu.SemaphoreType.DMA]),
        out_shape=jax.ShapeDtypeStruct((n_rows, d), F32),
        compiler_params=_cparams(("arbitrary",)),
        name="moe_dispatch",
    )(dest_flat, fill_bounds, x1, zero)


def _expert_body(be_ref, nvb_ref, xs_ref, wg_ref, bg_ref, wu_ref, bu_ref, wd_ref, bd_ref, y_ref,
                 wg_s, wu_s, wd_s):
    j = pl.program_id(0)
    e = be_ref[j]
    e_prev = be_ref[jnp.maximum(j - 1, 0)]

    @pl.when((j == 0) | (e != e_prev))
    def _():
        wg_s[...] = wg_ref[0].astype(BF16)
        wu_s[...] = wu_ref[0].astype(BF16)
        wd_s[...] = wd_ref[0].astype(BF16)

    @pl.when(j >= nvb_ref[0])
    def _():
        y_ref[...] = jnp.zeros_like(y_ref)

    @pl.when(j < nvb_ref[0])
    def _():
        x = xs_ref[...].astype(BF16)
        g = jnp.dot(x, wg_s[...], preferred_element_type=F32) + bg_ref[0]
        up = jnp.dot(x, wu_s[...], preferred_element_type=F32) + bu_ref[0]
        g = jnp.minimum(g, SWIGLU_LIMIT)
        up = jnp.clip(up, -SWIGLU_LIMIT, SWIGLU_LIMIT)
        hdn = (up + 1.0) * (g * _sigmoid(SWIGLU_ALPHA * g))
        y_ref[...] = jnp.dot(hdn.astype(BF16), wd_s[...], preferred_element_type=F32) + bd_ref[0]


def _experts(xs, blk_expert, n_valid_blocks, w_gate, b_gate, w_up, b_up, w_down, b_down):
    p, d = xs.shape
    n_blk = p // MOE_ROWS
    ne = w_gate.shape[0]
    rows = lambda j, be, nv: (jnp.minimum(j, nv[0] - 1), 0)
    wspec = lambda a, b: pl.BlockSpec((1, a, b), lambda j, be, nv: (be[j], 0, 0))
    return pl.pallas_call(
        _expert_body,
        grid_spec=pltpu.PrefetchScalarGridSpec(
            num_scalar_prefetch=2, grid=(n_blk,),
            in_specs=[pl.BlockSpec((MOE_ROWS, d), rows),
                      wspec(d, D_FF), wspec(1, D_FF), wspec(d, D_FF), wspec(1, D_FF),
                      wspec(D_FF, d), wspec(1, d)],
            out_specs=pl.BlockSpec((MOE_ROWS, d), lambda j, be, nv: (j, 0)),
            scratch_shapes=[pltpu.VMEM((d, D_FF), BF16), pltpu.VMEM((d, D_FF), BF16),
                            pltpu.VMEM((D_FF, d), BF16)]),
        out_shape=jax.ShapeDtypeStruct((p, d), F32),
        compiler_params=_cparams(("arbitrary",)),
        name="moe_experts",
    )(blk_expert, n_valid_blocks, xs, w_gate, b_gate.reshape(ne, 1, D_FF), w_up,
      b_up.reshape(ne, 1, D_FF), w_down, b_down.reshape(ne, 1, d))


def _combine_body(dest_ref, yb_hbm, route_ref, x1_ref, p_ref, wple_ref, wpg_ref, bpg_ref,
                  g2_ref, b2_ref, g3_ref, b3_ref, o_ref, buf, sem):
    i = pl.program_id(0)
    n = pl.num_programs(0)
    ts = x1_ref.shape[0]

    def issue(tile, slot):
        def tok(j, carry):
            a = (tile * ts + j) * TOP_K
            for kk in range(TOP_K):
                pltpu.make_async_copy(yb_hbm.at[pl.ds(dest_ref[a + kk], 1)],
                                      buf.at[slot, kk, pl.ds(j, 1)], sem.at[slot]).start()
            return carry
        lax.fori_loop(0, ts, tok, 0)

    @pl.when(i == 0)
    def _():
        issue(0, 0)

    @pl.when(i + 1 < n)
    def _():
        issue(i + 1, (i + 1) % 2)

    slot = i % 2
    pltpu.make_async_copy(buf.at[slot], buf.at[slot], sem.at[slot]).wait()

    route = route_ref[...]
    y = jnp.zeros((ts, D_MODEL), F32)
    for kk in range(TOP_K):
        gate = pltpu.bitcast(route[:, 2 * TOP_K + kk:2 * TOP_K + kk + 1], F32)
        y = y + gate * buf[slot, kk]
    x2 = _layer_norm(ALPHA * x1_ref[...] + y, g2_ref[...], b2_ref[...])
    emb = jnp.dot(p_ref[...].astype(BF16), wple_ref[...], preferred_element_type=F32)
    gate_in = jnp.dot(x2.astype(BF16), wpg_ref[...], preferred_element_type=F32) + bpg_ref[...]
    ple = emb * _sigmoid(gate_in)
    o_ref[...] = _layer_norm(ALPHA * x2 + ple, g3_ref[...], b3_ref[...])


def _combine(dest_flat, yb, route, x1, p2d, w_ple, w_ple_gate, b_ple_gate, ln2_g, ln2_b, ln3_g, ln3_b):
    t, d = x1.shape
    ts = min(256, t)
    row = lambda v: v.reshape(1, -1)
    tile = lambda w: pl.BlockSpec((ts, w), lambda i, dst: (i, 0))
    full = lambda shape: pl.BlockSpec(shape, lambda i, dst: (0,) * len(shape))
    return pl.pallas_call(
        _combine_body,
        grid_spec=pltpu.PrefetchScalarGridSpec(
            num_scalar_prefetch=1, grid=(t // ts,),
            in_specs=[pl.BlockSpec(memory_space=pl.ANY), tile(LANES), tile(d), tile(PLE_DIM),
                      full((PLE_DIM, d)), full((d, d)), full((1, d)),
                      full((1, d)), full((1, d)), full((1, d)), full((1, d))],
            out_specs=tile(d),
            scratch_shapes=[pltpu.VMEM((2, TOP_K, ts, d), F32), pltpu.SemaphoreType.DMA((2,))]),
        out_shape=jax.ShapeDtypeStruct((t, d), F32),
        compiler_params=_cparams(("arbitrary",)),
        name="moe_combine_ple",
    )(dest_flat, yb, route, x1, p2d, w_ple.astype(BF16), w_ple_gate.astype(BF16), row(b_ple_gate),
      row(ln2_g), row(ln2_b), row(ln3_g), row(ln3_b))


def _routing_tables(route, counts, t):
    counts = counts.reshape(N_EXPERTS).astype(I32)
    padded = (counts + MOE_ROWS - 1) // MOE_ROWS * MOE_ROWS
    pend = jnp.cumsum(padded)
    pstart = pend - padded
    top_e = route[:, 0:TOP_K]
    rank = route[:, TOP_K:2 * TOP_K]
    dest = (pstart[top_e] + rank).reshape(t * TOP_K)
    n_blk = t * TOP_K // MOE_ROWS + N_EXPERTS
    n_used = pend[N_EXPERTS - 1] // MOE_ROWS
    fill = jnp.concatenate([jnp.stack([pstart + counts, pend], axis=1).reshape(2 * N_EXPERTS),
                            jnp.stack([n_used, jnp.asarray(n_blk, I32)])]).astype(I32)
    blk_expert = jnp.minimum(
        jnp.searchsorted(pend, jnp.arange(n_blk, dtype=I32) * MOE_ROWS, side='right'),
        N_EXPERTS - 1).astype(I32)
    n_valid = (pend[N_EXPERTS - 1] // MOE_ROWS).reshape(1).astype(I32)
    return dest, fill, blk_expert, n_valid, n_blk * MOE_ROWS


def _layer(x2d, p2d, batch, seq, w):
    t = batch * seq
    z = _in_proj(x2d, w['w_in'].astype(BF16))
    y_a = _lru_branch(z, batch, seq, w['conv_w'], w['conv_b'], w['w_rg'], w['b_rg'], w['w_ig'],
                      w['b_ig'], w['lru_lambda'], w['w_rnn_out'])
    outs, lses = [], []
    for gi in range(N_GROUPS):
        o, lse = _attention_group(z, batch, seq, gi)
        outs.append(o)
        lses.append(lse)
    x1, route, counts = _merge(outs, lses, y_a, z, x2d, w['w_att_out'], w['w_out'], w['ln1_g'],
                               w['ln1_b'], w['w_router'], w['b_router'])
    dest, fill, blk_expert, n_valid, n_rows = _routing_tables(route, counts, t)
    xs = _dispatch(x1, dest, fill, n_rows)
    yb = _experts(xs, blk_expert, n_valid, w['w_gate'], w['b_gate'], w['w_up'], w['b_up'],
                  w['w_down'], w['b_down'])
    return _combine(dest, yb, route, x1, p2d, w['w_ple'], w['w_ple_gate'], w['b_ple_gate'],
                    w['ln2_g'], w['ln2_b'], w['ln3_g'], w['ln3_b'])


_WEIGHT_NAMES = ('w_in', 'conv_w', 'conv_b', 'w_rg', 'b_rg', 'w_ig', 'b_ig', 'lru_lambda',
                 'w_rnn_out', 'w_att_out', 'w_out', 'ln1_g', 'ln1_b', 'w_router', 'b_router',
                 'w_gate', 'b_gate', 'w_up', 'b_up', 'w_down', 'b_down', 'ln2_g', 'ln2_b',
                 'w_ple', 'w_ple_gate', 'b_ple_gate', 'ln3_g', 'ln3_b')


def kernel(x, p, w_in, conv_w, conv_b, w_rg, b_rg, w_ig, b_ig, lru_lambda, w_rnn_out, w_att_out,
           w_out, ln1_g, ln1_b, w_router, b_router, w_gate, b_gate, w_up, b_up, w_down, b_down,
           ln2_g, ln2_b, w_ple, w_ple_gate, b_ple_gate, ln3_g, ln3_b):
    stacked = dict(zip(_WEIGHT_NAMES, (
        w_in, conv_w, conv_b, w_rg, b_rg, w_ig, b_ig, lru_lambda, w_rnn_out, w_att_out, w_out,
        ln1_g, ln1_b, w_router, b_router, w_gate, b_gate, w_up, b_up, w_down, b_down, ln2_g,
        ln2_b, w_ple, w_ple_gate, b_ple_gate, ln3_g, ln3_b)))
    batch, seq, d = x.shape
    x2d = x.reshape(batch * seq, d)
    for i in range(p.shape[0]):
        layer_w = {k: v[i] for k, v in stacked.items()}
        x2d = _layer(x2d, p[i].reshape(batch * seq, PLE_DIM), batch, seq, layer_w)
    return x2d.reshape(batch, seq, d)
```

```python
import functools

import jax
import jax.numpy as jnp
from jax import lax
from jax.experimental import pallas as pl
from jax.experimental.pallas import tpu as pltpu

F32 = jnp.float32
BF16 = jnp.bfloat16
I32 = jnp.int32

D_MODEL = 1024
DEPTH = 2
D_RNN = 1024
RNN_BLOCKS = 4
RNN_BW = D_RNN // RNN_BLOCKS
CONV_W = 4
LRU_C = 8.0
ATT_GROUPS = ((128, 1), (512, 4), (2048, 16))
N_GROUPS = len(ATT_GROUPS)
ATT_HEADS = 8
ATT_HEAD_DIM = 64
ATT_W = ATT_HEADS * ATT_HEAD_DIM
ATT_BLOCK = 128
OFF_XR = 0
OFF_YR = OFF_XR + D_RNN
OFF_Q = OFF_YR + D_RNN
OFF_K = OFF_Q + N_GROUPS * ATT_W
OFF_V = OFF_K + N_GROUPS * ATT_W
OFF_GA = OFF_V + N_GROUPS * ATT_W
OFF_GB = OFF_GA + D_MODEL
N_IN = OFF_GB + D_MODEL
N_EXPERTS = 32
TOP_K = 4
D_FF = D_MODEL
SWIGLU_ALPHA = 1.702
SWIGLU_LIMIT = 7.0
PLE_DIM = 256
ALPHA = (2.0 * DEPTH) ** 0.25
LN_EPS = 1e-5

LANES = 128
SUBLANES = 8
ROW_TILES = D_MODEL // LANES
COLB = ATT_W
N_COLB = N_IN // COLB
MOE_ROWS = 256
VMEM_LIMIT = 56 * 1024 * 1024
NEG_BIG = -1e30

assert ROW_TILES == SUBLANES


def _cparams(sem):
    return pltpu.CompilerParams(dimension_semantics=sem, vmem_limit_bytes=VMEM_LIMIT)


def _layer_spec(li, shape):
    return pl.BlockSpec((None,) + tuple(shape), lambda *_: (li,) + (0,) * len(shape))


def _layer_norm(v, g, b):
    mu = jnp.mean(v, axis=-1, keepdims=True)
    c = v - mu
    var = jnp.mean(c * c, axis=-1, keepdims=True)
    return c * lax.rsqrt(var + LN_EPS) * g + b


def _sigmoid(v):
    return 1.0 / (1.0 + jnp.exp(-v))


def _to_row_tiles(ref, m, rows):
    for c in range(ROW_TILES):
        ref[pl.ds(c, rows, stride=ROW_TILES), :] = m[:, c * LANES:(c + 1) * LANES]


def _from_row_tiles(ref, rows):
    return jnp.concatenate([ref[pl.ds(c, rows, stride=ROW_TILES), :] for c in range(ROW_TILES)],
                           axis=-1)


def _inproj_body(x_ref, w_ref, o_ref):
    o_ref[...] = jnp.dot(x_ref[...].astype(BF16), w_ref[...],
                         preferred_element_type=F32).astype(o_ref.dtype)


def _in_proj(x2d, w_bf16, li):
    t, d = x2d.shape
    n = w_bf16.shape[-1]
    tm = min(512, t)
    tn = n // 4
    return pl.pallas_call(
        _inproj_body,
        grid=(t // tm, n // tn),
        in_specs=[pl.BlockSpec((tm, d), lambda i, j: (i, 0)),
                  pl.BlockSpec((None, d, tn), lambda i, j: (li, 0, j))],
        out_specs=pl.BlockSpec((tm, tn), lambda i, j: (i, j)),
        out_shape=jax.ShapeDtypeStruct((t, n), BF16),
        compiler_params=_cparams(("parallel", "parallel")),
        name="in_proj",
    )(x2d, w_bf16)


def _lru_body(xr_ref, yr_ref, cw_ref, cb_ref, wrg_ref, brg_ref, wig_ref, big_ref, lam_ref,
              wout_ref, o_ref, tail_ref, h_ref):
    s = pl.program_id(1)

    @pl.when(s == 0)
    def _():
        tail_ref[...] = jnp.zeros_like(tail_ref)
        h_ref[...] = jnp.zeros_like(h_ref)

    xr = xr_ref[...].astype(F32)
    ts = xr.shape[0]
    xe = jnp.concatenate([tail_ref[...], xr], axis=0)
    cw = cw_ref[...]
    xc = (cb_ref[...] + cw[3:4] * xr + cw[2:3] * xe[7:7 + ts]
          + cw[1:2] * xe[6:6 + ts] + cw[0:1] * xe[5:5 + ts])
    tail_ref[...] = xr[ts - 8:]

    xcb = xc.astype(BF16)

    def gate(w_ref, b_ref):
        parts = [jnp.dot(xcb[:, n * RNN_BW:(n + 1) * RNN_BW], w_ref[n],
                         preferred_element_type=F32) for n in range(RNN_BLOCKS)]
        return _sigmoid(jnp.concatenate(parts, axis=-1) + b_ref[...])

    r = gate(wrg_ref, brg_ref)
    i = gate(wig_ref, big_ref)
    nlam = -lam_ref[...]
    softplus = jnp.maximum(nlam, 0.0) + jnp.log1p(jnp.exp(-jnp.abs(nlam)))
    log_a = (-LRU_C) * r * softplus
    a = jnp.exp(log_a)
    bx = jnp.sqrt(jnp.tanh(-log_a) * (1.0 + a * a)) * (i * xc)

    row = lax.broadcasted_iota(I32, (ts, 1), 0)
    k = 1
    while k < ts:
        keep = row >= k
        a_prev = pltpu.roll(a, k, 0)
        b_prev = pltpu.roll(bx, k, 0)
        bx = jnp.where(keep, a * b_prev, 0.0) + bx
        a = jnp.where(keep, a * a_prev, a)
        k *= 2
    h = a * h_ref[0:1] + bx
    h_ref[0:1] = h[ts - 1:ts]

    yr = yr_ref[...].astype(F32)
    gelu = 0.5 * yr * (1.0 + jnp.tanh(0.7978845608028654 * (yr + 0.044715 * (yr * yr * yr))))
    o_ref[...] = jnp.dot((gelu * h).astype(BF16), wout_ref[...],
                         preferred_element_type=F32).astype(o_ref.dtype)


def _lru_branch(z, batch, seq, li, conv_w, conv_b, w_rg, b_rg, w_ig, b_ig, lam, w_rnn_out):
    ts = min(256, seq)
    ns = seq // ts
    c = D_RNN
    rows = lambda v: v.reshape(v.shape[0], 1, c)
    return pl.pallas_call(
        _lru_body,
        grid=(batch, ns),
        in_specs=[pl.BlockSpec((ts, c), lambda b, s: (b * ns + s, OFF_XR // c)),
                  pl.BlockSpec((ts, c), lambda b, s: (b * ns + s, OFF_YR // c)),
                  _layer_spec(li, (CONV_W, c)), _layer_spec(li, (1, c)),
                  _layer_spec(li, (RNN_BLOCKS, RNN_BW, RNN_BW)), _layer_spec(li, (1, c)),
                  _layer_spec(li, (RNN_BLOCKS, RNN_BW, RNN_BW)), _layer_spec(li, (1, c)),
                  _layer_spec(li, (1, c)), _layer_spec(li, (c, D_MODEL))],
        out_specs=pl.BlockSpec((ts, D_MODEL), lambda b, s: (b * ns + s, 0)),
        out_shape=jax.ShapeDtypeStruct((batch * seq, D_MODEL), BF16),
        scratch_shapes=[pltpu.VMEM((8, c), F32), pltpu.VMEM((8, c), F32)],
        compiler_params=_cparams(("arbitrary", "arbitrary")),
        name="lru_branch",
    )(z, z, conv_w, rows(conv_b), w_rg, rows(b_rg), w_ig, rows(b_ig), rows(lam), w_rnn_out)


def _attn_body(n_back, q_ref, kp_ref, kc_ref, vp_ref, vc_ref, o_ref, lse_ref):
    n = pl.program_id(2)
    blk = ATT_BLOCK
    q = q_ref[0]
    k = jnp.concatenate([kp_ref[0], kc_ref[0]], axis=0)
    v = jnp.concatenate([vp_ref[0], vc_ref[0]], axis=0)
    qi = lax.broadcasted_iota(I32, (blk, 2 * blk), 0)
    kj = lax.broadcasted_iota(I32, (blk, 2 * blk), 1)
    diff = blk + qi - kj
    valid = (diff >= 0) & (diff <= n_back) & ((kj >= blk) | (n > 0))
    lane = lax.broadcasted_iota(I32, (blk, LANES), 1)
    lse_tile = jnp.zeros((blk, LANES), F32)
    outs = []
    for h in range(ATT_HEADS):
        sl = slice(h * ATT_HEAD_DIM, (h + 1) * ATT_HEAD_DIM)
        s = lax.dot_general(q[:, sl], k[:, sl], (((1,), (1,)), ((), ())),
                            preferred_element_type=F32) * (ATT_HEAD_DIM ** -0.5)
        s = jnp.where(valid, s, NEG_BIG)
        m = jnp.max(s, axis=-1, keepdims=True)
        p = jnp.exp(s - m)
        l = jnp.sum(p, axis=-1, keepdims=True)
        o = jnp.dot(p.astype(BF16), v[:, sl], preferred_element_type=F32)
        outs.append(o / l)
        lse_tile = jnp.where(lane == h, m + jnp.log(l), lse_tile)
    o_ref[0] = jnp.concatenate(outs, axis=-1).astype(o_ref.dtype)
    lse_ref[0] = lse_tile


def _attention_group(z, batch, seq, gi):
    window, dil = ATT_GROUPS[gi]
    n_back = window // dil
    l = seq // dil
    nb = l // ATT_BLOCK
    zv = z.reshape(batch, l, dil * N_IN)
    qc, kc, vc = OFF_Q // COLB + gi, OFF_K // COLB + gi, OFF_V // COLB + gi
    blk = (1, ATT_BLOCK, COLB)

    def cur(c):
        return pl.BlockSpec(blk, lambda b, r, n: (b, n, r * N_COLB + c))

    def prev(c):
        return pl.BlockSpec(blk, lambda b, r, n: (b, jnp.maximum(n - 1, 0), r * N_COLB + c))

    o, lse = pl.pallas_call(
        functools.partial(_attn_body, n_back),
        grid=(batch, dil, nb),
        in_specs=[cur(qc), prev(kc), cur(kc), prev(vc), cur(vc)],
        out_specs=[pl.BlockSpec(blk, lambda b, r, n: (b, n, r)),
                   pl.BlockSpec((1, ATT_BLOCK, LANES), lambda b, r, n: (b, n, r))],
        out_shape=[jax.ShapeDtypeStruct((batch, l, dil * ATT_W), BF16),
                   jax.ShapeDtypeStruct((batch, l, dil * LANES), F32)],
        compiler_params=_cparams(("parallel", "parallel", "arbitrary")),
        name=f"attn_g{gi}",
    )(zv, zv, zv, zv, zv)
    return o.reshape(batch * seq, ATT_W), lse.reshape(batch * seq, LANES)


def _merge_body(o0_ref, o1_ref, o2_ref, l0_ref, l1_ref, l2_ref, ya_ref, ga0_ref, ga1_ref, gb0_ref,
                gb1_ref, x_ref, watt_ref, wout_ref, g_ref, b_ref, wr_ref, br_ref,
                x1_ref, route_ref, cnt_ref, carry_ref):
    i = pl.program_id(0)

    @pl.when(i == 0)
    def _():
        carry_ref[...] = jnp.zeros_like(carry_ref)

    ts = x_ref.shape[0]
    lses = [l0_ref[...], l1_ref[...], l2_ref[...]]
    mx = jnp.maximum(jnp.maximum(lses[0], lses[1]), lses[2])
    es = [jnp.exp(v - mx) for v in lses]
    den = es[0] + es[1] + es[2]
    er = lax.broadcasted_iota(I32, (LANES, ATT_W), 0)
    ec = lax.broadcasted_iota(I32, (LANES, ATT_W), 1)
    expand = jnp.where(ec // ATT_HEAD_DIM == er, 1.0, 0.0).astype(BF16)
    o = jnp.zeros((ts, ATT_W), F32)
    for e, o_ref in zip(es, (o0_ref, o1_ref, o2_ref)):
        w = e / den
        w_hi = w.astype(BF16)
        w_lo = (w - w_hi.astype(F32)).astype(BF16)
        wx = (jnp.dot(w_hi, expand, preferred_element_type=F32)
              + jnp.dot(w_lo, expand, preferred_element_type=F32))
        o = o + wx * o_ref[...].astype(F32)
    y_b = jnp.dot(o.astype(BF16), watt_ref[...], preferred_element_type=F32)
    ga = jnp.concatenate([ga0_ref[...], ga1_ref[...]], axis=-1).astype(F32)
    gb = jnp.concatenate([gb0_ref[...], gb1_ref[...]], axis=-1).astype(F32)
    merged = _sigmoid(ga) * ya_ref[...].astype(F32) + _sigmoid(gb) * y_b
    hmix = jnp.dot(merged.astype(BF16), wout_ref[...], preferred_element_type=F32)
    x1 = _layer_norm(ALPHA * x_ref[...] + hmix, g_ref[...], b_ref[...])
    x1_ref[...] = x1

    wr = wr_ref[...]
    wr_hi = wr.astype(BF16)
    wr_lo = (wr - wr_hi.astype(F32)).astype(BF16)
    x_hi = x1.astype(BF16)
    x_lo = (x1 - x_hi.astype(F32)).astype(BF16)
    logits = (jnp.dot(x_hi, wr_hi, preferred_element_type=F32)
              + jnp.dot(x_lo, wr_hi, preferred_element_type=F32)
              + jnp.dot(x_hi, wr_lo, preferred_element_type=F32)) + br_ref[...]

    el = lax.broadcasted_iota(I32, (ts, N_EXPERTS), 1)
    work = logits
    vals, idxs, hots = [], [], []
    for _ in range(TOP_K):
        m = jnp.max(work, axis=-1, keepdims=True)
        idx = jnp.min(jnp.where(work == m, el, N_EXPERTS), axis=-1, keepdims=True)
        hot = el == idx
        vals.append(m)
        idxs.append(idx)
        hots.append(hot)
        work = jnp.where(hot, NEG_BIG, work)
    exps = [jnp.exp(v - vals[0]) for v in vals]
    gden = exps[0] + exps[1] + exps[2] + exps[3]

    cnt = jnp.zeros((ts, N_EXPERTS), F32)
    for hot in hots:
        cnt = cnt + jnp.where(hot, 1.0, 0.0)
    tr = lax.broadcasted_iota(I32, (ts, ts), 0)
    tc = lax.broadcasted_iota(I32, (ts, ts), 1)
    tri = jnp.where(tc < tr, 1.0, 0.0).astype(BF16)
    before = jnp.dot(tri, cnt.astype(BF16), preferred_element_type=F32) + carry_ref[...]
    carry_ref[...] = carry_ref[...] + jnp.sum(cnt, axis=0, keepdims=True)
    cnt_ref[...] = carry_ref[...]

    lane = lax.broadcasted_iota(I32, (ts, LANES), 1)
    route = jnp.zeros((ts, LANES), I32)
    for kk in range(TOP_K):
        rank = jnp.sum(jnp.where(hots[kk], before, 0.0), axis=-1, keepdims=True).astype(I32)
        gate_bits = pltpu.bitcast(exps[kk] / gden, I32)
        route = jnp.where(lane == kk, idxs[kk], route)
        route = jnp.where(lane == TOP_K + kk, rank, route)
        route = jnp.where(lane == 2 * TOP_K + kk, gate_bits, route)
    route_ref[...] = route


def _merge(outs, lses, y_a, z, x2d, li, w_att_out, w_out, ln_g, ln_b, w_router, b_router):
    t = x2d.shape[0]
    ts = min(256, t)
    rows = lambda v: v.reshape(v.shape[0], 1, v.shape[-1])
    tile = lambda w: pl.BlockSpec((ts, w), lambda i: (i, 0))
    zcol = lambda c: pl.BlockSpec((ts, COLB), lambda i: (i, c))
    ga_c, gb_c = OFF_GA // COLB, OFF_GB // COLB
    return pl.pallas_call(
        _merge_body,
        grid=(t // ts,),
        in_specs=[tile(ATT_W)] * 3 + [tile(LANES)] * 3 + [tile(D_MODEL)]
                 + [zcol(ga_c), zcol(ga_c + 1), zcol(gb_c), zcol(gb_c + 1)]
                 + [tile(D_MODEL), _layer_spec(li, (ATT_W, D_MODEL)),
                    _layer_spec(li, (D_MODEL, D_MODEL)), _layer_spec(li, (1, D_MODEL)),
                    _layer_spec(li, (1, D_MODEL)), _layer_spec(li, (D_MODEL, N_EXPERTS)),
                    _layer_spec(li, (1, N_EXPERTS))],
        out_specs=[tile(D_MODEL), tile(LANES), pl.BlockSpec((1, N_EXPERTS), lambda i: (0, 0))],
        out_shape=[jax.ShapeDtypeStruct((t, D_MODEL), F32),
                   jax.ShapeDtypeStruct((t, LANES), I32),
                   jax.ShapeDtypeStruct((1, N_EXPERTS), F32)],
        scratch_shapes=[pltpu.VMEM((1, N_EXPERTS), F32)],
        compiler_params=_cparams(("arbitrary",)),
        name="merge_ln1_router",
    )(*outs, *lses, y_a, z, z, z, z, x2d, w_att_out, w_out, rows(ln_g), rows(ln_b), w_router,
      rows(b_router))


DISPATCH_CHUNK = 256


def _dispatch_body(dest_ref, fill_ref, x_ref, zero_hbm, xs_hbm, stage0, stage1, sem, fill_sem):
    c = pl.program_id(0)
    n = pl.num_programs(0)
    ch = DISPATCH_CHUNK
    slot = c % 2

    def scatter(stage, s):
        _to_row_tiles(stage, x_ref[...], ch)

        def tok(j, carry):
            tkn = c * ch + j
            src = stage.at[pl.ds(pl.multiple_of(j * ROW_TILES, ROW_TILES), ROW_TILES)]
            for kk in range(TOP_K):
                row = dest_ref[tkn * TOP_K + kk]
                dst = xs_hbm.at[pl.ds(pl.multiple_of(row * ROW_TILES, ROW_TILES), ROW_TILES)]
                pltpu.make_async_copy(src, dst, sem.at[s]).start()
            return carry
        lax.fori_loop(0, ch, tok, 0)

    @pl.when(slot == 0)
    def _():
        scatter(stage0, 0)

    @pl.when(slot == 1)
    def _():
        scatter(stage1, 1)

    def wait_chunk(s):
        for _ in range(ch * TOP_K // MOE_ROWS):
            pltpu.make_async_copy(zero_hbm, xs_hbm.at[pl.ds(0, MOE_ROWS * ROW_TILES)],
                                  sem.at[s]).wait()

    @pl.when(c > 0)
    def _():
        wait_chunk(1 - slot)

    @pl.when(c == n - 1)
    def _():
        wait_chunk(slot)

        def zero_fill(wait):
            def act(lo, size):
                cp = pltpu.make_async_copy(
                    zero_hbm.at[pl.ds(0, size * ROW_TILES)],
                    xs_hbm.at[pl.ds(pl.multiple_of(lo * ROW_TILES, ROW_TILES), size * ROW_TILES)],
                    fill_sem)
                cp.wait() if wait else cp.start()

            def per_expert(e, carry):
                lo = fill_ref[2 * e]
                pad = fill_ref[2 * e + 1] - lo
                size = MOE_ROWS // 2
                while size >= 1:
                    @pl.when((pad & size) != 0)
                    def _(lo=lo, size=size):
                        act(lo, size)
                    lo = lo + (pad & size)
                    size //= 2
                return carry
            lax.fori_loop(0, N_EXPERTS, per_expert, 0)

            def per_block(b, carry):
                act(b * MOE_ROWS, MOE_ROWS)
                return carry
            lax.fori_loop(fill_ref[2 * N_EXPERTS], fill_ref[2 * N_EXPERTS + 1], per_block, 0)

        zero_fill(False)
        zero_fill(True)


def _dispatch(x1, dest_flat, fill_bounds, n_rows):
    t, d = x1.shape
    ch = DISPATCH_CHUNK
    zero = jnp.zeros((MOE_ROWS * ROW_TILES, LANES), F32)
    stage = pltpu.VMEM((ch * ROW_TILES, LANES), F32)
    return pl.pallas_call(
        _dispatch_body,
        grid_spec=pltpu.PrefetchScalarGridSpec(
            num_scalar_prefetch=2, grid=(t // ch,),
            in_specs=[pl.BlockSpec((ch, d), lambda c, dst, fl: (c, 0)),
                      pl.BlockSpec(memory_space=pl.ANY)],
            out_specs=pl.BlockSpec(memory_space=pl.ANY),
            scratch_shapes=[stage, stage, pltpu.SemaphoreType.DMA((2,)), pltpu.SemaphoreType.DMA]),
        out_shape=jax.ShapeDtypeStruct((n_rows * ROW_TILES, LANES), F32),
        compiler_params=_cparams(("arbitrary",)),
        name="moe_dispatch",
    )(dest_flat, fill_bounds, x1, zero)


def _expert_body(be_ref, nvb_ref, xs_ref, wg_ref, bg_ref, wu_ref, bu_ref, wd_ref, bd_ref, y_ref,
                 wg_s, wu_s, wd_s):
    j = pl.program_id(0)
    e = be_ref[j]
    e_prev = be_ref[jnp.maximum(j - 1, 0)]

    @pl.when((j == 0) | (e != e_prev))
    def _():
        wg_s[...] = wg_ref[...].astype(BF16)
        wu_s[...] = wu_ref[...].astype(BF16)
        wd_s[...] = wd_ref[...].astype(BF16)

    @pl.when(j >= nvb_ref[0])
    def _():
        y_ref[...] = jnp.zeros_like(y_ref)

    @pl.when(j < nvb_ref[0])
    def _():
        x = _from_row_tiles(xs_ref, MOE_ROWS).astype(BF16)
        g = jnp.dot(x, wg_s[...], preferred_element_type=F32) + bg_ref[...]
        up = jnp.dot(x, wu_s[...], preferred_element_type=F32) + bu_ref[...]
        g = jnp.minimum(g, SWIGLU_LIMIT)
        up = jnp.clip(up, -SWIGLU_LIMIT, SWIGLU_LIMIT)
        hdn = (up + 1.0) * (g * _sigmoid(SWIGLU_ALPHA * g))
        y = jnp.dot(hdn.astype(BF16), wd_s[...], preferred_element_type=F32) + bd_ref[...]
        _to_row_tiles(y_ref, y, MOE_ROWS)


def _experts(xs, blk_expert, n_valid_blocks, li, w_gate, b_gate, w_up, b_up, w_down, b_down):
    n_blk = xs.shape[0] // (MOE_ROWS * ROW_TILES)
    d = D_MODEL
    blk = (MOE_ROWS * ROW_TILES, LANES)
    wspec = lambda a, b: pl.BlockSpec((None, None, a, b), lambda j, be, nv: (li, be[j], 0, 0))
    bias = lambda v: v.reshape(v.shape[0], v.shape[1], 1, v.shape[2])
    return pl.pallas_call(
        _expert_body,
        grid_spec=pltpu.PrefetchScalarGridSpec(
            num_scalar_prefetch=2, grid=(n_blk,),
            in_specs=[pl.BlockSpec(blk, lambda j, be, nv: (jnp.minimum(j, nv[0] - 1), 0)),
                      wspec(d, D_FF), wspec(1, D_FF), wspec(d, D_FF), wspec(1, D_FF),
                      wspec(D_FF, d), wspec(1, d)],
            out_specs=pl.BlockSpec(blk, lambda j, be, nv: (j, 0)),
            scratch_shapes=[pltpu.VMEM((d, D_FF), BF16), pltpu.VMEM((d, D_FF), BF16),
                            pltpu.VMEM((D_FF, d), BF16)]),
        out_shape=jax.ShapeDtypeStruct(xs.shape, F32),
        compiler_params=_cparams(("arbitrary",)),
        name="moe_experts",
    )(blk_expert, n_valid_blocks, xs, w_gate, bias(b_gate), w_up, bias(b_up), w_down, bias(b_down))


def _combine_body(dest_ref, yb_hbm, route_ref, x1_ref, p_ref, wple_ref, wpg_ref, bpg_ref,
                  g2_ref, b2_ref, g3_ref, b3_ref, o_ref, buf0, buf1, y_s, sem):
    i = pl.program_id(0)
    n = pl.num_programs(0)
    ts = x1_ref.shape[0]

    def issue(tile, buf, s):
        def tok(j, carry):
            a = (tile * ts + j) * TOP_K
            for kk in range(TOP_K):
                row = dest_ref[a + kk]
                src = yb_hbm.at[pl.ds(pl.multiple_of(row * ROW_TILES, ROW_TILES), ROW_TILES)]
                dst = buf.at[kk, pl.ds(pl.multiple_of(j * ROW_TILES, ROW_TILES), ROW_TILES)]
                pltpu.make_async_copy(src, dst, sem.at[s]).start()
            return carry
        lax.fori_loop(0, ts, tok, 0)

    @pl.when(i == 0)
    def _():
        issue(0, buf0, 0)

    route = route_ref[...]
    gates = [pltpu.bitcast(route[:, 2 * TOP_K + kk:2 * TOP_K + kk + 1], F32)
             for kk in range(TOP_K)]

    def phase(buf, s, other, so):
        @pl.when(i + 1 < n)
        def _():
            issue(i + 1, other, so)

        pltpu.make_async_copy(buf, buf, sem.at[s]).wait()
        for c in range(ROW_TILES):
            acc = jnp.zeros((ts, LANES), F32)
            for kk in range(TOP_K):
                acc = acc + gates[kk] * buf[kk, pl.ds(c, ts, stride=ROW_TILES), :]
            y_s[:, c * LANES:(c + 1) * LANES] = acc

    @pl.when(i % 2 == 0)
    def _():
        phase(buf0, 0, buf1, 1)

    @pl.when(i % 2 == 1)
    def _():
        phase(buf1, 1, buf0, 0)

    x2 = _layer_norm(ALPHA * x1_ref[...] + y_s[...], g2_ref[...], b2_ref[...])
    emb = jnp.dot(p_ref[...].astype(BF16), wple_ref[...], preferred_element_type=F32)
    gate_in = jnp.dot(x2.astype(BF16), wpg_ref[...], preferred_element_type=F32) + bpg_ref[...]
    ple = emb * _sigmoid(gate_in)
    o_ref[...] = _layer_norm(ALPHA * x2 + ple, g3_ref[...], b3_ref[...])


def _combine(dest_flat, yb, route, x1, p3d, li, w_ple, w_ple_gate, b_ple_gate, ln2_g, ln2_b,
             ln3_g, ln3_b):
    t, d = x1.shape
    ts = min(256, t)
    rows = lambda v: v.reshape(v.shape[0], 1, v.shape[-1])
    tile = lambda w: pl.BlockSpec((ts, w), lambda i, dst: (i, 0))
    buf = pltpu.VMEM((TOP_K, ts * ROW_TILES, LANES), F32)
    return pl.pallas_call(
        _combine_body,
        grid_spec=pltpu.PrefetchScalarGridSpec(
            num_scalar_prefetch=1, grid=(t // ts,),
            in_specs=[pl.BlockSpec(memory_space=pl.ANY), tile(LANES), tile(d),
                      pl.BlockSpec((None, ts, PLE_DIM), lambda i, dst: (li, i, 0)),
                      _layer_spec(li, (PLE_DIM, d)), _layer_spec(li, (d, d)),
                      _layer_spec(li, (1, d)), _layer_spec(li, (1, d)), _layer_spec(li, (1, d)),
                      _layer_spec(li, (1, d)), _layer_spec(li, (1, d))],
            out_specs=tile(d),
            scratch_shapes=[buf, buf, pltpu.VMEM((ts, d), F32), pltpu.SemaphoreType.DMA((2,))]),
        out_shape=jax.ShapeDtypeStruct((t, d), F32),
        compiler_params=_cparams(("arbitrary",)),
        name="moe_combine_ple",
    )(dest_flat, yb, route, x1, p3d, w_ple, w_ple_gate, rows(b_ple_gate), rows(ln2_g), rows(ln2_b),
      rows(ln3_g), rows(ln3_b))


def _routing_tables(route, counts, t):
    counts = counts.reshape(N_EXPERTS).astype(I32)
    padded = (counts + MOE_ROWS - 1) // MOE_ROWS * MOE_ROWS
    pend = jnp.cumsum(padded)
    pstart = pend - padded
    top_e = route[:, 0:TOP_K]
    rank = route[:, TOP_K:2 * TOP_K]
    onehot = top_e[:, :, None] == jnp.arange(N_EXPERTS, dtype=I32)
    dest = (jnp.sum(jnp.where(onehot, pstart, 0), axis=-1) + rank).reshape(t * TOP_K)
    n_blk = t * TOP_K // MOE_ROWS + N_EXPERTS
    n_used = pend[N_EXPERTS - 1] // MOE_ROWS
    fill = jnp.concatenate([jnp.stack([pstart + counts, pend], axis=1).reshape(2 * N_EXPERTS),
                            jnp.stack([n_used, jnp.asarray(n_blk, I32)])]).astype(I32)
    blk_start = jnp.arange(n_blk, dtype=I32) * MOE_ROWS
    blk_expert = jnp.minimum(jnp.sum((pend[None, :] <= blk_start[:, None]).astype(I32), axis=1),
                             N_EXPERTS - 1)
    return dest, fill, blk_expert, n_used.reshape(1), n_blk * MOE_ROWS


def _layer(x2d, p3d, batch, seq, li, w):
    t = batch * seq
    z = _in_proj(x2d, w['w_in'], li)
    y_a = _lru_branch(z, batch, seq, li, w['conv_w'], w['conv_b'], w['w_rg'], w['b_rg'], w['w_ig'],
                      w['b_ig'], w['lru_lambda'], w['w_rnn_out'])
    outs, lses = [], []
    for gi in range(N_GROUPS):
        o, lse = _attention_group(z, batch, seq, gi)
        outs.append(o)
        lses.append(lse)
    x1, route, counts = _merge(outs, lses, y_a, z, x2d, li, w['w_att_out'], w['w_out'], w['ln1_g'],
                               w['ln1_b'], w['w_router'], w['b_router'])
    dest, fill, blk_expert, n_valid, n_rows = _routing_tables(route, counts, t)
    xs = _dispatch(x1, dest, fill, n_rows)
    yb = _experts(xs, blk_expert, n_valid, li, w['w_gate'], w['b_gate'], w['w_up'], w['b_up'],
                  w['w_down'], w['b_down'])
    return _combine(dest, yb, route, x1, p3d, li, w['w_ple'], w['w_ple_gate'], w['b_ple_gate'],
                    w['ln2_g'], w['ln2_b'], w['ln3_g'], w['ln3_b'])


_WEIGHT_NAMES = ('w_in', 'conv_w', 'conv_b', 'w_rg', 'b_rg', 'w_ig', 'b_ig', 'lru_lambda',
                 'w_rnn_out', 'w_att_out', 'w_out', 'ln1_g', 'ln1_b', 'w_router', 'b_router',
                 'w_gate', 'b_gate', 'w_up', 'b_up', 'w_down', 'b_down', 'ln2_g', 'ln2_b',
                 'w_ple', 'w_ple_gate', 'b_ple_gate', 'ln3_g', 'ln3_b')
_BF16_WEIGHTS = ('w_in', 'w_rg', 'w_ig', 'w_rnn_out', 'w_att_out', 'w_out', 'w_ple', 'w_ple_gate')


def kernel(x, p, w_in, conv_w, conv_b, w_rg, b_rg, w_ig, b_ig, lru_lambda, w_rnn_out, w_att_out,
           w_out, ln1_g, ln1_b, w_router, b_router, w_gate, b_gate, w_up, b_up, w_down, b_down,
           ln2_g, ln2_b, w_ple, w_ple_gate, b_ple_gate, ln3_g, ln3_b):
    w = dict(zip(_WEIGHT_NAMES, (
        w_in, conv_w, conv_b, w_rg, b_rg, w_ig, b_ig, lru_lambda, w_rnn_out, w_att_out, w_out,
        ln1_g, ln1_b, w_router, b_router, w_gate, b_gate, w_up, b_up, w_down, b_down, ln2_g,
        ln2_b, w_ple, w_ple_gate, b_ple_gate, ln3_g, ln3_b)))
    for name in _BF16_WEIGHTS:
        w[name] = w[name].astype(BF16)
    batch, seq, d = x.shape
    depth = p.shape[0]
    x2d = x.reshape(batch * seq, d)
    p3d = p.reshape(depth, batch * seq, PLE_DIM)
    for li in range(depth):
        x2d = _layer(x2d, p3d, batch, seq, li, w)
    return x2d.reshape(batch, seq, d)
```

```python
import functools

import jax
import jax.numpy as jnp
from jax import lax
from jax.experimental import pallas as pl
from jax.experimental.pallas import tpu as pltpu

F32 = jnp.float32
BF16 = jnp.bfloat16
I32 = jnp.int32

D_MODEL = 1024
DEPTH = 2
D_RNN = 1024
RNN_BLOCKS = 4
RNN_BW = D_RNN // RNN_BLOCKS
CONV_W = 4
LRU_C = 8.0
ATT_GROUPS = ((128, 1), (512, 4), (2048, 16))
N_GROUPS = len(ATT_GROUPS)
ATT_HEADS = 8
ATT_HEAD_DIM = 64
ATT_W = ATT_HEADS * ATT_HEAD_DIM
ATT_BLOCK = 128
OFF_XR = 0
OFF_YR = OFF_XR + D_RNN
OFF_Q = OFF_YR + D_RNN
OFF_K = OFF_Q + N_GROUPS * ATT_W
OFF_V = OFF_K + N_GROUPS * ATT_W
OFF_GA = OFF_V + N_GROUPS * ATT_W
OFF_GB = OFF_GA + D_MODEL
N_IN = OFF_GB + D_MODEL
N_EXPERTS = 32
TOP_K = 4
D_FF = D_MODEL
SWIGLU_ALPHA = 1.702
SWIGLU_LIMIT = 7.0
PLE_DIM = 256
ALPHA = (2.0 * DEPTH) ** 0.25
LN_EPS = 1e-5

LANES = 128
SUBLANES = 8
ROW_TILES = D_MODEL // LANES
COLB = ATT_W
N_COLB = N_IN // COLB
MOE_ROWS = 256
VMEM_LIMIT = 56 * 1024 * 1024
NEG_BIG = -1e30

assert ROW_TILES == SUBLANES


def _cparams(sem):
    return pltpu.CompilerParams(dimension_semantics=sem, vmem_limit_bytes=VMEM_LIMIT)


def _layer_spec(li, shape):
    return pl.BlockSpec((None,) + tuple(shape), lambda *_: (li,) + (0,) * len(shape))


def _layer_norm(v, g, b):
    mu = jnp.mean(v, axis=-1, keepdims=True)
    c = v - mu
    var = jnp.mean(c * c, axis=-1, keepdims=True)
    return c * lax.rsqrt(var + LN_EPS) * g + b


def _sigmoid(v):
    return 1.0 / (1.0 + jnp.exp(-v))


def _to_row_tiles(ref, m, rows):
    for c in range(ROW_TILES):
        ref[pl.ds(c, rows, stride=ROW_TILES), :] = m[:, c * LANES:(c + 1) * LANES]


def _from_row_tiles(ref, rows):
    return jnp.concatenate([ref[pl.ds(c, rows, stride=ROW_TILES), :] for c in range(ROW_TILES)],
                           axis=-1)


PROJ_ROWS = 512
DENSE_W = 2 * D_RNN + 2 * D_MODEL


def _proj_dense_body(x_ref, wlo_ref, w0_ref, w1_ref, w2_ref, w3_ref, o_ref):
    x = x_ref[...].astype(BF16)
    lo = 2 * D_RNN
    o_ref[:, 0:lo] = jnp.dot(x, wlo_ref[...], preferred_element_type=F32).astype(o_ref.dtype)
    for c, w_ref in enumerate((w0_ref, w1_ref, w2_ref, w3_ref)):
        o_ref[:, lo + c * COLB:lo + (c + 1) * COLB] = jnp.dot(
            x, w_ref[...], preferred_element_type=F32).astype(o_ref.dtype)


def _proj_dense(x2d, w_in, li):
    t, d = x2d.shape
    tm = min(PROJ_ROWS, t)
    g0 = OFF_GA // COLB
    wcol = lambda c: pl.BlockSpec((None, d, COLB), lambda i: (li, 0, c))
    return pl.pallas_call(
        _proj_dense_body,
        grid=(t // tm,),
        in_specs=[pl.BlockSpec((tm, d), lambda i: (i, 0)),
                  pl.BlockSpec((None, d, 2 * D_RNN), lambda i: (li, 0, 0)),
                  wcol(g0), wcol(g0 + 1), wcol(g0 + 2), wcol(g0 + 3)],
        out_specs=pl.BlockSpec((tm, DENSE_W), lambda i: (i, 0)),
        out_shape=jax.ShapeDtypeStruct((t, DENSE_W), BF16),
        compiler_params=_cparams(("parallel",)),
        name="proj_dense",
    )(x2d, w_in, w_in, w_in, w_in, w_in)


def _proj_qkv_body(dil, x_ref, wq_ref, wk_ref, wv_ref, o_ref, res_s):
    x = x_ref[...].astype(BF16)
    rows = x.shape[0] // dil
    for c, w_ref in enumerate((wq_ref, wk_ref, wv_ref)):
        res = jnp.dot(x, w_ref[...], preferred_element_type=F32)
        cols = slice(c * ATT_W, (c + 1) * ATT_W)
        if dil == 1:
            o_ref[0, 0, :, cols] = res.astype(o_ref.dtype)
        else:
            for lc in range(ATT_W // LANES):
                res_s[lc] = res[:, lc * LANES:(lc + 1) * LANES]
            for r in range(dil):
                part = [res_s[lc, pl.ds(r, rows, stride=dil), :] for lc in range(ATT_W // LANES)]
                o_ref[0, r, :, cols] = jnp.concatenate(part, axis=-1).astype(o_ref.dtype)


def _proj_qkv(x2d, w_in, li, batch, seq, gi):
    d = x2d.shape[1]
    dil = ATT_GROUPS[gi][1]
    tm = min(PROJ_ROWS, seq)
    ns = seq // tm
    wcol = lambda off: pl.BlockSpec((None, d, ATT_W), lambda b, s: (li, 0, off // ATT_W + gi))
    return pl.pallas_call(
        functools.partial(_proj_qkv_body, dil),
        grid=(batch, ns),
        in_specs=[pl.BlockSpec((tm, d), lambda b, s: (b * ns + s, 0)),
                  wcol(OFF_Q), wcol(OFF_K), wcol(OFF_V)],
        out_specs=pl.BlockSpec((1, dil, tm // dil, 3 * ATT_W), lambda b, s: (b, 0, s, 0)),
        out_shape=jax.ShapeDtypeStruct((batch, dil, seq // dil, 3 * ATT_W), BF16),
        scratch_shapes=[pltpu.VMEM((ATT_W // LANES, tm, LANES), F32)],
        compiler_params=_cparams(("parallel", "parallel")),
        name=f"proj_qkv_g{gi}",
    )(x2d, w_in, w_in, w_in)


def _lru_body(xr_ref, yr_ref, cw_ref, cb_ref, wrg_ref, brg_ref, wig_ref, big_ref, lam_ref,
              wout_ref, o_ref, tail_ref, h_ref):
    s = pl.program_id(1)

    @pl.when(s == 0)
    def _():
        tail_ref[...] = jnp.zeros_like(tail_ref)
        h_ref[...] = jnp.zeros_like(h_ref)

    xr = xr_ref[...].astype(F32)
    ts = xr.shape[0]
    xe = jnp.concatenate([tail_ref[...], xr], axis=0)
    cw = cw_ref[...]
    xc = (cb_ref[...] + cw[3:4] * xr + cw[2:3] * xe[7:7 + ts]
          + cw[1:2] * xe[6:6 + ts] + cw[0:1] * xe[5:5 + ts])
    tail_ref[...] = xr[ts - 8:]

    xcb = xc.astype(BF16)

    def gate(w_ref, b_ref):
        parts = [jnp.dot(xcb[:, n * RNN_BW:(n + 1) * RNN_BW], w_ref[n],
                         preferred_element_type=F32) for n in range(RNN_BLOCKS)]
        return _sigmoid(jnp.concatenate(parts, axis=-1) + b_ref[...])

    r = gate(wrg_ref, brg_ref)
    i = gate(wig_ref, big_ref)
    nlam = -lam_ref[...]
    softplus = jnp.maximum(nlam, 0.0) + jnp.log1p(jnp.exp(-jnp.abs(nlam)))
    log_a = (-LRU_C) * r * softplus
    a = jnp.exp(log_a)
    bx = jnp.sqrt(jnp.tanh(-log_a) * (1.0 + a * a)) * (i * xc)

    row = lax.broadcasted_iota(I32, (ts, 1), 0)
    k = 1
    while k < ts:
        keep = row >= k
        a_prev = pltpu.roll(a, k, 0)
        b_prev = pltpu.roll(bx, k, 0)
        bx = jnp.where(keep, a * b_prev, 0.0) + bx
        a = jnp.where(keep, a * a_prev, a)
        k *= 2
    h = a * h_ref[0:1] + bx
    h_ref[0:1] = h[ts - 1:ts]

    yr = yr_ref[...].astype(F32)
    gelu = 0.5 * yr * (1.0 + jnp.tanh(0.7978845608028654 * (yr + 0.044715 * (yr * yr * yr))))
    o_ref[...] = jnp.dot((gelu * h).astype(BF16), wout_ref[...],
                         preferred_element_type=F32).astype(o_ref.dtype)


def _lru_branch(z, batch, seq, li, conv_w, conv_b, w_rg, b_rg, w_ig, b_ig, lam, w_rnn_out):
    ts = min(256, seq)
    ns = seq // ts
    c = D_RNN
    rows = lambda v: v.reshape(v.shape[0], 1, c)
    return pl.pallas_call(
        _lru_body,
        grid=(batch, ns),
        in_specs=[pl.BlockSpec((ts, c), lambda b, s: (b * ns + s, OFF_XR // c)),
                  pl.BlockSpec((ts, c), lambda b, s: (b * ns + s, OFF_YR // c)),
                  _layer_spec(li, (CONV_W, c)), _layer_spec(li, (1, c)),
                  _layer_spec(li, (RNN_BLOCKS, RNN_BW, RNN_BW)), _layer_spec(li, (1, c)),
                  _layer_spec(li, (RNN_BLOCKS, RNN_BW, RNN_BW)), _layer_spec(li, (1, c)),
                  _layer_spec(li, (1, c)), _layer_spec(li, (c, D_MODEL))],
        out_specs=pl.BlockSpec((ts, D_MODEL), lambda b, s: (b * ns + s, 0)),
        out_shape=jax.ShapeDtypeStruct((batch * seq, D_MODEL), BF16),
        scratch_shapes=[pltpu.VMEM((8, c), F32), pltpu.VMEM((8, c), F32)],
        compiler_params=_cparams(("arbitrary", "arbitrary")),
        name="lru_branch",
    )(z, z, conv_w, rows(conv_b), w_rg, rows(b_rg), w_ig, rows(b_ig), rows(lam), w_rnn_out)


def _attn_body(n_back, q_ref, kp_ref, kc_ref, vp_ref, vc_ref, o_ref, lse_ref):
    n = pl.program_id(2)
    blk = ATT_BLOCK
    q = q_ref[0, 0]
    k = jnp.concatenate([kp_ref[0, 0], kc_ref[0, 0]], axis=0)
    v = jnp.concatenate([vp_ref[0, 0], vc_ref[0, 0]], axis=0)
    qi = lax.broadcasted_iota(I32, (blk, 2 * blk), 0)
    kj = lax.broadcasted_iota(I32, (blk, 2 * blk), 1)
    diff = blk + qi - kj
    valid = (diff >= 0) & (diff <= n_back) & ((kj >= blk) | (n > 0))
    lane = lax.broadcasted_iota(I32, (blk, LANES), 1)
    lse_tile = jnp.zeros((blk, LANES), F32)
    outs = []
    for h in range(ATT_HEADS):
        sl = slice(h * ATT_HEAD_DIM, (h + 1) * ATT_HEAD_DIM)
        s = lax.dot_general(q[:, sl], k[:, sl], (((1,), (1,)), ((), ())),
                            preferred_element_type=F32) * (ATT_HEAD_DIM ** -0.5)
        s = jnp.where(valid, s, NEG_BIG)
        m = jnp.max(s, axis=-1, keepdims=True)
        p = jnp.exp(s - m)
        l = jnp.sum(p, axis=-1, keepdims=True)
        o = jnp.dot(p.astype(BF16), v[:, sl], preferred_element_type=F32)
        outs.append(o / l)
        lse_tile = jnp.where(lane == h, m + jnp.log(l), lse_tile)
    o_ref[0, 0] = jnp.concatenate(outs, axis=-1).astype(o_ref.dtype)
    lse_ref[0, 0] = lse_tile


def _attention_group(qkv, gi):
    batch, dil, l, _ = qkv.shape
    n_back = ATT_GROUPS[gi][0] // dil
    nb = l // ATT_BLOCK
    blk = (1, 1, ATT_BLOCK, ATT_W)

    def cur(c):
        return pl.BlockSpec(blk, lambda b, r, n: (b, r, n, c))

    def prev(c):
        return pl.BlockSpec(blk, lambda b, r, n: (b, r, jnp.maximum(n - 1, 0), c))

    return pl.pallas_call(
        functools.partial(_attn_body, n_back),
        grid=(batch, dil, nb),
        in_specs=[cur(0), prev(1), cur(1), prev(2), cur(2)],
        out_specs=[pl.BlockSpec(blk, lambda b, r, n: (b, r, n, 0)),
                   pl.BlockSpec((1, 1, ATT_BLOCK, LANES), lambda b, r, n: (b, r, n, 0))],
        out_shape=[jax.ShapeDtypeStruct((batch, dil, l, ATT_W), BF16),
                   jax.ShapeDtypeStruct((batch, dil, l, LANES), F32)],
        compiler_params=_cparams(("parallel", "parallel", "arbitrary")),
        name=f"attn_g{gi}",
    )(qkv, qkv, qkv, qkv, qkv)


def _merge_body(o0_ref, o1_ref, o2_ref, l0_ref, l1_ref, l2_ref, ya_ref, ga_ref, gb_ref,
                x_ref, watt_ref, wout_ref, g_ref, b_ref, wr_ref, br_ref,
                x1_ref, route_ref, cnt_ref, carry_ref, o_s, l_s):
    i = pl.program_id(0)

    @pl.when(i == 0)
    def _():
        carry_ref[...] = jnp.zeros_like(carry_ref)

    ts = x_ref.shape[0]

    def token_order(ref, scratch):
        dil = ref.shape[1]
        if dil == 1:
            return ref[0, 0].astype(F32)
        nl = ref.shape[3] // LANES
        for r in range(dil):
            v = ref[0, r].astype(F32)
            for lc in range(nl):
                scratch[lc, pl.ds(r, ts // dil, stride=dil), :] = v[:, lc * LANES:(lc + 1) * LANES]
        return jnp.concatenate([scratch[lc] for lc in range(nl)], axis=-1)

    group_o = [token_order(ref, o_s.at[gi]) for gi, ref in enumerate((o0_ref, o1_ref, o2_ref))]
    lses = [token_order(ref, l_s.at[gi]) for gi, ref in enumerate((l0_ref, l1_ref, l2_ref))]
    mx = jnp.maximum(jnp.maximum(lses[0], lses[1]), lses[2])
    es = [jnp.exp(v - mx) for v in lses]
    den = es[0] + es[1] + es[2]
    er = lax.broadcasted_iota(I32, (LANES, ATT_W), 0)
    ec = lax.broadcasted_iota(I32, (LANES, ATT_W), 1)
    expand = jnp.where(ec // ATT_HEAD_DIM == er, 1.0, 0.0).astype(BF16)
    o = jnp.zeros((ts, ATT_W), F32)
    for e, og in zip(es, group_o):
        w = e / den
        w_hi = w.astype(BF16)
        w_lo = (w - w_hi.astype(F32)).astype(BF16)
        wx = (jnp.dot(w_hi, expand, preferred_element_type=F32)
              + jnp.dot(w_lo, expand, preferred_element_type=F32))
        o = o + wx * og
    y_b = jnp.dot(o.astype(BF16), watt_ref[...], preferred_element_type=F32)
    ga = ga_ref[...].astype(F32)
    gb = gb_ref[...].astype(F32)
    merged = _sigmoid(ga) * ya_ref[...].astype(F32) + _sigmoid(gb) * y_b
    hmix = jnp.dot(merged.astype(BF16), wout_ref[...], preferred_element_type=F32)
    x1 = _layer_norm(ALPHA * x_ref[...] + hmix, g_ref[...], b_ref[...])
    x1_ref[...] = x1

    wr = wr_ref[...]
    wr_hi = wr.astype(BF16)
    wr_lo = (wr - wr_hi.astype(F32)).astype(BF16)
    x_hi = x1.astype(BF16)
    x_lo = (x1 - x_hi.astype(F32)).astype(BF16)
    logits = (jnp.dot(x_hi, wr_hi, preferred_element_type=F32)
              + jnp.dot(x_lo, wr_hi, preferred_element_type=F32)
              + jnp.dot(x_hi, wr_lo, preferred_element_type=F32)) + br_ref[...]

    el = lax.broadcasted_iota(I32, (ts, N_EXPERTS), 1)
    work = logits
    vals, idxs, hots = [], [], []
    for _ in range(TOP_K):
        m = jnp.max(work, axis=-1, keepdims=True)
        idx = jnp.min(jnp.where(work == m, el, N_EXPERTS), axis=-1, keepdims=True)
        hot = el == idx
        vals.append(m)
        idxs.append(idx)
        hots.append(hot)
        work = jnp.where(hot, NEG_BIG, work)
    exps = [jnp.exp(v - vals[0]) for v in vals]
    gden = exps[0] + exps[1] + exps[2] + exps[3]

    cnt = jnp.zeros((ts, N_EXPERTS), F32)
    for hot in hots:
        cnt = cnt + jnp.where(hot, 1.0, 0.0)
    tr = lax.broadcasted_iota(I32, (ts, ts), 0)
    tc = lax.broadcasted_iota(I32, (ts, ts), 1)
    tri = jnp.where(tc < tr, 1.0, 0.0).astype(BF16)
    before = jnp.dot(tri, cnt.astype(BF16), preferred_element_type=F32) + carry_ref[...]
    carry_ref[...] = carry_ref[...] + jnp.sum(cnt, axis=0, keepdims=True)
    cnt_ref[...] = carry_ref[...]

    lane = lax.broadcasted_iota(I32, (ts, LANES), 1)
    route = jnp.zeros((ts, LANES), I32)
    for kk in range(TOP_K):
        rank = jnp.sum(jnp.where(hots[kk], before, 0.0), axis=-1, keepdims=True).astype(I32)
        gate_bits = pltpu.bitcast(exps[kk] / gden, I32)
        route = jnp.where(lane == kk, idxs[kk], route)
        route = jnp.where(lane == TOP_K + kk, rank, route)
        route = jnp.where(lane == 2 * TOP_K + kk, gate_bits, route)
    route_ref[...] = route


def _merge(outs, lses, y_a, zd, x2d, seq, li, w_att_out, w_out, ln_g, ln_b, w_router, b_router):
    t = x2d.shape[0]
    ts = min(256, seq)
    ns = seq // ts
    rows = lambda v: v.reshape(v.shape[0], 1, v.shape[-1])
    tile = lambda w, c=0: pl.BlockSpec((ts, w), lambda i: (i, c))

    def by_residue(a):
        dil, w = a.shape[1], a.shape[3]
        return pl.BlockSpec((1, dil, ts // dil, w), lambda i: (i // ns, 0, i % ns, 0))

    return pl.pallas_call(
        _merge_body,
        grid=(t // ts,),
        in_specs=[by_residue(a) for a in (*outs, *lses)]
                 + [tile(D_MODEL), tile(D_MODEL, 2 * D_RNN // D_MODEL),
                    tile(D_MODEL, 2 * D_RNN // D_MODEL + 1), tile(D_MODEL),
                    _layer_spec(li, (ATT_W, D_MODEL)), _layer_spec(li, (D_MODEL, D_MODEL)),
                    _layer_spec(li, (1, D_MODEL)), _layer_spec(li, (1, D_MODEL)),
                    _layer_spec(li, (D_MODEL, N_EXPERTS)), _layer_spec(li, (1, N_EXPERTS))],
        out_specs=[tile(D_MODEL), tile(LANES), pl.BlockSpec((1, N_EXPERTS), lambda i: (0, 0))],
        out_shape=[jax.ShapeDtypeStruct((t, D_MODEL), F32),
                   jax.ShapeDtypeStruct((t, LANES), I32),
                   jax.ShapeDtypeStruct((1, N_EXPERTS), F32)],
        scratch_shapes=[pltpu.VMEM((1, N_EXPERTS), F32),
                        pltpu.VMEM((N_GROUPS, ATT_W // LANES, ts, LANES), F32),
                        pltpu.VMEM((N_GROUPS, 1, ts, LANES), F32)],
        compiler_params=_cparams(("arbitrary",)),
        name="merge_ln1_router",
    )(*outs, *lses, y_a, zd, zd, x2d, w_att_out, w_out, rows(ln_g), rows(ln_b), w_router,
      rows(b_router))


DISPATCH_CHUNK = 256


def _dispatch_body(dest_ref, fill_ref, x_ref, zero_hbm, xs_hbm, stage0, stage1, sem, fill_sem):
    c = pl.program_id(0)
    n = pl.num_programs(0)
    ch = DISPATCH_CHUNK
    slot = c % 2

    def scatter(stage, s):
        _to_row_tiles(stage, x_ref[...], ch)

        def tok(j, carry):
            tkn = c * ch + j
            src = stage.at[pl.ds(pl.multiple_of(j * ROW_TILES, ROW_TILES), ROW_TILES)]
            for kk in range(TOP_K):
                row = dest_ref[tkn * TOP_K + kk]
                dst = xs_hbm.at[pl.ds(pl.multiple_of(row * ROW_TILES, ROW_TILES), ROW_TILES)]
                pltpu.make_async_copy(src, dst, sem.at[s]).start()
            return carry
        lax.fori_loop(0, ch, tok, 0)

    @pl.when(slot == 0)
    def _():
        scatter(stage0, 0)

    @pl.when(slot == 1)
    def _():
        scatter(stage1, 1)

    def wait_chunk(s):
        for _ in range(ch * TOP_K // MOE_ROWS):
            pltpu.make_async_copy(zero_hbm, xs_hbm.at[pl.ds(0, MOE_ROWS * ROW_TILES)],
                                  sem.at[s]).wait()

    @pl.when(c > 0)
    def _():
        wait_chunk(1 - slot)

    @pl.when(c == n - 1)
    def _():
        wait_chunk(slot)

        def zero_fill(wait):
            def act(lo, size):
                cp = pltpu.make_async_copy(
                    zero_hbm.at[pl.ds(0, size * ROW_TILES)],
                    xs_hbm.at[pl.ds(pl.multiple_of(lo * ROW_TILES, ROW_TILES), size * ROW_TILES)],
                    fill_sem)
                cp.wait() if wait else cp.start()

            def per_expert(e, carry):
                lo = fill_ref[2 * e]
                pad = fill_ref[2 * e + 1] - lo
                size = MOE_ROWS // 2
                while size >= 1:
                    @pl.when((pad & size) != 0)
                    def _(lo=lo, size=size):
                        act(lo, size)
                    lo = lo + (pad & size)
                    size //= 2
                return carry
            lax.fori_loop(0, N_EXPERTS, per_expert, 0)

            def per_block(b, carry):
                act(b * MOE_ROWS, MOE_ROWS)
                return carry
            lax.fori_loop(fill_ref[2 * N_EXPERTS], fill_ref[2 * N_EXPERTS + 1], per_block, 0)

        zero_fill(False)
        zero_fill(True)


def _dispatch(x1, dest_flat, fill_bounds, n_rows):
    t, d = x1.shape
    ch = DISPATCH_CHUNK
    zero = jnp.zeros((MOE_ROWS * ROW_TILES, LANES), F32)
    stage = pltpu.VMEM((ch * ROW_TILES, LANES), F32)
    return pl.pallas_call(
        _dispatch_body,
        grid_spec=pltpu.PrefetchScalarGridSpec(
            num_scalar_prefetch=2, grid=(t // ch,),
            in_specs=[pl.BlockSpec((ch, d), lambda c, dst, fl: (c, 0)),
                      pl.BlockSpec(memory_space=pl.ANY)],
            out_specs=pl.BlockSpec(memory_space=pl.ANY),
            scratch_shapes=[stage, stage, pltpu.SemaphoreType.DMA((2,)), pltpu.SemaphoreType.DMA]),
        out_shape=jax.ShapeDtypeStruct((n_rows * ROW_TILES, LANES), F32),
        compiler_params=_cparams(("arbitrary",)),
        name="moe_dispatch",
    )(dest_flat, fill_bounds, x1, zero)


def _expert_body(be_ref, nvb_ref, xs_ref, wg_ref, bg_ref, wu_ref, bu_ref, wd_ref, bd_ref, y_ref,
                 wg_s, wu_s, wd_s):
    j = pl.program_id(0)
    e = be_ref[j]
    e_prev = be_ref[jnp.maximum(j - 1, 0)]

    @pl.when((j == 0) | (e != e_prev))
    def _():
        wg_s[...] = wg_ref[...].astype(BF16)
        wu_s[...] = wu_ref[...].astype(BF16)
        wd_s[...] = wd_ref[...].astype(BF16)

    @pl.when(j >= nvb_ref[0])
    def _():
        y_ref[...] = jnp.zeros_like(y_ref)

    @pl.when(j < nvb_ref[0])
    def _():
        x = _from_row_tiles(xs_ref, MOE_ROWS).astype(BF16)
        g = jnp.dot(x, wg_s[...], preferred_element_type=F32) + bg_ref[...]
        up = jnp.dot(x, wu_s[...], preferred_element_type=F32) + bu_ref[...]
        g = jnp.minimum(g, SWIGLU_LIMIT)
        up = jnp.clip(up, -SWIGLU_LIMIT, SWIGLU_LIMIT)
        hdn = (up + 1.0) * (g * _sigmoid(SWIGLU_ALPHA * g))
        y = jnp.dot(hdn.astype(BF16), wd_s[...], preferred_element_type=F32) + bd_ref[...]
        _to_row_tiles(y_ref, y, MOE_ROWS)


def _experts(xs, blk_expert, n_valid_blocks, li, w_gate, b_gate, w_up, b_up, w_down, b_down):
    n_blk = xs.shape[0] // (MOE_ROWS * ROW_TILES)
    d = D_MODEL
    blk = (MOE_ROWS * ROW_TILES, LANES)
    wspec = lambda a, b: pl.BlockSpec((None, None, a, b), lambda j, be, nv: (li, be[j], 0, 0))
    bias = lambda v: v.reshape(v.shape[0], v.shape[1], 1, v.shape[2])
    return pl.pallas_call(
        _expert_body,
        grid_spec=pltpu.PrefetchScalarGridSpec(
            num_scalar_prefetch=2, grid=(n_blk,),
            in_specs=[pl.BlockSpec(blk, lambda j, be, nv: (jnp.minimum(j, nv[0] - 1), 0)),
                      wspec(d, D_FF), wspec(1, D_FF), wspec(d, D_FF), wspec(1, D_FF),
                      wspec(D_FF, d), wspec(1, d)],
            out_specs=pl.BlockSpec(blk, lambda j, be, nv: (j, 0)),
            scratch_shapes=[pltpu.VMEM((d, D_FF), BF16), pltpu.VMEM((d, D_FF), BF16),
                            pltpu.VMEM((D_FF, d), BF16)]),
        out_shape=jax.ShapeDtypeStruct(xs.shape, F32),
        compiler_params=_cparams(("arbitrary",)),
        name="moe_experts",
    )(blk_expert, n_valid_blocks, xs, w_gate, bias(b_gate), w_up, bias(b_up), w_down, bias(b_down))


def _combine_body(dest_ref, yb_hbm, route_ref, x1_ref, p_ref, wple_ref, wpg_ref, bpg_ref,
                  g2_ref, b2_ref, g3_ref, b3_ref, o_ref, buf0, buf1, y_s, sem):
    i = pl.program_id(0)
    n = pl.num_programs(0)
    ts = x1_ref.shape[0]

    def issue(tile, buf, s):
        def tok(j, carry):
            a = (tile * ts + j) * TOP_K
            for kk in range(TOP_K):
                row = dest_ref[a + kk]
                src = yb_hbm.at[pl.ds(pl.multiple_of(row * ROW_TILES, ROW_TILES), ROW_TILES)]
                dst = buf.at[kk, pl.ds(pl.multiple_of(j * ROW_TILES, ROW_TILES), ROW_TILES)]
                pltpu.make_async_copy(src, dst, sem.at[s]).start()
            return carry
        lax.fori_loop(0, ts, tok, 0)

    @pl.when(i == 0)
    def _():
        issue(0, buf0, 0)

    route = route_ref[...]
    gates = [pltpu.bitcast(route[:, 2 * TOP_K + kk:2 * TOP_K + kk + 1], F32)
             for kk in range(TOP_K)]

    def phase(buf, s, other, so):
        @pl.when(i + 1 < n)
        def _():
            issue(i + 1, other, so)

        pltpu.make_async_copy(buf, buf, sem.at[s]).wait()
        for c in range(ROW_TILES):
            acc = jnp.zeros((ts, LANES), F32)
            for kk in range(TOP_K):
                acc = acc + gates[kk] * buf[kk, pl.ds(c, ts, stride=ROW_TILES), :]
            y_s[:, c * LANES:(c + 1) * LANES] = acc

    @pl.when(i % 2 == 0)
    def _():
        phase(buf0, 0, buf1, 1)

    @pl.when(i % 2 == 1)
    def _():
        phase(buf1, 1, buf0, 0)

    x2 = _layer_norm(ALPHA * x1_ref[...] + y_s[...], g2_ref[...], b2_ref[...])
    emb = jnp.dot(p_ref[...].astype(BF16), wple_ref[...], preferred_element_type=F32)
    gate_in = jnp.dot(x2.astype(BF16), wpg_ref[...], preferred_element_type=F32) + bpg_ref[...]
    ple = emb * _sigmoid(gate_in)
    o_ref[...] = _layer_norm(ALPHA * x2 + ple, g3_ref[...], b3_ref[...])


def _combine(dest_flat, yb, route, x1, p3d, li, w_ple, w_ple_gate, b_ple_gate, ln2_g, ln2_b,
             ln3_g, ln3_b):
    t, d = x1.shape
    ts = min(256, t)
    rows = lambda v: v.reshape(v.shape[0], 1, v.shape[-1])
    tile = lambda w: pl.BlockSpec((ts, w), lambda i, dst: (i, 0))
    buf = pltpu.VMEM((TOP_K, ts * ROW_TILES, LANES), F32)
    return pl.pallas_call(
        _combine_body,
        grid_spec=pltpu.PrefetchScalarGridSpec(
            num_scalar_prefetch=1, grid=(t // ts,),
            in_specs=[pl.BlockSpec(memory_space=pl.ANY), tile(LANES), tile(d),
                      pl.BlockSpec((None, ts, PLE_DIM), lambda i, dst: (li, i, 0)),
                      _layer_spec(li, (PLE_DIM, d)), _layer_spec(li, (d, d)),
                      _layer_spec(li, (1, d)), _layer_spec(li, (1, d)), _layer_spec(li, (1, d)),
                      _layer_spec(li, (1, d)), _layer_spec(li, (1, d))],
            out_specs=tile(d),
            scratch_shapes=[buf, buf, pltpu.VMEM((ts, d), F32), pltpu.SemaphoreType.DMA((2,))]),
        out_shape=jax.ShapeDtypeStruct((t, d), F32),
        compiler_params=_cparams(("arbitrary",)),
        name="moe_combine_ple",
    )(dest_flat, yb, route, x1, p3d, w_ple, w_ple_gate, rows(b_ple_gate), rows(ln2_g), rows(ln2_b),
      rows(ln3_g), rows(ln3_b))


def _routing_tables(route, counts, t):
    counts = counts.reshape(N_EXPERTS).astype(I32)
    padded = (counts + MOE_ROWS - 1) // MOE_ROWS * MOE_ROWS
    pend = jnp.cumsum(padded)
    pstart = pend - padded
    top_e = route[:, 0:TOP_K]
    rank = route[:, TOP_K:2 * TOP_K]
    onehot = top_e[:, :, None] == jnp.arange(N_EXPERTS, dtype=I32)
    dest = (jnp.sum(jnp.where(onehot, pstart, 0), axis=-1) + rank).reshape(t * TOP_K)
    n_blk = t * TOP_K // MOE_ROWS + N_EXPERTS
    n_used = pend[N_EXPERTS - 1] // MOE_ROWS
    fill = jnp.concatenate([jnp.stack([pstart + counts, pend], axis=1).reshape(2 * N_EXPERTS),
                            jnp.stack([n_used, jnp.asarray(n_blk, I32)])]).astype(I32)
    blk_start = jnp.arange(n_blk, dtype=I32) * MOE_ROWS
    blk_expert = jnp.minimum(jnp.sum((pend[None, :] <= blk_start[:, None]).astype(I32), axis=1),
                             N_EXPERTS - 1)
    return dest, fill, blk_expert, n_used.reshape(1), n_blk * MOE_ROWS


def _layer(x2d, p3d, batch, seq, li, w):
    t = batch * seq
    zd = _proj_dense(x2d, w['w_in'], li)
    y_a = _lru_branch(zd, batch, seq, li, w['conv_w'], w['conv_b'], w['w_rg'], w['b_rg'],
                      w['w_ig'], w['b_ig'], w['lru_lambda'], w['w_rnn_out'])
    outs, lses = [], []
    for gi in range(N_GROUPS):
        o, lse = _attention_group(_proj_qkv(x2d, w['w_in'], li, batch, seq, gi), gi)
        outs.append(o)
        lses.append(lse)
    x1, route, counts = _merge(outs, lses, y_a, zd, x2d, seq, li, w['w_att_out'], w['w_out'],
                               w['ln1_g'], w['ln1_b'], w['w_router'], w['b_router'])
    dest, fill, blk_expert, n_valid, n_rows = _routing_tables(route, counts, t)
    xs = _dispatch(x1, dest, fill, n_rows)
    yb = _experts(xs, blk_expert, n_valid, li, w['w_gate'], w['b_gate'], w['w_up'], w['b_up'],
                  w['w_down'], w['b_down'])
    return _combine(dest, yb, route, x1, p3d, li, w['w_ple'], w['w_ple_gate'], w['b_ple_gate'],
                    w['ln2_g'], w['ln2_b'], w['ln3_g'], w['ln3_b'])


_WEIGHT_NAMES = ('w_in', 'conv_w', 'conv_b', 'w_rg', 'b_rg', 'w_ig', 'b_ig', 'lru_lambda',
                 'w_rnn_out', 'w_att_out', 'w_out', 'ln1_g', 'ln1_b', 'w_router', 'b_router',
                 'w_gate', 'b_gate', 'w_up', 'b_up', 'w_down', 'b_down', 'ln2_g', 'ln2_b',
                 'w_ple', 'w_ple_gate', 'b_ple_gate', 'ln3_g', 'ln3_b')
_BF16_WEIGHTS = ('w_in', 'w_rg', 'w_ig', 'w_rnn_out', 'w_att_out', 'w_out', 'w_ple', 'w_ple_gate')


def kernel(x, p, w_in, conv_w, conv_b, w_rg, b_rg, w_ig, b_ig, lru_lambda, w_rnn_out, w_att_out,
           w_out, ln1_g, ln1_b, w_router, b_router, w_gate, b_gate, w_up, b_up, w_down, b_down,
           ln2_g, ln2_b, w_ple, w_ple_gate, b_ple_gate, ln3_g, ln3_b):
    w = dict(zip(_WEIGHT_NAMES, (
        w_in, conv_w, conv_b, w_rg, b_rg, w_ig, b_ig, lru_lambda, w_rnn_out, w_att_out, w_out,
        ln1_g, ln1_b, w_router, b_router, w_gate, b_gate, w_up, b_up, w_down, b_down, ln2_g,
        ln2_b, w_ple, w_ple_gate, b_ple_gate, ln3_g, ln3_b)))
    for name in _BF16_WEIGHTS:
        w[name] = w[name].astype(BF16)
    batch, seq, d = x.shape
    depth = p.shape[0]
    x2d = x.reshape(batch * seq, d)
    p3d = p.reshape(depth, batch * seq, PLE_DIM)
    for li in range(depth):
        x2d = _layer(x2d, p3d, batch, seq, li, w)
    return x2d.reshape(batch, seq, d)
```

```python
import functools

import jax
import jax.numpy as jnp
from jax import lax
from jax.experimental import pallas as pl
from jax.experimental.pallas import tpu as pltpu

F32 = jnp.float32
BF16 = jnp.bfloat16
I32 = jnp.int32

D_MODEL = 1024
DEPTH = 2
D_RNN = 1024
RNN_BLOCKS = 4
RNN_BW = D_RNN // RNN_BLOCKS
CONV_W = 4
LRU_C = 8.0
ATT_GROUPS = ((128, 1), (512, 4), (2048, 16))
N_GROUPS = len(ATT_GROUPS)
ATT_HEADS = 8
ATT_HEAD_DIM = 64
ATT_W = ATT_HEADS * ATT_HEAD_DIM
ATT_BLOCK = 128
OFF_XR = 0
OFF_YR = OFF_XR + D_RNN
OFF_Q = OFF_YR + D_RNN
OFF_K = OFF_Q + N_GROUPS * ATT_W
OFF_V = OFF_K + N_GROUPS * ATT_W
OFF_GA = OFF_V + N_GROUPS * ATT_W
OFF_GB = OFF_GA + D_MODEL
N_IN = OFF_GB + D_MODEL
N_EXPERTS = 32
TOP_K = 4
D_FF = D_MODEL
SWIGLU_ALPHA = 1.702
SWIGLU_LIMIT = 7.0
PLE_DIM = 256
ALPHA = (2.0 * DEPTH) ** 0.25
LN_EPS = 1e-5

LANES = 128
SUBLANES = 8
ROW_TILES = D_MODEL // LANES
COLB = ATT_W
N_COLB = N_IN // COLB
MOE_ROWS = 256
VMEM_LIMIT = 56 * 1024 * 1024
NEG_BIG = -1e30

assert ROW_TILES == SUBLANES


def _cparams(sem):
    return pltpu.CompilerParams(dimension_semantics=sem, vmem_limit_bytes=VMEM_LIMIT)


def _layer_spec(li, shape):
    return pl.BlockSpec((None,) + tuple(shape), lambda *_: (li,) + (0,) * len(shape))


def _layer_norm(v, g, b):
    mu = jnp.mean(v, axis=-1, keepdims=True)
    c = v - mu
    var = jnp.mean(c * c, axis=-1, keepdims=True)
    return c * lax.rsqrt(var + LN_EPS) * g + b


def _sigmoid(v):
    return 1.0 / (1.0 + jnp.exp(-v))


def _to_row_tiles(ref, m, rows):
    for c in range(ROW_TILES):
        ref[pl.ds(c, rows, stride=ROW_TILES), :] = m[:, c * LANES:(c + 1) * LANES]


def _from_row_tiles(ref, rows):
    return jnp.concatenate([ref[pl.ds(c, rows, stride=ROW_TILES), :] for c in range(ROW_TILES)],
                           axis=-1)


PROJ_ROWS = 512
DENSE_W = 2 * D_RNN + 2 * D_MODEL


def _proj_dense_body(x_ref, wlo_ref, w0_ref, w1_ref, w2_ref, w3_ref, o_ref):
    x = x_ref[...].astype(BF16)
    lo = 2 * D_RNN
    o_ref[:, 0:lo] = jnp.dot(x, wlo_ref[...], preferred_element_type=F32).astype(o_ref.dtype)
    for c, w_ref in enumerate((w0_ref, w1_ref, w2_ref, w3_ref)):
        o_ref[:, lo + c * COLB:lo + (c + 1) * COLB] = jnp.dot(
            x, w_ref[...], preferred_element_type=F32).astype(o_ref.dtype)


def _proj_dense(x2d, w_in, li):
    t, d = x2d.shape
    tm = min(PROJ_ROWS, t)
    g0 = OFF_GA // COLB
    wcol = lambda c: pl.BlockSpec((None, d, COLB), lambda i: (li, 0, c))
    return pl.pallas_call(
        _proj_dense_body,
        grid=(t // tm,),
        in_specs=[pl.BlockSpec((tm, d), lambda i: (i, 0)),
                  pl.BlockSpec((None, d, 2 * D_RNN), lambda i: (li, 0, 0)),
                  wcol(g0), wcol(g0 + 1), wcol(g0 + 2), wcol(g0 + 3)],
        out_specs=pl.BlockSpec((tm, DENSE_W), lambda i: (i, 0)),
        out_shape=jax.ShapeDtypeStruct((t, DENSE_W), BF16),
        compiler_params=_cparams(("parallel",)),
        name="proj_dense",
    )(x2d, w_in, w_in, w_in, w_in, w_in)


def _proj_qkv_body(dil, x_ref, wq_ref, wk_ref, wv_ref, o_ref, res_s):
    x = x_ref[...].astype(BF16)
    rows = x.shape[0] // dil
    for c, w_ref in enumerate((wq_ref, wk_ref, wv_ref)):
        res = jnp.dot(x, w_ref[...], preferred_element_type=F32)
        cols = slice(c * ATT_W, (c + 1) * ATT_W)
        if dil == 1:
            o_ref[0, 0, :, cols] = res.astype(o_ref.dtype)
        else:
            for lc in range(ATT_W // LANES):
                res_s[lc] = res[:, lc * LANES:(lc + 1) * LANES]
            for r in range(dil):
                part = [res_s[lc, pl.ds(r, rows, stride=dil), :] for lc in range(ATT_W // LANES)]
                o_ref[0, r, :, cols] = jnp.concatenate(part, axis=-1).astype(o_ref.dtype)


def _proj_qkv(x2d, w_in, li, batch, seq, gi):
    d = x2d.shape[1]
    dil = ATT_GROUPS[gi][1]
    tm = min(PROJ_ROWS, seq)
    ns = seq // tm
    wcol = lambda off: pl.BlockSpec((None, d, ATT_W), lambda b, s: (li, 0, off // ATT_W + gi))
    return pl.pallas_call(
        functools.partial(_proj_qkv_body, dil),
        grid=(batch, ns),
        in_specs=[pl.BlockSpec((tm, d), lambda b, s: (b * ns + s, 0)),
                  wcol(OFF_Q), wcol(OFF_K), wcol(OFF_V)],
        out_specs=pl.BlockSpec((1, dil, tm // dil, 3 * ATT_W), lambda b, s: (b, 0, s, 0)),
        out_shape=jax.ShapeDtypeStruct((batch, dil, seq // dil, 3 * ATT_W), BF16),
        scratch_shapes=[pltpu.VMEM((ATT_W // LANES, tm, LANES), F32)],
        compiler_params=_cparams(("parallel", "parallel")),
        name=f"proj_qkv_g{gi}",
    )(x2d, w_in, w_in, w_in)


def _lru_body(xr_ref, yr_ref, cw_ref, cb_ref, wrg_ref, brg_ref, wig_ref, big_ref, lam_ref,
              wout_ref, o_ref, tail_ref, h_ref):
    s = pl.program_id(1)

    @pl.when(s == 0)
    def _():
        tail_ref[...] = jnp.zeros_like(tail_ref)
        h_ref[...] = jnp.zeros_like(h_ref)

    xr = xr_ref[...].astype(F32)
    ts = xr.shape[0]
    xe = jnp.concatenate([tail_ref[...], xr], axis=0)
    cw = cw_ref[...]
    xc = (cb_ref[...] + cw[3:4] * xr + cw[2:3] * xe[7:7 + ts]
          + cw[1:2] * xe[6:6 + ts] + cw[0:1] * xe[5:5 + ts])
    tail_ref[...] = xr[ts - 8:]

    xcb = xc.astype(BF16)

    def gate(w_ref, b_ref):
        parts = [jnp.dot(xcb[:, n * RNN_BW:(n + 1) * RNN_BW], w_ref[n],
                         preferred_element_type=F32) for n in range(RNN_BLOCKS)]
        return _sigmoid(jnp.concatenate(parts, axis=-1) + b_ref[...])

    r = gate(wrg_ref, brg_ref)
    i = gate(wig_ref, big_ref)
    nlam = -lam_ref[...]
    softplus = jnp.maximum(nlam, 0.0) + jnp.log1p(jnp.exp(-jnp.abs(nlam)))
    log_a = (-LRU_C) * r * softplus
    a = jnp.exp(log_a)
    bx = jnp.sqrt(jnp.tanh(-log_a) * (1.0 + a * a)) * (i * xc)

    groups = ts // SUBLANES
    a3 = a.reshape(groups, SUBLANES, a.shape[-1])
    b3 = bx.reshape(groups, SUBLANES, a.shape[-1])
    sub = lax.broadcasted_iota(I32, (1, SUBLANES, 1), 1)
    k = 1
    while k < SUBLANES:
        keep = sub >= k
        a_prev = pltpu.roll(a3, k, 1)
        b_prev = pltpu.roll(b3, k, 1)
        b3 = jnp.where(keep, a3 * b_prev, 0.0) + b3
        a3 = jnp.where(keep, a3 * a_prev, a3)
        k *= 2
    carry = h_ref[0:1]
    hs = []
    for gidx in range(groups):
        hg = a3[gidx] * carry + b3[gidx]
        hs.append(hg)
        carry = hg[SUBLANES - 1:SUBLANES]
    h = jnp.concatenate(hs, axis=0)
    h_ref[0:1] = carry

    yr = yr_ref[...].astype(F32)
    gelu = 0.5 * yr * (1.0 + jnp.tanh(0.7978845608028654 * (yr + 0.044715 * (yr * yr * yr))))
    o_ref[...] = jnp.dot((gelu * h).astype(BF16), wout_ref[...],
                         preferred_element_type=F32).astype(o_ref.dtype)


def _lru_branch(z, batch, seq, li, conv_w, conv_b, w_rg, b_rg, w_ig, b_ig, lam, w_rnn_out):
    ts = min(256, seq)
    ns = seq // ts
    c = D_RNN
    rows = lambda v: v.reshape(v.shape[0], 1, c)
    return pl.pallas_call(
        _lru_body,
        grid=(batch, ns),
        in_specs=[pl.BlockSpec((ts, c), lambda b, s: (b * ns + s, OFF_XR // c)),
                  pl.BlockSpec((ts, c), lambda b, s: (b * ns + s, OFF_YR // c)),
                  _layer_spec(li, (CONV_W, c)), _layer_spec(li, (1, c)),
                  _layer_spec(li, (RNN_BLOCKS, RNN_BW, RNN_BW)), _layer_spec(li, (1, c)),
                  _layer_spec(li, (RNN_BLOCKS, RNN_BW, RNN_BW)), _layer_spec(li, (1, c)),
                  _layer_spec(li, (1, c)), _layer_spec(li, (c, D_MODEL))],
        out_specs=pl.BlockSpec((ts, D_MODEL), lambda b, s: (b * ns + s, 0)),
        out_shape=jax.ShapeDtypeStruct((batch * seq, D_MODEL), BF16),
        scratch_shapes=[pltpu.VMEM((8, c), F32), pltpu.VMEM((8, c), F32)],
        compiler_params=_cparams(("arbitrary", "arbitrary")),
        name="lru_branch",
    )(z, z, conv_w, rows(conv_b), w_rg, rows(b_rg), w_ig, rows(b_ig), rows(lam), w_rnn_out)


def _attn_body(n_back, q_ref, kp_ref, kc_ref, vp_ref, vc_ref, o_ref, lse_ref):
    n = pl.program_id(2)
    blk = ATT_BLOCK
    q = q_ref[0, 0]
    k = jnp.concatenate([kp_ref[0, 0], kc_ref[0, 0]], axis=0)
    v = jnp.concatenate([vp_ref[0, 0], vc_ref[0, 0]], axis=0)
    qi = lax.broadcasted_iota(I32, (blk, 2 * blk), 0)
    kj = lax.broadcasted_iota(I32, (blk, 2 * blk), 1)
    diff = blk + qi - kj
    valid = (diff >= 0) & (diff <= n_back) & ((kj >= blk) | (n > 0))
    lane = lax.broadcasted_iota(I32, (blk, LANES), 1)
    lse_tile = jnp.zeros((blk, LANES), F32)
    outs = []
    for h in range(ATT_HEADS):
        sl = slice(h * ATT_HEAD_DIM, (h + 1) * ATT_HEAD_DIM)
        s = lax.dot_general(q[:, sl], k[:, sl], (((1,), (1,)), ((), ())),
                            preferred_element_type=F32) * (ATT_HEAD_DIM ** -0.5)
        s = jnp.where(valid, s, NEG_BIG)
        m = jnp.max(s, axis=-1, keepdims=True)
        p = jnp.exp(s - m)
        l = jnp.sum(p, axis=-1, keepdims=True)
        o = jnp.dot(p.astype(BF16), v[:, sl], preferred_element_type=F32)
        outs.append(o / l)
        lse_tile = jnp.where(lane == h, m + jnp.log(l), lse_tile)
    o_ref[0, 0] = jnp.concatenate(outs, axis=-1).astype(o_ref.dtype)
    lse_ref[0, 0] = lse_tile


def _attention_group(qkv, gi):
    batch, dil, l, _ = qkv.shape
    n_back = ATT_GROUPS[gi][0] // dil
    nb = l // ATT_BLOCK
    blk = (1, 1, ATT_BLOCK, ATT_W)

    def cur(c):
        return pl.BlockSpec(blk, lambda b, r, n: (b, r, n, c))

    def prev(c):
        return pl.BlockSpec(blk, lambda b, r, n: (b, r, jnp.maximum(n - 1, 0), c))

    return pl.pallas_call(
        functools.partial(_attn_body, n_back),
        grid=(batch, dil, nb),
        in_specs=[cur(0), prev(1), cur(1), prev(2), cur(2)],
        out_specs=[pl.BlockSpec(blk, lambda b, r, n: (b, r, n, 0)),
                   pl.BlockSpec((1, 1, ATT_BLOCK, LANES), lambda b, r, n: (b, r, n, 0))],
        out_shape=[jax.ShapeDtypeStruct((batch, dil, l, ATT_W), BF16),
                   jax.ShapeDtypeStruct((batch, dil, l, LANES), F32)],
        compiler_params=_cparams(("parallel", "parallel", "arbitrary")),
        name=f"attn_g{gi}",
    )(qkv, qkv, qkv, qkv, qkv)


def _merge_body(o0_ref, o1_ref, o2_ref, l0_ref, l1_ref, l2_ref, ya_ref, ga_ref, gb_ref,
                x_ref, watt_ref, wout_ref, g_ref, b_ref, wr_ref, br_ref,
                x1_ref, x1t_ref, route_ref, cnt_ref, carry_ref, o_s, l_s):
    i = pl.program_id(0)

    @pl.when(i == 0)
    def _():
        carry_ref[...] = jnp.zeros_like(carry_ref)

    ts = x_ref.shape[0]

    def token_order(ref, scratch):
        dil = ref.shape[1]
        if dil == 1:
            return ref[0, 0].astype(F32)
        nl = ref.shape[3] // LANES
        for r in range(dil):
            v = ref[0, r].astype(F32)
            for lc in range(nl):
                scratch[lc, pl.ds(r, ts // dil, stride=dil), :] = v[:, lc * LANES:(lc + 1) * LANES]
        return jnp.concatenate([scratch[lc] for lc in range(nl)], axis=-1)

    group_o = [token_order(ref, o_s.at[gi]) for gi, ref in enumerate((o0_ref, o1_ref, o2_ref))]
    lses = [token_order(ref, l_s.at[gi]) for gi, ref in enumerate((l0_ref, l1_ref, l2_ref))]
    mx = jnp.maximum(jnp.maximum(lses[0], lses[1]), lses[2])
    es = [jnp.exp(v - mx) for v in lses]
    den = es[0] + es[1] + es[2]
    er = lax.broadcasted_iota(I32, (LANES, ATT_W), 0)
    ec = lax.broadcasted_iota(I32, (LANES, ATT_W), 1)
    expand = jnp.where(ec // ATT_HEAD_DIM == er, 1.0, 0.0).astype(BF16)
    o = jnp.zeros((ts, ATT_W), F32)
    for e, og in zip(es, group_o):
        w = e / den
        w_hi = w.astype(BF16)
        w_lo = (w - w_hi.astype(F32)).astype(BF16)
        wx = (jnp.dot(w_hi, expand, preferred_element_type=F32)
              + jnp.dot(w_lo, expand, preferred_element_type=F32))
        o = o + wx * og
    y_b = jnp.dot(o.astype(BF16), watt_ref[...], preferred_element_type=F32)
    ga = ga_ref[...].astype(F32)
    gb = gb_ref[...].astype(F32)
    merged = _sigmoid(ga) * ya_ref[...].astype(F32) + _sigmoid(gb) * y_b
    hmix = jnp.dot(merged.astype(BF16), wout_ref[...], preferred_element_type=F32)
    x1 = _layer_norm(ALPHA * x_ref[...] + hmix, g_ref[...], b_ref[...])
    x1_ref[...] = x1
    _to_row_tiles(x1t_ref, x1, ts)

    wr = wr_ref[...]
    wr_hi = wr.astype(BF16)
    wr_lo = (wr - wr_hi.astype(F32)).astype(BF16)
    x_hi = x1.astype(BF16)
    x_lo = (x1 - x_hi.astype(F32)).astype(BF16)
    logits = (jnp.dot(x_hi, wr_hi, preferred_element_type=F32)
              + jnp.dot(x_lo, wr_hi, preferred_element_type=F32)
              + jnp.dot(x_hi, wr_lo, preferred_element_type=F32)) + br_ref[...]

    el = lax.broadcasted_iota(I32, (ts, N_EXPERTS), 1)
    work = logits
    vals, idxs, hots = [], [], []
    for _ in range(TOP_K):
        m = jnp.max(work, axis=-1, keepdims=True)
        idx = jnp.min(jnp.where(work == m, el, N_EXPERTS), axis=-1, keepdims=True)
        hot = el == idx
        vals.append(m)
        idxs.append(idx)
        hots.append(hot)
        work = jnp.where(hot, NEG_BIG, work)
    exps = [jnp.exp(v - vals[0]) for v in vals]
    gden = exps[0] + exps[1] + exps[2] + exps[3]

    cnt = jnp.zeros((ts, N_EXPERTS), F32)
    for hot in hots:
        cnt = cnt + jnp.where(hot, 1.0, 0.0)
    tr = lax.broadcasted_iota(I32, (ts, ts), 0)
    tc = lax.broadcasted_iota(I32, (ts, ts), 1)
    tri = jnp.where(tc < tr, 1.0, 0.0).astype(BF16)
    before = jnp.dot(tri, cnt.astype(BF16), preferred_element_type=F32) + carry_ref[...]
    carry_ref[...] = carry_ref[...] + jnp.sum(cnt, axis=0, keepdims=True)
    cnt_ref[...] = carry_ref[...]

    lane = lax.broadcasted_iota(I32, (ts, LANES), 1)
    route = jnp.zeros((ts, LANES), I32)
    for kk in range(TOP_K):
        rank = jnp.sum(jnp.where(hots[kk], before, 0.0), axis=-1, keepdims=True).astype(I32)
        gate_bits = pltpu.bitcast(exps[kk] / gden, I32)
        route = jnp.where(lane == kk, idxs[kk], route)
        route = jnp.where(lane == TOP_K + kk, rank, route)
        route = jnp.where(lane == 2 * TOP_K + kk, gate_bits, route)
    route_ref[...] = route


def _merge(outs, lses, y_a, zd, x2d, seq, li, w_att_out, w_out, ln_g, ln_b, w_router, b_router):
    t = x2d.shape[0]
    ts = min(256, seq)
    ns = seq // ts
    rows = lambda v: v.reshape(v.shape[0], 1, v.shape[-1])
    tile = lambda w, c=0: pl.BlockSpec((ts, w), lambda i: (i, c))

    def by_residue(a):
        dil, w = a.shape[1], a.shape[3]
        return pl.BlockSpec((1, dil, ts // dil, w), lambda i: (i // ns, 0, i % ns, 0))

    return pl.pallas_call(
        _merge_body,
        grid=(t // ts,),
        in_specs=[by_residue(a) for a in (*outs, *lses)]
                 + [tile(D_MODEL), tile(D_MODEL, 2 * D_RNN // D_MODEL),
                    tile(D_MODEL, 2 * D_RNN // D_MODEL + 1), tile(D_MODEL),
                    _layer_spec(li, (ATT_W, D_MODEL)), _layer_spec(li, (D_MODEL, D_MODEL)),
                    _layer_spec(li, (1, D_MODEL)), _layer_spec(li, (1, D_MODEL)),
                    _layer_spec(li, (D_MODEL, N_EXPERTS)), _layer_spec(li, (1, N_EXPERTS))],
        out_specs=[tile(D_MODEL), pl.BlockSpec((ts * ROW_TILES, LANES), lambda i: (i, 0)),
                   tile(LANES), pl.BlockSpec((1, N_EXPERTS), lambda i: (0, 0))],
        out_shape=[jax.ShapeDtypeStruct((t, D_MODEL), F32),
                   jax.ShapeDtypeStruct((t * ROW_TILES, LANES), F32),
                   jax.ShapeDtypeStruct((t, LANES), I32),
                   jax.ShapeDtypeStruct((1, N_EXPERTS), F32)],
        scratch_shapes=[pltpu.VMEM((1, N_EXPERTS), F32),
                        pltpu.VMEM((N_GROUPS, ATT_W // LANES, ts, LANES), F32),
                        pltpu.VMEM((N_GROUPS, 1, ts, LANES), F32)],
        compiler_params=_cparams(("arbitrary",)),
        name="merge_ln1_router",
    )(*outs, *lses, y_a, zd, zd, x2d, w_att_out, w_out, rows(ln_g), rows(ln_b), w_router,
      rows(b_router))


INVERT_UNROLL = 8


def _invert_body(dest_ref, fill_ref, n_ref, inv_ref):
    def put(row, val):
        inv_ref[row] = val

    def zero_range(lo, hi):
        def z(row, carry):
            put(row, 0)
            return carry
        lax.fori_loop(lo, hi, z, 0)

    def per_expert(e, carry):
        zero_range(fill_ref[2 * e], fill_ref[2 * e + 1])
        return carry
    lax.fori_loop(0, N_EXPERTS, per_expert, 0)
    zero_range(fill_ref[2 * N_EXPERTS] * MOE_ROWS, fill_ref[2 * N_EXPERTS + 1] * MOE_ROWS)

    def chunk(c, carry):
        base = c * INVERT_UNROLL
        tok0 = c * (INVERT_UNROLL // TOP_K)
        for u in range(INVERT_UNROLL):
            put(dest_ref[base + u], tok0 + u // TOP_K)
        return carry
    lax.fori_loop(0, n_ref[0] // INVERT_UNROLL, chunk, 0)


def _invert(dest_flat, fill_bounds, n_rows):
    n = jnp.full((1,), dest_flat.shape[0], I32)
    return pl.pallas_call(
        _invert_body,
        grid_spec=pltpu.PrefetchScalarGridSpec(
            num_scalar_prefetch=3, grid=(1,), in_specs=[],
            out_specs=pl.BlockSpec(memory_space=pltpu.SMEM)),
        out_shape=jax.ShapeDtypeStruct((n_rows,), I32),
        compiler_params=_cparams(("arbitrary",)),
        name="moe_row_order",
    )(dest_flat, fill_bounds, n)


GATHER_UNROLL = 8


def _expert_body(be_ref, nvb_ref, inv_ref, x1t_hbm, wg_ref, bg_ref, wu_ref, bu_ref, wd_ref, bd_ref,
                 y_ref, wg_s, wu_s, wd_s, rows0, rows1, x_s, sem):
    j = pl.program_id(0)
    nvb = nvb_ref[0]
    e = be_ref[j]
    e_prev = be_ref[jnp.maximum(j - 1, 0)]

    def issue(blk, buf, s):
        def chunk(c, carry):
            for u in range(GATHER_UNROLL):
                r = c * GATHER_UNROLL + u
                tok = inv_ref[blk * MOE_ROWS + r]
                src = x1t_hbm.at[pl.ds(pl.multiple_of(tok * ROW_TILES, ROW_TILES), ROW_TILES)]
                dst = buf.at[pl.ds(pl.multiple_of(r * ROW_TILES, ROW_TILES), ROW_TILES)]
                pltpu.make_async_copy(src, dst, sem.at[s]).start()
            return carry
        lax.fori_loop(0, MOE_ROWS // GATHER_UNROLL, chunk, 0)

    @pl.when(j == 0)
    def _():
        issue(0, rows0, 0)

    @pl.when((j == 0) | (e != e_prev))
    def _():
        wg_s[...] = wg_ref[...].astype(BF16)
        wu_s[...] = wu_ref[...].astype(BF16)
        wd_s[...] = wd_ref[...].astype(BF16)

    def fetch(buf, s, other, so):
        @pl.when(j + 1 < nvb)
        def _():
            issue(j + 1, other, so)

        pltpu.make_async_copy(buf, buf, sem.at[s]).wait()
        x_s[...] = _from_row_tiles(buf, MOE_ROWS).astype(BF16)

    @pl.when((j < nvb) & (j % 2 == 0))
    def _():
        fetch(rows0, 0, rows1, 1)

    @pl.when((j < nvb) & (j % 2 == 1))
    def _():
        fetch(rows1, 1, rows0, 0)

    @pl.when(j >= nvb)
    def _():
        y_ref[...] = jnp.zeros_like(y_ref)

    @pl.when(j < nvb)
    def _():
        x = x_s[...]
        g = jnp.dot(x, wg_s[...], preferred_element_type=F32) + bg_ref[...]
        up = jnp.dot(x, wu_s[...], preferred_element_type=F32) + bu_ref[...]
        g = jnp.minimum(g, SWIGLU_LIMIT)
        up = jnp.clip(up, -SWIGLU_LIMIT, SWIGLU_LIMIT)
        hdn = (up + 1.0) * (g * _sigmoid(SWIGLU_ALPHA * g))
        y = jnp.dot(hdn.astype(BF16), wd_s[...], preferred_element_type=F32) + bd_ref[...]
        _to_row_tiles(y_ref, y, MOE_ROWS)


def _experts(x1t, inv, blk_expert, n_valid_blocks, li, w_gate, b_gate, w_up, b_up, w_down, b_down):
    n_rows = inv.shape[0]
    d = D_MODEL
    blk = (MOE_ROWS * ROW_TILES, LANES)
    wspec = lambda a, b: pl.BlockSpec((None, None, a, b), lambda j, be, nv, iv: (li, be[j], 0, 0))
    bias = lambda v: v.reshape(v.shape[0], v.shape[1], 1, v.shape[2])
    rows = pltpu.VMEM(blk, F32)
    return pl.pallas_call(
        _expert_body,
        grid_spec=pltpu.PrefetchScalarGridSpec(
            num_scalar_prefetch=3, grid=(n_rows // MOE_ROWS,),
            in_specs=[pl.BlockSpec(memory_space=pl.ANY),
                      wspec(d, D_FF), wspec(1, D_FF), wspec(d, D_FF), wspec(1, D_FF),
                      wspec(D_FF, d), wspec(1, d)],
            out_specs=pl.BlockSpec(blk, lambda j, be, nv, iv: (j, 0)),
            scratch_shapes=[pltpu.VMEM((d, D_FF), BF16), pltpu.VMEM((d, D_FF), BF16),
                            pltpu.VMEM((D_FF, d), BF16), rows, rows,
                            pltpu.VMEM((MOE_ROWS, d), BF16), pltpu.SemaphoreType.DMA((2,))]),
        out_shape=jax.ShapeDtypeStruct((n_rows * ROW_TILES, LANES), F32),
        compiler_params=_cparams(("arbitrary",)),
        name="moe_experts",
    )(blk_expert, n_valid_blocks, inv, x1t, w_gate, bias(b_gate), w_up, bias(b_up), w_down,
      bias(b_down))


def _combine_body(dest_ref, yb_hbm, route_ref, x1_ref, p_ref, wple_ref, wpg_ref, bpg_ref,
                  g2_ref, b2_ref, g3_ref, b3_ref, o_ref, buf0, buf1, y_s, sem):
    i = pl.program_id(0)
    n = pl.num_programs(0)
    ts = x1_ref.shape[0]

    def issue(tile, buf, s):
        def tok(j, carry):
            a = (tile * ts + j) * TOP_K
            for kk in range(TOP_K):
                row = dest_ref[a + kk]
                src = yb_hbm.at[pl.ds(pl.multiple_of(row * ROW_TILES, ROW_TILES), ROW_TILES)]
                dst = buf.at[kk, pl.ds(pl.multiple_of(j * ROW_TILES, ROW_TILES), ROW_TILES)]
                pltpu.make_async_copy(src, dst, sem.at[s]).start()
            return carry
        lax.fori_loop(0, ts, tok, 0)

    @pl.when(i == 0)
    def _():
        issue(0, buf0, 0)

    route = route_ref[...]
    gates = [pltpu.bitcast(route[:, 2 * TOP_K + kk:2 * TOP_K + kk + 1], F32)
             for kk in range(TOP_K)]

    def phase(buf, s, other, so):
        @pl.when(i + 1 < n)
        def _():
            issue(i + 1, other, so)

        pltpu.make_async_copy(buf, buf, sem.at[s]).wait()
        for c in range(ROW_TILES):
            acc = jnp.zeros((ts, LANES), F32)
            for kk in range(TOP_K):
                acc = acc + gates[kk] * buf[kk, pl.ds(c, ts, stride=ROW_TILES), :]
            y_s[:, c * LANES:(c + 1) * LANES] = acc

    @pl.when(i % 2 == 0)
    def _():
        phase(buf0, 0, buf1, 1)

    @pl.when(i % 2 == 1)
    def _():
        phase(buf1, 1, buf0, 0)

    x2 = _layer_norm(ALPHA * x1_ref[...] + y_s[...], g2_ref[...], b2_ref[...])
    emb = jnp.dot(p_ref[...].astype(BF16), wple_ref[...], preferred_element_type=F32)
    gate_in = jnp.dot(x2.astype(BF16), wpg_ref[...], preferred_element_type=F32) + bpg_ref[...]
    ple = emb * _sigmoid(gate_in)
    o_ref[...] = _layer_norm(ALPHA * x2 + ple, g3_ref[...], b3_ref[...])


def _combine(dest_flat, yb, route, x1, p3d, li, w_ple, w_ple_gate, b_ple_gate, ln2_g, ln2_b,
             ln3_g, ln3_b):
    t, d = x1.shape
    ts = min(256, t)
    rows = lambda v: v.reshape(v.shape[0], 1, v.shape[-1])
    tile = lambda w: pl.BlockSpec((ts, w), lambda i, dst: (i, 0))
    buf = pltpu.VMEM((TOP_K, ts * ROW_TILES, LANES), F32)
    return pl.pallas_call(
        _combine_body,
        grid_spec=pltpu.PrefetchScalarGridSpec(
            num_scalar_prefetch=1, grid=(t // ts,),
            in_specs=[pl.BlockSpec(memory_space=pl.ANY), tile(LANES), tile(d),
                      pl.BlockSpec((None, ts, PLE_DIM), lambda i, dst: (li, i, 0)),
                      _layer_spec(li, (PLE_DIM, d)), _layer_spec(li, (d, d)),
                      _layer_spec(li, (1, d)), _layer_spec(li, (1, d)), _layer_spec(li, (1, d)),
                      _layer_spec(li, (1, d)), _layer_spec(li, (1, d))],
            out_specs=tile(d),
            scratch_shapes=[buf, buf, pltpu.VMEM((ts, d), F32), pltpu.SemaphoreType.DMA((2,))]),
        out_shape=jax.ShapeDtypeStruct((t, d), F32),
        compiler_params=_cparams(("arbitrary",)),
        name="moe_combine_ple",
    )(dest_flat, yb, route, x1, p3d, w_ple, w_ple_gate, rows(b_ple_gate), rows(ln2_g), rows(ln2_b),
      rows(ln3_g), rows(ln3_b))


def _routing_tables(route, counts, t):
    counts = counts.reshape(N_EXPERTS).astype(I32)
    padded = (counts + MOE_ROWS - 1) // MOE_ROWS * MOE_ROWS
    pend = jnp.cumsum(padded)
    pstart = pend - padded
    top_e = route[:, 0:TOP_K]
    rank = route[:, TOP_K:2 * TOP_K]
    onehot = top_e[:, :, None] == jnp.arange(N_EXPERTS, dtype=I32)
    dest = (jnp.sum(jnp.where(onehot, pstart, 0), axis=-1) + rank).reshape(t * TOP_K)
    n_blk = t * TOP_K // MOE_ROWS + N_EXPERTS
    n_used = pend[N_EXPERTS - 1] // MOE_ROWS
    fill = jnp.concatenate([jnp.stack([pstart + counts, pend], axis=1).reshape(2 * N_EXPERTS),
                            jnp.stack([n_used, jnp.asarray(n_blk, I32)])]).astype(I32)
    blk_start = jnp.arange(n_blk, dtype=I32) * MOE_ROWS
    blk_expert = jnp.minimum(jnp.sum((pend[None, :] <= blk_start[:, None]).astype(I32), axis=1),
                             N_EXPERTS - 1)
    return dest, fill, blk_expert, n_used.reshape(1), n_blk * MOE_ROWS


def _layer(x2d, p3d, batch, seq, li, w):
    t = batch * seq
    zd = _proj_dense(x2d, w['w_in'], li)
    y_a = _lru_branch(zd, batch, seq, li, w['conv_w'], w['conv_b'], w['w_rg'], w['b_rg'],
                      w['w_ig'], w['b_ig'], w['lru_lambda'], w['w_rnn_out'])
    outs, lses = [], []
    for gi in range(N_GROUPS):
        o, lse = _attention_group(_proj_qkv(x2d, w['w_in'], li, batch, seq, gi), gi)
        outs.append(o)
        lses.append(lse)
    x1, x1t, route, counts = _merge(outs, lses, y_a, zd, x2d, seq, li, w['w_att_out'], w['w_out'],
                                    w['ln1_g'], w['ln1_b'], w['w_router'], w['b_router'])
    dest, fill, blk_expert, n_valid, n_rows = _routing_tables(route, counts, t)
    inv = _invert(dest, fill, n_rows)
    yb = _experts(x1t, inv, blk_expert, n_valid, li, w['w_gate'], w['b_gate'], w['w_up'],
                  w['b_up'], w['w_down'], w['b_down'])
    return _combine(dest, yb, route, x1, p3d, li, w['w_ple'], w['w_ple_gate'], w['b_ple_gate'],
                    w['ln2_g'], w['ln2_b'], w['ln3_g'], w['ln3_b'])


_WEIGHT_NAMES = ('w_in', 'conv_w', 'conv_b', 'w_rg', 'b_rg', 'w_ig', 'b_ig', 'lru_lambda',
                 'w_rnn_out', 'w_att_out', 'w_out', 'ln1_g', 'ln1_b', 'w_router', 'b_router',
                 'w_gate', 'b_gate', 'w_up', 'b_up', 'w_down', 'b_down', 'ln2_g', 'ln2_b',
                 'w_ple', 'w_ple_gate', 'b_ple_gate', 'ln3_g', 'ln3_b')
_BF16_WEIGHTS = ('w_in', 'w_rg', 'w_ig', 'w_rnn_out', 'w_att_out', 'w_out', 'w_ple', 'w_ple_gate')


def kernel(x, p, w_in, conv_w, conv_b, w_rg, b_rg, w_ig, b_ig, lru_lambda, w_rnn_out, w_att_out,
           w_out, ln1_g, ln1_b, w_router, b_router, w_gate, b_gate, w_up, b_up, w_down, b_down,
           ln2_g, ln2_b, w_ple, w_ple_gate, b_ple_gate, ln3_g, ln3_b):
    w = dict(zip(_WEIGHT_NAMES, (
        w_in, conv_w, conv_b, w_rg, b_rg, w_ig, b_ig, lru_lambda, w_rnn_out, w_att_out, w_out,
        ln1_g, ln1_b, w_router, b_router, w_gate, b_gate, w_up, b_up, w_down, b_down, ln2_g,
        ln2_b, w_ple, w_ple_gate, b_ple_gate, ln3_g, ln3_b)))
    for name in _BF16_WEIGHTS:
        w[name] = w[name].astype(BF16)
    batch, seq, d = x.shape
    depth = p.shape[0]
    x2d = x.reshape(batch * seq, d)
    p3d = p.reshape(depth, batch * seq, PLE_DIM)
    for li in range(depth):
        x2d = _layer(x2d, p3d, batch, seq, li, w)
    return x2d.reshape(batch, seq, d)
```

```python
import functools

import jax
import jax.numpy as jnp
from jax import lax
from jax.experimental import pallas as pl
from jax.experimental.pallas import tpu as pltpu

F32 = jnp.float32
BF16 = jnp.bfloat16
I32 = jnp.int32

D_MODEL = 1024
DEPTH = 2
D_RNN = 1024
RNN_BLOCKS = 4
RNN_BW = D_RNN // RNN_BLOCKS
CONV_W = 4
LRU_C = 8.0
ATT_GROUPS = ((128, 1), (512, 4), (2048, 16))
N_GROUPS = len(ATT_GROUPS)
ATT_HEADS = 8
ATT_HEAD_DIM = 64
ATT_W = ATT_HEADS * ATT_HEAD_DIM
ATT_BLOCK = 128
OFF_XR = 0
OFF_YR = OFF_XR + D_RNN
OFF_Q = OFF_YR + D_RNN
OFF_K = OFF_Q + N_GROUPS * ATT_W
OFF_V = OFF_K + N_GROUPS * ATT_W
OFF_GA = OFF_V + N_GROUPS * ATT_W
OFF_GB = OFF_GA + D_MODEL
N_IN = OFF_GB + D_MODEL
N_EXPERTS = 32
TOP_K = 4
D_FF = D_MODEL
SWIGLU_ALPHA = 1.702
SWIGLU_LIMIT = 7.0
PLE_DIM = 256
ALPHA = (2.0 * DEPTH) ** 0.25
LN_EPS = 1e-5

LANES = 128
SUBLANES = 8
ROW_TILES = D_MODEL // LANES
COLB = ATT_W
N_COLB = N_IN // COLB
MOE_ROWS = 256
VMEM_LIMIT = 56 * 1024 * 1024
NEG_BIG = -1e30

assert ROW_TILES == SUBLANES


def _cparams(sem):
    return pltpu.CompilerParams(dimension_semantics=sem, vmem_limit_bytes=VMEM_LIMIT)


def _layer_spec(li, shape):
    return pl.BlockSpec((None,) + tuple(shape), lambda *_: (li,) + (0,) * len(shape))


def _layer_norm(v, g, b):
    mu = jnp.mean(v, axis=-1, keepdims=True)
    c = v - mu
    var = jnp.mean(c * c, axis=-1, keepdims=True)
    return c * lax.rsqrt(var + LN_EPS) * g + b


def _sigmoid(v):
    return 1.0 / (1.0 + jnp.exp(-v))


def _to_row_tiles(ref, m, rows):
    for c in range(ROW_TILES):
        ref[pl.ds(c, rows, stride=ROW_TILES), :] = m[:, c * LANES:(c + 1) * LANES]


def _from_row_tiles(ref, rows):
    return jnp.concatenate([ref[pl.ds(c, rows, stride=ROW_TILES), :] for c in range(ROW_TILES)],
                           axis=-1)


PROJ_ROWS = 512
DENSE_W = 2 * D_RNN + 2 * D_MODEL


def _proj_dense_body(x_ref, wlo_ref, w0_ref, w1_ref, w2_ref, w3_ref, o_ref):
    x = x_ref[...].astype(BF16)
    lo = 2 * D_RNN
    o_ref[:, 0:lo] = jnp.dot(x, wlo_ref[...], preferred_element_type=F32).astype(o_ref.dtype)
    for c, w_ref in enumerate((w0_ref, w1_ref, w2_ref, w3_ref)):
        o_ref[:, lo + c * COLB:lo + (c + 1) * COLB] = jnp.dot(
            x, w_ref[...], preferred_element_type=F32).astype(o_ref.dtype)


def _proj_dense(x2d, w_in, li):
    t, d = x2d.shape
    tm = min(PROJ_ROWS, t)
    g0 = OFF_GA // COLB
    wcol = lambda c: pl.BlockSpec((None, d, COLB), lambda i: (li, 0, c))
    return pl.pallas_call(
        _proj_dense_body,
        grid=(t // tm,),
        in_specs=[pl.BlockSpec((tm, d), lambda i: (i, 0)),
                  pl.BlockSpec((None, d, 2 * D_RNN), lambda i: (li, 0, 0)),
                  wcol(g0), wcol(g0 + 1), wcol(g0 + 2), wcol(g0 + 3)],
        out_specs=pl.BlockSpec((tm, DENSE_W), lambda i: (i, 0)),
        out_shape=jax.ShapeDtypeStruct((t, DENSE_W), BF16),
        compiler_params=_cparams(("parallel",)),
        name="proj_dense",
    )(x2d, w_in, w_in, w_in, w_in, w_in)


def _proj_qkv_body(dil, x_ref, wq_ref, wk_ref, wv_ref, o_ref, res_s):
    x = x_ref[...].astype(BF16)
    rows = x.shape[0] // dil
    for c, w_ref in enumerate((wq_ref, wk_ref, wv_ref)):
        res = jnp.dot(x, w_ref[...], preferred_element_type=F32)
        cols = slice(c * ATT_W, (c + 1) * ATT_W)
        if dil == 1:
            o_ref[0, 0, :, cols] = res.astype(o_ref.dtype)
        else:
            for lc in range(ATT_W // LANES):
                res_s[lc] = res[:, lc * LANES:(lc + 1) * LANES]
            for r in range(dil):
                part = [res_s[lc, pl.ds(r, rows, stride=dil), :] for lc in range(ATT_W // LANES)]
                o_ref[0, r, :, cols] = jnp.concatenate(part, axis=-1).astype(o_ref.dtype)


def _proj_qkv(x2d, w_in, li, batch, seq, gi):
    d = x2d.shape[1]
    dil = ATT_GROUPS[gi][1]
    tm = min(PROJ_ROWS, seq)
    ns = seq // tm
    wcol = lambda off: pl.BlockSpec((None, d, ATT_W), lambda b, s: (li, 0, off // ATT_W + gi))
    return pl.pallas_call(
        functools.partial(_proj_qkv_body, dil),
        grid=(batch, ns),
        in_specs=[pl.BlockSpec((tm, d), lambda b, s: (b * ns + s, 0)),
                  wcol(OFF_Q), wcol(OFF_K), wcol(OFF_V)],
        out_specs=pl.BlockSpec((1, dil, tm // dil, 3 * ATT_W), lambda b, s: (b, 0, s, 0)),
        out_shape=jax.ShapeDtypeStruct((batch, dil, seq // dil, 3 * ATT_W), BF16),
        scratch_shapes=[pltpu.VMEM((ATT_W // LANES, tm, LANES), F32)],
        compiler_params=_cparams(("parallel", "parallel")),
        name=f"proj_qkv_g{gi}",
    )(x2d, w_in, w_in, w_in)


def _lru_body(xr_ref, yr_ref, cw_ref, cb_ref, wrg_ref, brg_ref, wig_ref, big_ref, lam_ref,
              wout_ref, o_ref, tail_ref, h_ref):
    s = pl.program_id(1)

    @pl.when(s == 0)
    def _():
        tail_ref[...] = jnp.zeros_like(tail_ref)
        h_ref[...] = jnp.zeros_like(h_ref)

    xr = xr_ref[...].astype(F32)
    ts = xr.shape[0]
    xe = jnp.concatenate([tail_ref[...], xr], axis=0)
    cw = cw_ref[...]
    xc = (cb_ref[...] + cw[3:4] * xr + cw[2:3] * xe[7:7 + ts]
          + cw[1:2] * xe[6:6 + ts] + cw[0:1] * xe[5:5 + ts])
    tail_ref[...] = xr[ts - 8:]

    xcb = xc.astype(BF16)

    def gate(w_ref, b_ref):
        parts = [jnp.dot(xcb[:, n * RNN_BW:(n + 1) * RNN_BW], w_ref[n],
                         preferred_element_type=F32) for n in range(RNN_BLOCKS)]
        return _sigmoid(jnp.concatenate(parts, axis=-1) + b_ref[...])

    r = gate(wrg_ref, brg_ref)
    i = gate(wig_ref, big_ref)
    nlam = -lam_ref[...]
    softplus = jnp.maximum(nlam, 0.0) + jnp.log1p(jnp.exp(-jnp.abs(nlam)))
    log_a = (-LRU_C) * r * softplus
    a = jnp.exp(log_a)
    bx = jnp.sqrt(jnp.tanh(-log_a) * (1.0 + a * a)) * (i * xc)

    groups = ts // SUBLANES
    a3 = a.reshape(groups, SUBLANES, a.shape[-1])
    b3 = bx.reshape(groups, SUBLANES, a.shape[-1])
    sub = lax.broadcasted_iota(I32, (1, SUBLANES, 1), 1)
    k = 1
    while k < SUBLANES:
        keep = sub >= k
        a_prev = pltpu.roll(a3, k, 1)
        b_prev = pltpu.roll(b3, k, 1)
        b3 = jnp.where(keep, a3 * b_prev, 0.0) + b3
        a3 = jnp.where(keep, a3 * a_prev, a3)
        k *= 2
    carry = h_ref[0:1]
    hs = []
    for gidx in range(groups):
        hg = a3[gidx] * carry + b3[gidx]
        hs.append(hg)
        carry = hg[SUBLANES - 1:SUBLANES]
    h = jnp.concatenate(hs, axis=0)
    h_ref[0:1] = carry

    yr = yr_ref[...].astype(F32)
    gelu = 0.5 * yr * (1.0 + jnp.tanh(0.7978845608028654 * (yr + 0.044715 * (yr * yr * yr))))
    o_ref[...] = jnp.dot((gelu * h).astype(BF16), wout_ref[...],
                         preferred_element_type=F32).astype(o_ref.dtype)


def _lru_branch(z, batch, seq, li, conv_w, conv_b, w_rg, b_rg, w_ig, b_ig, lam, w_rnn_out):
    ts = min(256, seq)
    ns = seq // ts
    c = D_RNN
    rows = lambda v: v.reshape(v.shape[0], 1, c)
    return pl.pallas_call(
        _lru_body,
        grid=(batch, ns),
        in_specs=[pl.BlockSpec((ts, c), lambda b, s: (b * ns + s, OFF_XR // c)),
                  pl.BlockSpec((ts, c), lambda b, s: (b * ns + s, OFF_YR // c)),
                  _layer_spec(li, (CONV_W, c)), _layer_spec(li, (1, c)),
                  _layer_spec(li, (RNN_BLOCKS, RNN_BW, RNN_BW)), _layer_spec(li, (1, c)),
                  _layer_spec(li, (RNN_BLOCKS, RNN_BW, RNN_BW)), _layer_spec(li, (1, c)),
                  _layer_spec(li, (1, c)), _layer_spec(li, (c, D_MODEL))],
        out_specs=pl.BlockSpec((ts, D_MODEL), lambda b, s: (b * ns + s, 0)),
        out_shape=jax.ShapeDtypeStruct((batch * seq, D_MODEL), BF16),
        scratch_shapes=[pltpu.VMEM((8, c), F32), pltpu.VMEM((8, c), F32)],
        compiler_params=_cparams(("arbitrary", "arbitrary")),
        name="lru_branch",
    )(z, z, conv_w, rows(conv_b), w_rg, rows(b_rg), w_ig, rows(b_ig), rows(lam), w_rnn_out)


ATT_QROWS = 512


def _attn_body(n_back, q_ref, kp_ref, kc_ref, vp_ref, vc_ref, o_ref, lse_ref):
    n = pl.program_id(2)
    blk = ATT_BLOCK
    nsub = q_ref.shape[2] // blk
    qi = lax.broadcasted_iota(I32, (blk, 2 * blk), 0)
    kj = lax.broadcasted_iota(I32, (blk, 2 * blk), 1)
    diff = blk + qi - kj
    in_window = (diff >= 0) & (diff <= n_back)
    lane = lax.broadcasted_iota(I32, (blk, LANES), 1)
    scale = jnp.asarray(ATT_HEAD_DIM ** -0.5, q_ref.dtype)
    for sb in range(nsub):
        rows = slice(sb * blk, (sb + 1) * blk)
        q = q_ref[0, 0, rows, :] * scale
        if sb == 0:
            k = jnp.concatenate([kp_ref[0, 0], kc_ref[0, 0, rows, :]], axis=0)
            v = jnp.concatenate([vp_ref[0, 0], vc_ref[0, 0, rows, :]], axis=0)
            valid = in_window & ((kj >= blk) | (n > 0))
        else:
            k = kc_ref[0, 0, (sb - 1) * blk:(sb + 1) * blk, :]
            v = vc_ref[0, 0, (sb - 1) * blk:(sb + 1) * blk, :]
            valid = in_window
        lse_tile = jnp.zeros((blk, LANES), F32)
        outs = []
        for h in range(ATT_HEADS):
            sl = slice(h * ATT_HEAD_DIM, (h + 1) * ATT_HEAD_DIM)
            s = lax.dot_general(q[:, sl], k[:, sl], (((1,), (1,)), ((), ())),
                                preferred_element_type=F32)
            s = jnp.where(valid, s, NEG_BIG)
            m = jnp.max(s, axis=-1, keepdims=True)
            p = jnp.exp(s - m)
            l = jnp.sum(p, axis=-1, keepdims=True)
            o = jnp.dot(p.astype(BF16), v[:, sl], preferred_element_type=F32)
            outs.append(o / l)
            lse_tile = jnp.where(lane == h, m + jnp.log(l), lse_tile)
        o_ref[0, 0, rows, :] = jnp.concatenate(outs, axis=-1).astype(o_ref.dtype)
        lse_ref[0, 0, rows, :] = lse_tile


def _attention_group(qkv, gi):
    batch, dil, l, _ = qkv.shape
    n_back = ATT_GROUPS[gi][0] // dil
    qrows = min(ATT_QROWS, l)
    nsub = qrows // ATT_BLOCK
    cur = lambda c: pl.BlockSpec((1, 1, qrows, ATT_W), lambda b, r, n: (b, r, n, c))
    prev = lambda c: pl.BlockSpec((1, 1, ATT_BLOCK, ATT_W),
                                  lambda b, r, n: (b, r, jnp.maximum(n * nsub - 1, 0), c))
    return pl.pallas_call(
        functools.partial(_attn_body, n_back),
        grid=(batch, dil, l // qrows),
        in_specs=[cur(0), prev(1), cur(1), prev(2), cur(2)],
        out_specs=[pl.BlockSpec((1, 1, qrows, ATT_W), lambda b, r, n: (b, r, n, 0)),
                   pl.BlockSpec((1, 1, qrows, LANES), lambda b, r, n: (b, r, n, 0))],
        out_shape=[jax.ShapeDtypeStruct((batch, dil, l, ATT_W), BF16),
                   jax.ShapeDtypeStruct((batch, dil, l, LANES), F32)],
        compiler_params=_cparams(("parallel", "parallel", "arbitrary")),
        name=f"attn_g{gi}",
    )(qkv, qkv, qkv, qkv, qkv)


def _merge_body(o0_ref, o1_ref, o2_ref, l0_ref, l1_ref, l2_ref, ya_ref, ga_ref, gb_ref,
                x_ref, watt_ref, wout_ref, g_ref, b_ref, wr_ref, br_ref,
                x1_ref, x1t_ref, route_ref, cnt_ref, carry_ref, o_s, l_s):
    i = pl.program_id(0)

    @pl.when(i == 0)
    def _():
        carry_ref[...] = jnp.zeros_like(carry_ref)

    ts = x_ref.shape[0]

    def token_order(ref, scratch):
        dil = ref.shape[1]
        if dil == 1:
            return ref[0, 0].astype(F32)
        nl = ref.shape[3] // LANES
        for r in range(dil):
            v = ref[0, r].astype(F32)
            for lc in range(nl):
                scratch[lc, pl.ds(r, ts // dil, stride=dil), :] = v[:, lc * LANES:(lc + 1) * LANES]
        return jnp.concatenate([scratch[lc] for lc in range(nl)], axis=-1)

    group_o = [token_order(ref, o_s.at[gi]) for gi, ref in enumerate((o0_ref, o1_ref, o2_ref))]
    lses = [token_order(ref, l_s.at[gi]) for gi, ref in enumerate((l0_ref, l1_ref, l2_ref))]
    mx = jnp.maximum(jnp.maximum(lses[0], lses[1]), lses[2])
    es = [jnp.exp(v - mx) for v in lses]
    den = es[0] + es[1] + es[2]
    er = lax.broadcasted_iota(I32, (LANES, ATT_W), 0)
    ec = lax.broadcasted_iota(I32, (LANES, ATT_W), 1)
    expand = jnp.where(ec // ATT_HEAD_DIM == er, 1.0, 0.0).astype(BF16)
    o = jnp.zeros((ts, ATT_W), F32)
    for e, og in zip(es, group_o):
        w = e / den
        w_hi = w.astype(BF16)
        w_lo = (w - w_hi.astype(F32)).astype(BF16)
        wx = (jnp.dot(w_hi, expand, preferred_element_type=F32)
              + jnp.dot(w_lo, expand, preferred_element_type=F32))
        o = o + wx * og
    y_b = jnp.dot(o.astype(BF16), watt_ref[...], preferred_element_type=F32)
    ga = ga_ref[...].astype(F32)
    gb = gb_ref[...].astype(F32)
    merged = _sigmoid(ga) * ya_ref[...].astype(F32) + _sigmoid(gb) * y_b
    hmix = jnp.dot(merged.astype(BF16), wout_ref[...], preferred_element_type=F32)
    x1 = _layer_norm(ALPHA * x_ref[...] + hmix, g_ref[...], b_ref[...])
    x1_ref[...] = x1
    _to_row_tiles(x1t_ref, x1, ts)

    wr = wr_ref[...]
    wr_hi = wr.astype(BF16)
    wr_lo = (wr - wr_hi.astype(F32)).astype(BF16)
    x_hi = x1.astype(BF16)
    x_lo = (x1 - x_hi.astype(F32)).astype(BF16)
    logits = (jnp.dot(x_hi, wr_hi, preferred_element_type=F32)
              + jnp.dot(x_lo, wr_hi, preferred_element_type=F32)
              + jnp.dot(x_hi, wr_lo, preferred_element_type=F32)) + br_ref[...]

    el = lax.broadcasted_iota(I32, (ts, N_EXPERTS), 1)
    work = logits
    vals, idxs, hots = [], [], []
    for _ in range(TOP_K):
        m = jnp.max(work, axis=-1, keepdims=True)
        idx = jnp.min(jnp.where(work == m, el, N_EXPERTS), axis=-1, keepdims=True)
        hot = el == idx
        vals.append(m)
        idxs.append(idx)
        hots.append(hot)
        work = jnp.where(hot, NEG_BIG, work)
    exps = [jnp.exp(v - vals[0]) for v in vals]
    gden = exps[0] + exps[1] + exps[2] + exps[3]

    cnt = jnp.zeros((ts, N_EXPERTS), F32)
    for hot in hots:
        cnt = cnt + jnp.where(hot, 1.0, 0.0)
    tr = lax.broadcasted_iota(I32, (ts, ts), 0)
    tc = lax.broadcasted_iota(I32, (ts, ts), 1)
    tri = jnp.where(tc < tr, 1.0, 0.0).astype(BF16)
    before = jnp.dot(tri, cnt.astype(BF16), preferred_element_type=F32) + carry_ref[...]
    carry_ref[...] = carry_ref[...] + jnp.sum(cnt, axis=0, keepdims=True)
    cnt_ref[...] = carry_ref[...]

    lane = lax.broadcasted_iota(I32, (ts, LANES), 1)
    route = jnp.zeros((ts, LANES), I32)
    for kk in range(TOP_K):
        rank = jnp.sum(jnp.where(hots[kk], before, 0.0), axis=-1, keepdims=True).astype(I32)
        gate_bits = pltpu.bitcast(exps[kk] / gden, I32)
        route = jnp.where(lane == kk, idxs[kk], route)
        route = jnp.where(lane == TOP_K + kk, rank, route)
        route = jnp.where(lane == 2 * TOP_K + kk, gate_bits, route)
    route_ref[...] = route


def _merge(outs, lses, y_a, zd, x2d, seq, li, w_att_out, w_out, ln_g, ln_b, w_router, b_router):
    t = x2d.shape[0]
    ts = min(256, seq)
    ns = seq // ts
    rows = lambda v: v.reshape(v.shape[0], 1, v.shape[-1])
    tile = lambda w, c=0: pl.BlockSpec((ts, w), lambda i: (i, c))

    def by_residue(a):
        dil, w = a.shape[1], a.shape[3]
        return pl.BlockSpec((1, dil, ts // dil, w), lambda i: (i // ns, 0, i % ns, 0))

    return pl.pallas_call(
        _merge_body,
        grid=(t // ts,),
        in_specs=[by_residue(a) for a in (*outs, *lses)]
                 + [tile(D_MODEL), tile(D_MODEL, 2 * D_RNN // D_MODEL),
                    tile(D_MODEL, 2 * D_RNN // D_MODEL + 1), tile(D_MODEL),
                    _layer_spec(li, (ATT_W, D_MODEL)), _layer_spec(li, (D_MODEL, D_MODEL)),
                    _layer_spec(li, (1, D_MODEL)), _layer_spec(li, (1, D_MODEL)),
                    _layer_spec(li, (D_MODEL, N_EXPERTS)), _layer_spec(li, (1, N_EXPERTS))],
        out_specs=[tile(D_MODEL), pl.BlockSpec((ts * ROW_TILES, LANES), lambda i: (i, 0)),
                   tile(LANES), pl.BlockSpec((1, N_EXPERTS), lambda i: (0, 0))],
        out_shape=[jax.ShapeDtypeStruct((t, D_MODEL), F32),
                   jax.ShapeDtypeStruct((t * ROW_TILES, LANES), F32),
                   jax.ShapeDtypeStruct((t, LANES), I32),
                   jax.ShapeDtypeStruct((1, N_EXPERTS), F32)],
        scratch_shapes=[pltpu.VMEM((1, N_EXPERTS), F32),
                        pltpu.VMEM((N_GROUPS, ATT_W // LANES, ts, LANES), F32),
                        pltpu.VMEM((N_GROUPS, 1, ts, LANES), F32)],
        compiler_params=_cparams(("arbitrary",)),
        name="merge_ln1_router",
    )(*outs, *lses, y_a, zd, zd, x2d, w_att_out, w_out, rows(ln_g), rows(ln_b), w_router,
      rows(b_router))


INVERT_UNROLL = 8
GATHER_UNROLL = 8


def _build_row_order(dest_ref, fill_ref, n_ref, inv_ref):
    def zero_range(lo, hi):
        def z(row, carry):
            inv_ref[row] = 0
            return carry
        lax.fori_loop(lo, hi, z, 0)

    def per_expert(e, carry):
        zero_range(fill_ref[2 * e], fill_ref[2 * e + 1])
        return carry
    lax.fori_loop(0, N_EXPERTS, per_expert, 0)
    zero_range(fill_ref[2 * N_EXPERTS] * MOE_ROWS, fill_ref[2 * N_EXPERTS + 1] * MOE_ROWS)

    def chunk(c, carry):
        base = c * INVERT_UNROLL
        tok0 = c * (INVERT_UNROLL // TOP_K)
        for u in range(INVERT_UNROLL):
            inv_ref[dest_ref[base + u]] = tok0 + u // TOP_K
        return carry
    lax.fori_loop(0, n_ref[0] // INVERT_UNROLL, chunk, 0)


def _expert_body(li, be_ref, nvb_ref, first_ref, wslot_ref, nxt_ref, dest_ref, fill_ref, n_ref,
                 x1t_hbm, wg_hbm, wu_hbm, wd_hbm, bg_ref, bu_ref, bd_ref, y_ref,
                 inv_s, wst, wg_s, wu_s, wd_s, rows0, rows1, x_s, sem, wsem):
    j = pl.program_id(0)
    nvb = nvb_ref[0]
    weights = (wg_hbm, wu_hbm, wd_hbm)

    def issue(blk, buf, s):
        def chunk(c, carry):
            for u in range(GATHER_UNROLL):
                r = c * GATHER_UNROLL + u
                tok = inv_s[blk * MOE_ROWS + r]
                src = x1t_hbm.at[pl.ds(pl.multiple_of(tok * ROW_TILES, ROW_TILES), ROW_TILES)]
                dst = buf.at[pl.ds(pl.multiple_of(r * ROW_TILES, ROW_TILES), ROW_TILES)]
                pltpu.make_async_copy(src, dst, sem.at[s]).start()
            return carry
        lax.fori_loop(0, MOE_ROWS // GATHER_UNROLL, chunk, 0)

    def weight_copies(e, slot):
        return [pltpu.make_async_copy(w.at[li, e], wst.at[slot, k], wsem.at[slot])
                for k, w in enumerate(weights)]

    @pl.when(j == 0)
    def _():
        for cp in weight_copies(be_ref[0], 0):
            cp.start()
        _build_row_order(dest_ref, fill_ref, n_ref, inv_s)
        issue(0, rows0, 0)

    @pl.when(first_ref[j] == 1)
    def _():
        slot = wslot_ref[j]
        for cp in weight_copies(0, slot):
            cp.wait()
        wg_s[...] = wst[slot, 0].astype(BF16)
        wu_s[...] = wst[slot, 1].astype(BF16)
        wd_s[...] = wst[slot, 2].astype(BF16)

        @pl.when(nxt_ref[j] >= 0)
        def _():
            for cp in weight_copies(nxt_ref[j], 1 - slot):
                cp.start()

    def fetch(buf, s, other, so):
        @pl.when(j + 1 < nvb)
        def _():
            issue(j + 1, other, so)

        pltpu.make_async_copy(buf, buf, sem.at[s]).wait()
        x_s[...] = _from_row_tiles(buf, MOE_ROWS).astype(BF16)

    @pl.when((j < nvb) & (j % 2 == 0))
    def _():
        fetch(rows0, 0, rows1, 1)

    @pl.when((j < nvb) & (j % 2 == 1))
    def _():
        fetch(rows1, 1, rows0, 0)

    @pl.when(j >= nvb)
    def _():
        y_ref[...] = jnp.zeros_like(y_ref)

    @pl.when(j < nvb)
    def _():
        x = x_s[...]
        g = jnp.dot(x, wg_s[...], preferred_element_type=F32) + bg_ref[...]
        up = jnp.dot(x, wu_s[...], preferred_element_type=F32) + bu_ref[...]
        g = jnp.minimum(g, SWIGLU_LIMIT)
        up = jnp.clip(up, -SWIGLU_LIMIT, SWIGLU_LIMIT)
        hdn = (up + 1.0) * (g * _sigmoid(SWIGLU_ALPHA * g))
        y = jnp.dot(hdn.astype(BF16), wd_s[...], preferred_element_type=F32) + bd_ref[...]
        _to_row_tiles(y_ref, y, MOE_ROWS)


def _experts(x1t, tables, li, w_gate, b_gate, w_up, b_up, w_down, b_down):
    n_rows = tables['n_rows']
    d = D_MODEL
    blk = (MOE_ROWS * ROW_TILES, LANES)
    prefetch = (tables['blk_expert'], tables['n_used'], tables['first'], tables['wslot'],
                tables['next_expert'], tables['dest'], tables['fill'], tables['n_assign'])
    bspec = pl.BlockSpec((None, None, 1, d), lambda j, be, *_: (li, be[j], 0, 0))
    bias = lambda v: v.reshape(v.shape[0], v.shape[1], 1, v.shape[2])
    hbm = pl.BlockSpec(memory_space=pl.ANY)
    rows = pltpu.VMEM(blk, F32)
    return pl.pallas_call(
        functools.partial(_expert_body, li),
        grid_spec=pltpu.PrefetchScalarGridSpec(
            num_scalar_prefetch=len(prefetch), grid=(n_rows // MOE_ROWS,),
            in_specs=[hbm, hbm, hbm, hbm, bspec, bspec, bspec],
            out_specs=pl.BlockSpec(blk, lambda j, *_: (j, 0)),
            scratch_shapes=[pltpu.SMEM((n_rows,), I32), pltpu.VMEM((2, 3, d, D_FF), F32),
                            pltpu.VMEM((d, D_FF), BF16), pltpu.VMEM((d, D_FF), BF16),
                            pltpu.VMEM((D_FF, d), BF16), rows, rows,
                            pltpu.VMEM((MOE_ROWS, d), BF16), pltpu.SemaphoreType.DMA((2,)),
                            pltpu.SemaphoreType.DMA((2,))]),
        out_shape=jax.ShapeDtypeStruct((n_rows * ROW_TILES, LANES), F32),
        compiler_params=_cparams(("arbitrary",)),
        name="moe_experts",
    )(*prefetch, x1t, w_gate, w_up, w_down, bias(b_gate), bias(b_up), bias(b_down))


def _combine_body(dest_ref, yb_hbm, route_ref, x1_ref, p_ref, wple_ref, wpg_ref, bpg_ref,
                  g2_ref, b2_ref, g3_ref, b3_ref, o_ref, buf0, buf1, y_s, sem):
    i = pl.program_id(0)
    n = pl.num_programs(0)
    ts = x1_ref.shape[0]

    def issue(tile, buf, s):
        def tok(j, carry):
            a = (tile * ts + j) * TOP_K
            for kk in range(TOP_K):
                row = dest_ref[a + kk]
                src = yb_hbm.at[pl.ds(pl.multiple_of(row * ROW_TILES, ROW_TILES), ROW_TILES)]
                dst = buf.at[kk, pl.ds(pl.multiple_of(j * ROW_TILES, ROW_TILES), ROW_TILES)]
                pltpu.make_async_copy(src, dst, sem.at[s]).start()
            return carry
        lax.fori_loop(0, ts, tok, 0)

    @pl.when(i == 0)
    def _():
        issue(0, buf0, 0)

    route = route_ref[...]
    gates = [pltpu.bitcast(route[:, 2 * TOP_K + kk:2 * TOP_K + kk + 1], F32)
             for kk in range(TOP_K)]

    def phase(buf, s, other, so):
        @pl.when(i + 1 < n)
        def _():
            issue(i + 1, other, so)

        pltpu.make_async_copy(buf, buf, sem.at[s]).wait()
        for c in range(ROW_TILES):
            acc = jnp.zeros((ts, LANES), F32)
            for kk in range(TOP_K):
                acc = acc + gates[kk] * buf[kk, pl.ds(c, ts, stride=ROW_TILES), :]
            y_s[:, c * LANES:(c + 1) * LANES] = acc

    @pl.when(i % 2 == 0)
    def _():
        phase(buf0, 0, buf1, 1)

    @pl.when(i % 2 == 1)
    def _():
        phase(buf1, 1, buf0, 0)

    x2 = _layer_norm(ALPHA * x1_ref[...] + y_s[...], g2_ref[...], b2_ref[...])
    emb = jnp.dot(p_ref[...].astype(BF16), wple_ref[...], preferred_element_type=F32)
    gate_in = jnp.dot(x2.astype(BF16), wpg_ref[...], preferred_element_type=F32) + bpg_ref[...]
    ple = emb * _sigmoid(gate_in)
    o_ref[...] = _layer_norm(ALPHA * x2 + ple, g3_ref[...], b3_ref[...])


def _combine(dest_flat, yb, route, x1, p3d, li, w_ple, w_ple_gate, b_ple_gate, ln2_g, ln2_b,
             ln3_g, ln3_b):
    t, d = x1.shape
    ts = min(256, t)
    rows = lambda v: v.reshape(v.shape[0], 1, v.shape[-1])
    tile = lambda w: pl.BlockSpec((ts, w), lambda i, dst: (i, 0))
    buf = pltpu.VMEM((TOP_K, ts * ROW_TILES, LANES), F32)
    return pl.pallas_call(
        _combine_body,
        grid_spec=pltpu.PrefetchScalarGridSpec(
            num_scalar_prefetch=1, grid=(t // ts,),
            in_specs=[pl.BlockSpec(memory_space=pl.ANY), tile(LANES), tile(d),
                      pl.BlockSpec((None, ts, PLE_DIM), lambda i, dst: (li, i, 0)),
                      _layer_spec(li, (PLE_DIM, d)), _layer_spec(li, (d, d)),
                      _layer_spec(li, (1, d)), _layer_spec(li, (1, d)), _layer_spec(li, (1, d)),
                      _layer_spec(li, (1, d)), _layer_spec(li, (1, d))],
            out_specs=tile(d),
            scratch_shapes=[buf, buf, pltpu.VMEM((ts, d), F32), pltpu.SemaphoreType.DMA((2,))]),
        out_shape=jax.ShapeDtypeStruct((t, d), F32),
        compiler_params=_cparams(("arbitrary",)),
        name="moe_combine_ple",
    )(dest_flat, yb, route, x1, p3d, w_ple, w_ple_gate, rows(b_ple_gate), rows(ln2_g), rows(ln2_b),
      rows(ln3_g), rows(ln3_b))


def _routing_tables(route, counts, t):
    counts = counts.reshape(N_EXPERTS).astype(I32)
    padded = (counts + MOE_ROWS - 1) // MOE_ROWS * MOE_ROWS
    pend = jnp.cumsum(padded)
    pstart = pend - padded
    top_e = route[:, 0:TOP_K]
    rank = route[:, TOP_K:2 * TOP_K]
    onehot = top_e[:, :, None] == jnp.arange(N_EXPERTS, dtype=I32)
    dest = (jnp.sum(jnp.where(onehot, pstart, 0), axis=-1) + rank).reshape(t * TOP_K)
    n_blk = t * TOP_K // MOE_ROWS + N_EXPERTS
    n_used = pend[N_EXPERTS - 1] // MOE_ROWS
    fill = jnp.concatenate([jnp.stack([pstart + counts, pend], axis=1).reshape(2 * N_EXPERTS),
                            jnp.stack([n_used, jnp.asarray(n_blk, I32)])]).astype(I32)
    blk = jnp.arange(n_blk, dtype=I32)
    blk_expert = jnp.minimum(jnp.sum((pend[None, :] <= (blk * MOE_ROWS)[:, None]).astype(I32),
                                     axis=1), N_EXPERTS - 1)
    first = (blk < n_used) & ((blk == 0) | (blk_expert != jnp.roll(blk_expert, 1)))
    wslot = (jnp.cumsum(first.astype(I32)) - 1) % 2
    first_at = jnp.where(first, blk, n_blk)
    next_first = jnp.concatenate([lax.cummin(first_at[::-1])[::-1][1:],
                                  jnp.full((1,), n_blk, I32)])
    next_expert = jnp.where(next_first < n_blk,
                            blk_expert[jnp.minimum(next_first, n_blk - 1)], -1)
    return dict(dest=dest, fill=fill, blk_expert=blk_expert, n_used=n_used.reshape(1),
                first=first.astype(I32), wslot=wslot.astype(I32),
                next_expert=next_expert.astype(I32), n_assign=jnp.full((1,), t * TOP_K, I32),
                n_rows=n_blk * MOE_ROWS)


def _layer(x2d, p3d, batch, seq, li, w):
    t = batch * seq
    zd = _proj_dense(x2d, w['w_in'], li)
    y_a = _lru_branch(zd, batch, seq, li, w['conv_w'], w['conv_b'], w['w_rg'], w['b_rg'],
                      w['w_ig'], w['b_ig'], w['lru_lambda'], w['w_rnn_out'])
    outs, lses = [], []
    for gi in range(N_GROUPS):
        o, lse = _attention_group(_proj_qkv(x2d, w['w_in'], li, batch, seq, gi), gi)
        outs.append(o)
        lses.append(lse)
    x1, x1t, route, counts = _merge(outs, lses, y_a, zd, x2d, seq, li, w['w_att_out'], w['w_out'],
                                    w['ln1_g'], w['ln1_b'], w['w_router'], w['b_router'])
    tables = _routing_tables(route, counts, t)
    yb = _experts(x1t, tables, li, w['w_gate'], w['b_gate'], w['w_up'], w['b_up'], w['w_down'],
                  w['b_down'])
    return _combine(tables['dest'], yb, route, x1, p3d, li, w['w_ple'], w['w_ple_gate'],
                    w['b_ple_gate'],
                    w['ln2_g'], w['ln2_b'], w['ln3_g'], w['ln3_b'])


_WEIGHT_NAMES = ('w_in', 'conv_w', 'conv_b', 'w_rg', 'b_rg', 'w_ig', 'b_ig', 'lru_lambda',
                 'w_rnn_out', 'w_att_out', 'w_out', 'ln1_g', 'ln1_b', 'w_router', 'b_router',
                 'w_gate', 'b_gate', 'w_up', 'b_up', 'w_down', 'b_down', 'ln2_g', 'ln2_b',
                 'w_ple', 'w_ple_gate', 'b_ple_gate', 'ln3_g', 'ln3_b')
_BF16_WEIGHTS = ('w_in', 'w_rg', 'w_ig', 'w_rnn_out', 'w_att_out', 'w_out', 'w_ple', 'w_ple_gate')


def kernel(x, p, w_in, conv_w, conv_b, w_rg, b_rg, w_ig, b_ig, lru_lambda, w_rnn_out, w_att_out,
           w_out, ln1_g, ln1_b, w_router, b_router, w_gate, b_gate, w_up, b_up, w_down, b_down,
           ln2_g, ln2_b, w_ple, w_ple_gate, b_ple_gate, ln3_g, ln3_b):
    w = dict(zip(_WEIGHT_NAMES, (
        w_in, conv_w, conv_b, w_rg, b_rg, w_ig, b_ig, lru_lambda, w_rnn_out, w_att_out, w_out,
        ln1_g, ln1_b, w_router, b_router, w_gate, b_gate, w_up, b_up, w_down, b_down, ln2_g,
        ln2_b, w_ple, w_ple_gate, b_ple_gate, ln3_g, ln3_b)))
    for name in _BF16_WEIGHTS:
        w[name] = w[name].astype(BF16)
    batch, seq, d = x.shape
    depth = p.shape[0]
    x2d = x.reshape(batch * seq, d)
    p3d = p.reshape(depth, batch * seq, PLE_DIM)
    for li in range(depth):
        x2d = _layer(x2d, p3d, batch, seq, li, w)
    return x2d.reshape(batch, seq, d)
```

```python
import functools

import jax
import jax.numpy as jnp
from jax import lax
from jax.experimental import pallas as pl
from jax.experimental.pallas import tpu as pltpu

F32 = jnp.float32
BF16 = jnp.bfloat16
I32 = jnp.int32

D_MODEL = 1024
DEPTH = 2
D_RNN = 1024
RNN_BLOCKS = 4
RNN_BW = D_RNN // RNN_BLOCKS
CONV_W = 4
LRU_C = 8.0
ATT_GROUPS = ((128, 1), (512, 4), (2048, 16))
N_GROUPS = len(ATT_GROUPS)
ATT_HEADS = 8
ATT_HEAD_DIM = 64
ATT_W = ATT_HEADS * ATT_HEAD_DIM
ATT_BLOCK = 128
OFF_XR = 0
OFF_YR = OFF_XR + D_RNN
OFF_Q = OFF_YR + D_RNN
OFF_K = OFF_Q + N_GROUPS * ATT_W
OFF_V = OFF_K + N_GROUPS * ATT_W
OFF_GA = OFF_V + N_GROUPS * ATT_W
OFF_GB = OFF_GA + D_MODEL
N_IN = OFF_GB + D_MODEL
N_EXPERTS = 32
TOP_K = 4
D_FF = D_MODEL
SWIGLU_ALPHA = 1.702
SWIGLU_LIMIT = 7.0
PLE_DIM = 256
ALPHA = (2.0 * DEPTH) ** 0.25
LN_EPS = 1e-5

LANES = 128
SUBLANES = 8
ROW_TILES = D_MODEL // LANES
COLB = ATT_W
N_COLB = N_IN // COLB
MOE_ROWS = 256
VMEM_LIMIT = 56 * 1024 * 1024
NEG_BIG = -1e30

assert ROW_TILES == SUBLANES


def _cparams(sem):
    return pltpu.CompilerParams(dimension_semantics=sem, vmem_limit_bytes=VMEM_LIMIT)


def _layer_spec(li, shape):
    return pl.BlockSpec((None,) + tuple(shape), lambda *_: (li,) + (0,) * len(shape))


def _layer_norm(v, g, b):
    mu = jnp.mean(v, axis=-1, keepdims=True)
    c = v - mu
    var = jnp.mean(c * c, axis=-1, keepdims=True)
    return c * lax.rsqrt(var + LN_EPS) * g + b


def _sigmoid(v):
    return 1.0 / (1.0 + jnp.exp(-v))


def _to_row_tiles(ref, m, rows):
    for c in range(ROW_TILES):
        ref[pl.ds(c, rows, stride=ROW_TILES), :] = m[:, c * LANES:(c + 1) * LANES]


def _from_row_tiles(ref, rows):
    return jnp.concatenate([ref[pl.ds(c, rows, stride=ROW_TILES), :] for c in range(ROW_TILES)],
                           axis=-1)


PROJ_ROWS = 512
DENSE_W = 2 * D_RNN + 2 * D_MODEL


def _proj_dense_body(x_ref, wlo_ref, w0_ref, w1_ref, w2_ref, w3_ref, o_ref):
    x = x_ref[...].astype(BF16)
    lo = 2 * D_RNN
    o_ref[:, 0:lo] = jnp.dot(x, wlo_ref[...], preferred_element_type=F32).astype(o_ref.dtype)
    for c, w_ref in enumerate((w0_ref, w1_ref, w2_ref, w3_ref)):
        o_ref[:, lo + c * COLB:lo + (c + 1) * COLB] = jnp.dot(
            x, w_ref[...], preferred_element_type=F32).astype(o_ref.dtype)


def _proj_dense(x2d, w_in, li):
    t, d = x2d.shape
    tm = min(PROJ_ROWS, t)
    g0 = OFF_GA // COLB
    wcol = lambda c: pl.BlockSpec((None, d, COLB), lambda i: (li, 0, c))
    return pl.pallas_call(
        _proj_dense_body,
        grid=(t // tm,),
        in_specs=[pl.BlockSpec((tm, d), lambda i: (i, 0)),
                  pl.BlockSpec((None, d, 2 * D_RNN), lambda i: (li, 0, 0)),
                  wcol(g0), wcol(g0 + 1), wcol(g0 + 2), wcol(g0 + 3)],
        out_specs=pl.BlockSpec((tm, DENSE_W), lambda i: (i, 0)),
        out_shape=jax.ShapeDtypeStruct((t, DENSE_W), BF16),
        compiler_params=_cparams(("parallel",)),
        name="proj_dense",
    )(x2d, w_in, w_in, w_in, w_in, w_in)


def _proj_qkv_body(dil, x_ref, wq_ref, wk_ref, wv_ref, o_ref, res_s):
    x = x_ref[...].astype(BF16)
    rows = x.shape[0] // dil
    for c, w_ref in enumerate((wq_ref, wk_ref, wv_ref)):
        res = jnp.dot(x, w_ref[...], preferred_element_type=F32)
        cols = slice(c * ATT_W, (c + 1) * ATT_W)
        if dil == 1:
            o_ref[0, 0, :, cols] = res.astype(o_ref.dtype)
        else:
            for lc in range(ATT_W // LANES):
                res_s[lc] = res[:, lc * LANES:(lc + 1) * LANES]
            for r in range(dil):
                part = [res_s[lc, pl.ds(r, rows, stride=dil), :] for lc in range(ATT_W // LANES)]
                o_ref[0, r, :, cols] = jnp.concatenate(part, axis=-1).astype(o_ref.dtype)


def _proj_qkv(x2d, w_in, li, batch, seq, gi):
    d = x2d.shape[1]
    dil = ATT_GROUPS[gi][1]
    tm = min(PROJ_ROWS, seq)
    ns = seq // tm
    wcol = lambda off: pl.BlockSpec((None, d, ATT_W), lambda b, s: (li, 0, off // ATT_W + gi))
    return pl.pallas_call(
        functools.partial(_proj_qkv_body, dil),
        grid=(batch, ns),
        in_specs=[pl.BlockSpec((tm, d), lambda b, s: (b * ns + s, 0)),
                  wcol(OFF_Q), wcol(OFF_K), wcol(OFF_V)],
        out_specs=pl.BlockSpec((1, dil, tm // dil, 3 * ATT_W), lambda b, s: (b, 0, s, 0)),
        out_shape=jax.ShapeDtypeStruct((batch, dil, seq // dil, 3 * ATT_W), BF16),
        scratch_shapes=[pltpu.VMEM((ATT_W // LANES, tm, LANES), F32)],
        compiler_params=_cparams(("parallel", "parallel")),
        name=f"proj_qkv_g{gi}",
    )(x2d, w_in, w_in, w_in)


def _lru_body(xr_ref, yr_ref, cw_ref, cb_ref, wrg_ref, brg_ref, wig_ref, big_ref, lam_ref,
              wout_ref, o_ref, tail_ref, h_ref):
    s = pl.program_id(1)

    @pl.when(s == 0)
    def _():
        tail_ref[...] = jnp.zeros_like(tail_ref)
        h_ref[...] = jnp.zeros_like(h_ref)

    xr = xr_ref[...].astype(F32)
    ts = xr.shape[0]
    xe = jnp.concatenate([tail_ref[...], xr], axis=0)
    cw = cw_ref[...]
    xc = (cb_ref[...] + cw[3:4] * xr + cw[2:3] * xe[7:7 + ts]
          + cw[1:2] * xe[6:6 + ts] + cw[0:1] * xe[5:5 + ts])
    tail_ref[...] = xr[ts - 8:]

    xcb = xc.astype(BF16)

    def gate(w_ref, b_ref):
        parts = [jnp.dot(xcb[:, n * RNN_BW:(n + 1) * RNN_BW], w_ref[n],
                         preferred_element_type=F32) for n in range(RNN_BLOCKS)]
        return _sigmoid(jnp.concatenate(parts, axis=-1) + b_ref[...])

    r = gate(wrg_ref, brg_ref)
    i = gate(wig_ref, big_ref)
    nlam = -lam_ref[...]
    softplus = jnp.maximum(nlam, 0.0) + jnp.log1p(jnp.exp(-jnp.abs(nlam)))
    log_a = (-LRU_C) * r * softplus
    a = jnp.exp(log_a)
    bx = jnp.sqrt(jnp.tanh(-log_a) * (1.0 + a * a)) * (i * xc)

    groups = ts // SUBLANES
    a3 = a.reshape(groups, SUBLANES, a.shape[-1])
    b3 = bx.reshape(groups, SUBLANES, a.shape[-1])
    sub = lax.broadcasted_iota(I32, (1, SUBLANES, 1), 1)
    k = 1
    while k < SUBLANES:
        keep = sub >= k
        a_prev = pltpu.roll(a3, k, 1)
        b_prev = pltpu.roll(b3, k, 1)
        b3 = jnp.where(keep, a3 * b_prev, 0.0) + b3
        a3 = jnp.where(keep, a3 * a_prev, a3)
        k *= 2
    carry = h_ref[0:1]
    hs = []
    for gidx in range(groups):
        hg = a3[gidx] * carry + b3[gidx]
        hs.append(hg)
        carry = hg[SUBLANES - 1:SUBLANES]
    h = jnp.concatenate(hs, axis=0)
    h_ref[0:1] = carry

    yr = yr_ref[...].astype(F32)
    gelu = 0.5 * yr * (1.0 + jnp.tanh(0.7978845608028654 * (yr + 0.044715 * (yr * yr * yr))))
    o_ref[...] = jnp.dot((gelu * h).astype(BF16), wout_ref[...],
                         preferred_element_type=F32).astype(o_ref.dtype)


def _lru_branch(z, batch, seq, li, conv_w, conv_b, w_rg, b_rg, w_ig, b_ig, lam, w_rnn_out):
    ts = min(256, seq)
    ns = seq // ts
    c = D_RNN
    rows = lambda v: v.reshape(v.shape[0], 1, c)
    return pl.pallas_call(
        _lru_body,
        grid=(batch, ns),
        in_specs=[pl.BlockSpec((ts, c), lambda b, s: (b * ns + s, OFF_XR // c)),
                  pl.BlockSpec((ts, c), lambda b, s: (b * ns + s, OFF_YR // c)),
                  _layer_spec(li, (CONV_W, c)), _layer_spec(li, (1, c)),
                  _layer_spec(li, (RNN_BLOCKS, RNN_BW, RNN_BW)), _layer_spec(li, (1, c)),
                  _layer_spec(li, (RNN_BLOCKS, RNN_BW, RNN_BW)), _layer_spec(li, (1, c)),
                  _layer_spec(li, (1, c)), _layer_spec(li, (c, D_MODEL))],
        out_specs=pl.BlockSpec((ts, D_MODEL), lambda b, s: (b * ns + s, 0)),
        out_shape=jax.ShapeDtypeStruct((batch * seq, D_MODEL), BF16),
        scratch_shapes=[pltpu.VMEM((8, c), F32), pltpu.VMEM((8, c), F32)],
        compiler_params=_cparams(("arbitrary", "arbitrary")),
        name="lru_branch",
    )(z, z, conv_w, rows(conv_b), w_rg, rows(b_rg), w_ig, rows(b_ig), rows(lam), w_rnn_out)


ATT_QROWS = 128


def _attn_body(n_back, q_ref, kp_ref, kc_ref, vp_ref, vc_ref, o_ref, lse_ref, s_s, p_s):
    n = pl.program_id(2)
    blk = ATT_BLOCK
    nsub = q_ref.shape[2] // blk
    qi = lax.broadcasted_iota(I32, (blk, 2 * blk), 0)
    kj = lax.broadcasted_iota(I32, (blk, 2 * blk), 1)
    diff = blk + qi - kj
    in_window = (diff >= 0) & (diff <= n_back)
    lane = lax.broadcasted_iota(I32, (blk, LANES), 1)
    scale = jnp.asarray(ATT_HEAD_DIM ** -0.5, q_ref.dtype)
    low_half = lane < ATT_HEAD_DIM
    ones = jnp.ones((2 * blk, LANES), BF16)
    for sb in range(nsub):
        rows = slice(sb * blk, (sb + 1) * blk)
        q = q_ref[0, 0, rows, :] * scale
        if sb == 0:
            k = jnp.concatenate([kp_ref[0, 0], kc_ref[0, 0, rows, :]], axis=0)
            v = jnp.concatenate([vp_ref[0, 0], vc_ref[0, 0, rows, :]], axis=0)
            valid = in_window & ((kj >= blk) | (n > 0))
        else:
            k = kc_ref[0, 0, (sb - 1) * blk:(sb + 1) * blk, :]
            v = vc_ref[0, 0, (sb - 1) * blk:(sb + 1) * blk, :]
            valid = in_window
        for h in range(ATT_HEADS):
            tile = slice((h // 2) * LANES, (h // 2 + 1) * LANES)
            mine = low_half if h % 2 == 0 else ~low_half
            qh = jnp.where(mine, q[:, tile], jnp.zeros_like(q[:, tile]))
            s_s[h] = lax.dot_general(qh, k[:, tile], (((1,), (1,)), ((), ())),
                                     preferred_element_type=F32)
        m_tile = jnp.zeros((blk, LANES), F32)
        for h in range(ATT_HEADS):
            s = jnp.where(valid, s_s[h], NEG_BIG)
            m = jnp.max(s, axis=-1, keepdims=True)
            p_s[h] = jnp.exp(s - m).astype(BF16)
            m_tile = jnp.where(lane == h, m, m_tile)
        lse_tile = jnp.zeros((blk, LANES), F32)
        for pair in range(ATT_HEADS // 2):
            tile = slice(pair * LANES, (pair + 1) * LANES)
            o_pair, l_pair = [], []
            for h in (2 * pair, 2 * pair + 1):
                o_pair.append(jnp.dot(p_s[h], v[:, tile], preferred_element_type=F32))
                l_pair.append(jnp.dot(p_s[h], ones, preferred_element_type=F32))
                lse_tile = jnp.where(lane == h, m_tile + jnp.log(l_pair[-1]), lse_tile)
            o_ref[0, 0, rows, tile] = jnp.where(low_half, o_pair[0] / l_pair[0],
                                                o_pair[1] / l_pair[1]).astype(o_ref.dtype)
        lse_ref[0, 0, rows, :] = lse_tile


def _attention_group(qkv, gi):
    batch, dil, l, _ = qkv.shape
    n_back = ATT_GROUPS[gi][0] // dil
    qrows = min(ATT_QROWS, l)
    nsub = qrows // ATT_BLOCK
    cur = lambda c: pl.BlockSpec((1, 1, qrows, ATT_W), lambda b, r, n: (b, r, n, c))
    prev = lambda c: pl.BlockSpec((1, 1, ATT_BLOCK, ATT_W),
                                  lambda b, r, n: (b, r, jnp.maximum(n * nsub - 1, 0), c))
    return pl.pallas_call(
        functools.partial(_attn_body, n_back),
        grid=(batch, dil, l // qrows),
        in_specs=[cur(0), prev(1), cur(1), prev(2), cur(2)],
        out_specs=[pl.BlockSpec((1, 1, qrows, ATT_W), lambda b, r, n: (b, r, n, 0)),
                   pl.BlockSpec((1, 1, qrows, LANES), lambda b, r, n: (b, r, n, 0))],
        out_shape=[jax.ShapeDtypeStruct((batch, dil, l, ATT_W), BF16),
                   jax.ShapeDtypeStruct((batch, dil, l, LANES), F32)],
        scratch_shapes=[pltpu.VMEM((ATT_HEADS, ATT_BLOCK, 2 * ATT_BLOCK), F32),
                        pltpu.VMEM((ATT_HEADS, ATT_BLOCK, 2 * ATT_BLOCK), BF16)],
        compiler_params=_cparams(("parallel", "parallel", "arbitrary")),
        name=f"attn_g{gi}",
    )(qkv, qkv, qkv, qkv, qkv)


def _merge_body(o0_ref, o1_ref, o2_ref, l0_ref, l1_ref, l2_ref, ya_ref, ga_ref, gb_ref,
                x_ref, watt_ref, wout_ref, g_ref, b_ref, wr_ref, br_ref,
                x1_ref, x1t_ref, route_ref, cnt_ref, carry_ref, o_s, l_s):
    i = pl.program_id(0)

    @pl.when(i == 0)
    def _():
        carry_ref[...] = jnp.zeros_like(carry_ref)

    ts = x_ref.shape[0]

    def token_order(ref, scratch):
        dil = ref.shape[1]
        if dil == 1:
            return ref[0, 0].astype(F32)
        nl = ref.shape[3] // LANES
        for r in range(dil):
            v = ref[0, r].astype(F32)
            for lc in range(nl):
                scratch[lc, pl.ds(r, ts // dil, stride=dil), :] = v[:, lc * LANES:(lc + 1) * LANES]
        return jnp.concatenate([scratch[lc] for lc in range(nl)], axis=-1)

    group_o = [token_order(ref, o_s.at[gi]) for gi, ref in enumerate((o0_ref, o1_ref, o2_ref))]
    lses = [token_order(ref, l_s.at[gi]) for gi, ref in enumerate((l0_ref, l1_ref, l2_ref))]
    mx = jnp.maximum(jnp.maximum(lses[0], lses[1]), lses[2])
    es = [jnp.exp(v - mx) for v in lses]
    den = es[0] + es[1] + es[2]
    er = lax.broadcasted_iota(I32, (LANES, ATT_W), 0)
    ec = lax.broadcasted_iota(I32, (LANES, ATT_W), 1)
    expand = jnp.where(ec // ATT_HEAD_DIM == er, 1.0, 0.0).astype(BF16)
    o = jnp.zeros((ts, ATT_W), F32)
    for e, og in zip(es, group_o):
        w = e / den
        w_hi = w.astype(BF16)
        w_lo = (w - w_hi.astype(F32)).astype(BF16)
        wx = (jnp.dot(w_hi, expand, preferred_element_type=F32)
              + jnp.dot(w_lo, expand, preferred_element_type=F32))
        o = o + wx * og
    y_b = jnp.dot(o.astype(BF16), watt_ref[...], preferred_element_type=F32)
    ga = ga_ref[...].astype(F32)
    gb = gb_ref[...].astype(F32)
    merged = _sigmoid(ga) * ya_ref[...].astype(F32) + _sigmoid(gb) * y_b
    hmix = jnp.dot(merged.astype(BF16), wout_ref[...], preferred_element_type=F32)
    x1 = _layer_norm(ALPHA * x_ref[...] + hmix, g_ref[...], b_ref[...])
    x1_ref[...] = x1
    _to_row_tiles(x1t_ref, x1, ts)

    wr = wr_ref[...]
    wr_hi = wr.astype(BF16)
    wr_lo = (wr - wr_hi.astype(F32)).astype(BF16)
    x_hi = x1.astype(BF16)
    x_lo = (x1 - x_hi.astype(F32)).astype(BF16)
    logits = (jnp.dot(x_hi, wr_hi, preferred_element_type=F32)
              + jnp.dot(x_lo, wr_hi, preferred_element_type=F32)
              + jnp.dot(x_hi, wr_lo, preferred_element_type=F32)) + br_ref[...]

    el = lax.broadcasted_iota(I32, (ts, N_EXPERTS), 1)
    work = logits
    vals, idxs, hots = [], [], []
    for _ in range(TOP_K):
        m = jnp.max(work, axis=-1, keepdims=True)
        idx = jnp.min(jnp.where(work == m, el, N_EXPERTS), axis=-1, keepdims=True)
        hot = el == idx
        vals.append(m)
        idxs.append(idx)
        hots.append(hot)
        work = jnp.where(hot, NEG_BIG, work)
    exps = [jnp.exp(v - vals[0]) for v in vals]
    gden = exps[0] + exps[1] + exps[2] + exps[3]

    cnt = jnp.zeros((ts, N_EXPERTS), F32)
    for hot in hots:
        cnt = cnt + jnp.where(hot, 1.0, 0.0)
    tr = lax.broadcasted_iota(I32, (ts, ts), 0)
    tc = lax.broadcasted_iota(I32, (ts, ts), 1)
    tri = jnp.where(tc < tr, 1.0, 0.0).astype(BF16)
    before = jnp.dot(tri, cnt.astype(BF16), preferred_element_type=F32) + carry_ref[...]
    carry_ref[...] = carry_ref[...] + jnp.sum(cnt, axis=0, keepdims=True)
    cnt_ref[...] = carry_ref[...]

    lane = lax.broadcasted_iota(I32, (ts, LANES), 1)
    route = jnp.zeros((ts, LANES), I32)
    for kk in range(TOP_K):
        rank = jnp.sum(jnp.where(hots[kk], before, 0.0), axis=-1, keepdims=True).astype(I32)
        gate_bits = pltpu.bitcast(exps[kk] / gden, I32)
        route = jnp.where(lane == kk, idxs[kk], route)
        route = jnp.where(lane == TOP_K + kk, rank, route)
        route = jnp.where(lane == 2 * TOP_K + kk, gate_bits, route)
    route_ref[...] = route


def _merge(outs, lses, y_a, zd, x2d, seq, li, w_att_out, w_out, ln_g, ln_b, w_router, b_router):
    t = x2d.shape[0]
    ts = min(256, seq)
    ns = seq // ts
    rows = lambda v: v.reshape(v.shape[0], 1, v.shape[-1])
    tile = lambda w, c=0: pl.BlockSpec((ts, w), lambda i: (i, c))

    def by_residue(a):
        dil, w = a.shape[1], a.shape[3]
        return pl.BlockSpec((1, dil, ts // dil, w), lambda i: (i // ns, 0, i % ns, 0))

    return pl.pallas_call(
        _merge_body,
        grid=(t // ts,),
        in_specs=[by_residue(a) for a in (*outs, *lses)]
                 + [tile(D_MODEL), tile(D_MODEL, 2 * D_RNN // D_MODEL),
                    tile(D_MODEL, 2 * D_RNN // D_MODEL + 1), tile(D_MODEL),
                    _layer_spec(li, (ATT_W, D_MODEL)), _layer_spec(li, (D_MODEL, D_MODEL)),
                    _layer_spec(li, (1, D_MODEL)), _layer_spec(li, (1, D_MODEL)),
                    _layer_spec(li, (D_MODEL, N_EXPERTS)), _layer_spec(li, (1, N_EXPERTS))],
        out_specs=[tile(D_MODEL), pl.BlockSpec((ts * ROW_TILES, LANES), lambda i: (i, 0)),
                   tile(LANES), pl.BlockSpec((1, N_EXPERTS), lambda i: (0, 0))],
        out_shape=[jax.ShapeDtypeStruct((t, D_MODEL), F32),
                   jax.ShapeDtypeStruct((t * ROW_TILES, LANES), F32),
                   jax.ShapeDtypeStruct((t, LANES), I32),
                   jax.ShapeDtypeStruct((1, N_EXPERTS), F32)],
        scratch_shapes=[pltpu.VMEM((1, N_EXPERTS), F32),
                        pltpu.VMEM((N_GROUPS, ATT_W // LANES, ts, LANES), F32),
                        pltpu.VMEM((N_GROUPS, 1, ts, LANES), F32)],
        compiler_params=_cparams(("arbitrary",)),
        name="merge_ln1_router",
    )(*outs, *lses, y_a, zd, zd, x2d, w_att_out, w_out, rows(ln_g), rows(ln_b), w_router,
      rows(b_router))


INVERT_UNROLL = 8
GATHER_UNROLL = 8


def _build_row_order(dest_ref, fill_ref, n_ref, inv_ref):
    def zero_range(lo, hi):
        def z(row, carry):
            inv_ref[row] = 0
            return carry
        lax.fori_loop(lo, hi, z, 0)

    def per_expert(e, carry):
        zero_range(fill_ref[2 * e], fill_ref[2 * e + 1])
        return carry
    lax.fori_loop(0, N_EXPERTS, per_expert, 0)
    zero_range(fill_ref[2 * N_EXPERTS] * MOE_ROWS, fill_ref[2 * N_EXPERTS + 1] * MOE_ROWS)

    def chunk(c, carry):
        base = c * INVERT_UNROLL
        tok0 = c * (INVERT_UNROLL // TOP_K)
        for u in range(INVERT_UNROLL):
            inv_ref[dest_ref[base + u]] = tok0 + u // TOP_K
        return carry
    lax.fori_loop(0, n_ref[0] // INVERT_UNROLL, chunk, 0)


def _expert_body(li, be_ref, nvb_ref, first_ref, wslot_ref, nxt_ref, dest_ref, fill_ref, n_ref,
                 x1t_hbm, wg_hbm, wu_hbm, wd_hbm, bg_ref, bu_ref, bd_ref, y_ref,
                 inv_s, wst, wg_s, wu_s, wd_s, rows0, rows1, x_s, sem, wsem):
    j = pl.program_id(0)
    nvb = nvb_ref[0]
    weights = (wg_hbm, wu_hbm, wd_hbm)

    def issue(blk, buf, s):
        def chunk(c, carry):
            for u in range(GATHER_UNROLL):
                r = c * GATHER_UNROLL + u
                tok = inv_s[blk * MOE_ROWS + r]
                src = x1t_hbm.at[pl.ds(pl.multiple_of(tok * ROW_TILES, ROW_TILES), ROW_TILES)]
                dst = buf.at[pl.ds(pl.multiple_of(r * ROW_TILES, ROW_TILES), ROW_TILES)]
                pltpu.make_async_copy(src, dst, sem.at[s]).start()
            return carry
        lax.fori_loop(0, MOE_ROWS // GATHER_UNROLL, chunk, 0)

    def weight_copies(e, slot):
        return [pltpu.make_async_copy(w.at[li, e], wst.at[slot, k], wsem.at[slot])
                for k, w in enumerate(weights)]

    @pl.when(j == 0)
    def _():
        for cp in weight_copies(be_ref[0], 0):
            cp.start()
        _build_row_order(dest_ref, fill_ref, n_ref, inv_s)
        issue(0, rows0, 0)

    @pl.when(first_ref[j] == 1)
    def _():
        slot = wslot_ref[j]
        for cp in weight_copies(0, slot):
            cp.wait()
        wg_s[...] = wst[slot, 0].astype(BF16)
        wu_s[...] = wst[slot, 1].astype(BF16)
        wd_s[...] = wst[slot, 2].astype(BF16)

        @pl.when(nxt_ref[j] >= 0)
        def _():
            for cp in weight_copies(nxt_ref[j], 1 - slot):
                cp.start()

    def fetch(buf, s, other, so):
        @pl.when(j + 1 < nvb)
        def _():
            issue(j + 1, other, so)

        pltpu.make_async_copy(buf, buf, sem.at[s]).wait()
        x_s[...] = _from_row_tiles(buf, MOE_ROWS).astype(BF16)

    @pl.when((j < nvb) & (j % 2 == 0))
    def _():
        fetch(rows0, 0, rows1, 1)

    @pl.when((j < nvb) & (j % 2 == 1))
    def _():
        fetch(rows1, 1, rows0, 0)

    @pl.when(j >= nvb)
    def _():
        y_ref[...] = jnp.zeros_like(y_ref)

    @pl.when(j < nvb)
    def _():
        x = x_s[...]
        g = jnp.dot(x, wg_s[...], preferred_element_type=F32) + bg_ref[...]
        up = jnp.dot(x, wu_s[...], preferred_element_type=F32) + bu_ref[...]
        g = jnp.minimum(g, SWIGLU_LIMIT)
        up = jnp.clip(up, -SWIGLU_LIMIT, SWIGLU_LIMIT)
        hdn = (up + 1.0) * (g * _sigmoid(SWIGLU_ALPHA * g))
        y = jnp.dot(hdn.astype(BF16), wd_s[...], preferred_element_type=F32) + bd_ref[...]
        _to_row_tiles(y_ref, y, MOE_ROWS)


def _experts(x1t, tables, li, w_gate, b_gate, w_up, b_up, w_down, b_down):
    n_rows = tables['n_rows']
    d = D_MODEL
    blk = (MOE_ROWS * ROW_TILES, LANES)
    prefetch = (tables['blk_expert'], tables['n_used'], tables['first'], tables['wslot'],
                tables['next_expert'], tables['dest'], tables['fill'], tables['n_assign'])
    bspec = pl.BlockSpec((None, None, 1, d), lambda j, be, *_: (li, be[j], 0, 0))
    bias = lambda v: v.reshape(v.shape[0], v.shape[1], 1, v.shape[2])
    hbm = pl.BlockSpec(memory_space=pl.ANY)
    rows = pltpu.VMEM(blk, F32)
    return pl.pallas_call(
        functools.partial(_expert_body, li),
        grid_spec=pltpu.PrefetchScalarGridSpec(
            num_scalar_prefetch=len(prefetch), grid=(n_rows // MOE_ROWS,),
            in_specs=[hbm, hbm, hbm, hbm, bspec, bspec, bspec],
            out_specs=pl.BlockSpec(blk, lambda j, *_: (j, 0)),
            scratch_shapes=[pltpu.SMEM((n_rows,), I32), pltpu.VMEM((2, 3, d, D_FF), F32),
                            pltpu.VMEM((d, D_FF), BF16), pltpu.VMEM((d, D_FF), BF16),
                            pltpu.VMEM((D_FF, d), BF16), rows, rows,
                            pltpu.VMEM((MOE_ROWS, d), BF16), pltpu.SemaphoreType.DMA((2,)),
                            pltpu.SemaphoreType.DMA((2,))]),
        out_shape=jax.ShapeDtypeStruct((n_rows * ROW_TILES, LANES), F32),
        compiler_params=_cparams(("arbitrary",)),
        name="moe_experts",
    )(*prefetch, x1t, w_gate, w_up, w_down, bias(b_gate), bias(b_up), bias(b_down))


def _combine_body(dest_ref, yb_hbm, route_ref, x1_ref, p_ref, wple_ref, wpg_ref, bpg_ref,
                  g2_ref, b2_ref, g3_ref, b3_ref, o_ref, buf0, buf1, y_s, sem):
    i = pl.program_id(0)
    n = pl.num_programs(0)
    ts = x1_ref.shape[0]

    def issue(tile, buf, s):
        def tok(j, carry):
            a = (tile * ts + j) * TOP_K
            for kk in range(TOP_K):
                row = dest_ref[a + kk]
                src = yb_hbm.at[pl.ds(pl.multiple_of(row * ROW_TILES, ROW_TILES), ROW_TILES)]
                dst = buf.at[kk, pl.ds(pl.multiple_of(j * ROW_TILES, ROW_TILES), ROW_TILES)]
                pltpu.make_async_copy(src, dst, sem.at[s]).start()
            return carry
        lax.fori_loop(0, ts, tok, 0)

    @pl.when(i == 0)
    def _():
        issue(0, buf0, 0)

    route = route_ref[...]
    gates = [pltpu.bitcast(route[:, 2 * TOP_K + kk:2 * TOP_K + kk + 1], F32)
             for kk in range(TOP_K)]

    def phase(buf, s, other, so):
        @pl.when(i + 1 < n)
        def _():
            issue(i + 1, other, so)

        pltpu.make_async_copy(buf, buf, sem.at[s]).wait()
        for c in range(ROW_TILES):
            acc = jnp.zeros((ts, LANES), F32)
            for kk in range(TOP_K):
                acc = acc + gates[kk] * buf[kk, pl.ds(c, ts, stride=ROW_TILES), :]
            y_s[:, c * LANES:(c + 1) * LANES] = acc

    @pl.when(i % 2 == 0)
    def _():
        phase(buf0, 0, buf1, 1)

    @pl.when(i % 2 == 1)
    def _():
        phase(buf1, 1, buf0, 0)

    x2 = _layer_norm(ALPHA * x1_ref[...] + y_s[...], g2_ref[...], b2_ref[...])
    emb = jnp.dot(p_ref[...].astype(BF16), wple_ref[...], preferred_element_type=F32)
    gate_in = jnp.dot(x2.astype(BF16), wpg_ref[...], preferred_element_type=F32) + bpg_ref[...]
    ple = emb * _sigmoid(gate_in)
    o_ref[...] = _layer_norm(ALPHA * x2 + ple, g3_ref[...], b3_ref[...])


def _combine(dest_flat, yb, route, x1, p3d, li, w_ple, w_ple_gate, b_ple_gate, ln2_g, ln2_b,
             ln3_g, ln3_b):
    t, d = x1.shape
    ts = min(256, t)
    rows = lambda v: v.reshape(v.shape[0], 1, v.shape[-1])
    tile = lambda w: pl.BlockSpec((ts, w), lambda i, dst: (i, 0))
    buf = pltpu.VMEM((TOP_K, ts * ROW_TILES, LANES), F32)
    return pl.pallas_call(
        _combine_body,
        grid_spec=pltpu.PrefetchScalarGridSpec(
            num_scalar_prefetch=1, grid=(t // ts,),
            in_specs=[pl.BlockSpec(memory_space=pl.ANY), tile(LANES), tile(d),
                      pl.BlockSpec((None, ts, PLE_DIM), lambda i, dst: (li, i, 0)),
                      _layer_spec(li, (PLE_DIM, d)), _layer_spec(li, (d, d)),
                      _layer_spec(li, (1, d)), _layer_spec(li, (1, d)), _layer_spec(li, (1, d)),
                      _layer_spec(li, (1, d)), _layer_spec(li, (1, d))],
            out_specs=tile(d),
            scratch_shapes=[buf, buf, pltpu.VMEM((ts, d), F32), pltpu.SemaphoreType.DMA((2,))]),
        out_shape=jax.ShapeDtypeStruct((t, d), F32),
        compiler_params=_cparams(("arbitrary",)),
        name="moe_combine_ple",
    )(dest_flat, yb, route, x1, p3d, w_ple, w_ple_gate, rows(b_ple_gate), rows(ln2_g), rows(ln2_b),
      rows(ln3_g), rows(ln3_b))


def _routing_tables(route, counts, t):
    counts = counts.reshape(N_EXPERTS).astype(I32)
    padded = (counts + MOE_ROWS - 1) // MOE_ROWS * MOE_ROWS
    pend = jnp.cumsum(padded)
    pstart = pend - padded
    top_e = route[:, 0:TOP_K]
    rank = route[:, TOP_K:2 * TOP_K]
    onehot = top_e[:, :, None] == jnp.arange(N_EXPERTS, dtype=I32)
    dest = (jnp.sum(jnp.where(onehot, pstart, 0), axis=-1) + rank).reshape(t * TOP_K)
    n_blk = t * TOP_K // MOE_ROWS + N_EXPERTS
    n_used = pend[N_EXPERTS - 1] // MOE_ROWS
    fill = jnp.concatenate([jnp.stack([pstart + counts, pend], axis=1).reshape(2 * N_EXPERTS),
                            jnp.stack([n_used, jnp.asarray(n_blk, I32)])]).astype(I32)
    blk = jnp.arange(n_blk, dtype=I32)
    blk_expert = jnp.minimum(jnp.sum((pend[None, :] <= (blk * MOE_ROWS)[:, None]).astype(I32),
                                     axis=1), N_EXPERTS - 1)
    first = (blk < n_used) & ((blk == 0) | (blk_expert != jnp.roll(blk_expert, 1)))
    wslot = (jnp.cumsum(first.astype(I32)) - 1) % 2
    first_at = jnp.where(first, blk, n_blk)
    next_first = jnp.concatenate([lax.cummin(first_at[::-1])[::-1][1:],
                                  jnp.full((1,), n_blk, I32)])
    next_expert = jnp.where(next_first < n_blk,
                            blk_expert[jnp.minimum(next_first, n_blk - 1)], -1)
    return dict(dest=dest, fill=fill, blk_expert=blk_expert, n_used=n_used.reshape(1),
                first=first.astype(I32), wslot=wslot.astype(I32),
                next_expert=next_expert.astype(I32), n_assign=jnp.full((1,), t * TOP_K, I32),
                n_rows=n_blk * MOE_ROWS)


def _layer(x2d, p3d, batch, seq, li, w):
    t = batch * seq
    zd = _proj_dense(x2d, w['w_in'], li)
    y_a = _lru_branch(zd, batch, seq, li, w['conv_w'], w['conv_b'], w['w_rg'], w['b_rg'],
                      w['w_ig'], w['b_ig'], w['lru_lambda'], w['w_rnn_out'])
    outs, lses = [], []
    for gi in range(N_GROUPS):
        o, lse = _attention_group(_proj_qkv(x2d, w['w_in'], li, batch, seq, gi), gi)
        outs.append(o)
        lses.append(lse)
    x1, x1t, route, counts = _merge(outs, lses, y_a, zd, x2d, seq, li, w['w_att_out'], w['w_out'],
                                    w['ln1_g'], w['ln1_b'], w['w_router'], w['b_router'])
    tables = _routing_tables(route, counts, t)
    yb = _experts(x1t, tables, li, w['w_gate'], w['b_gate'], w['w_up'], w['b_up'], w['w_down'],
                  w['b_down'])
    return _combine(tables['dest'], yb, route, x1, p3d, li, w['w_ple'], w['w_ple_gate'],
                    w['b_ple_gate'],
                    w['ln2_g'], w['ln2_b'], w['ln3_g'], w['ln3_b'])


_WEIGHT_NAMES = ('w_in', 'conv_w', 'conv_b', 'w_rg', 'b_rg', 'w_ig', 'b_ig', 'lru_lambda',
                 'w_rnn_out', 'w_att_out', 'w_out', 'ln1_g', 'ln1_b', 'w_router', 'b_router',
                 'w_gate', 'b_gate', 'w_up', 'b_up', 'w_down', 'b_down', 'ln2_g', 'ln2_b',
                 'w_ple', 'w_ple_gate', 'b_ple_gate', 'ln3_g', 'ln3_b')
_BF16_WEIGHTS = ('w_in', 'w_rg', 'w_ig', 'w_rnn_out', 'w_att_out', 'w_out', 'w_ple', 'w_ple_gate')


def kernel(x, p, w_in, conv_w, conv_b, w_rg, b_rg, w_ig, b_ig, lru_lambda, w_rnn_out, w_att_out,
           w_out, ln1_g, ln1_b, w_router, b_router, w_gate, b_gate, w_up, b_up, w_down, b_down,
           ln2_g, ln2_b, w_ple, w_ple_gate, b_ple_gate, ln3_g, ln3_b):
    w = dict(zip(_WEIGHT_NAMES, (
        w_in, conv_w, conv_b, w_rg, b_rg, w_ig, b_ig, lru_lambda, w_rnn_out, w_att_out, w_out,
        ln1_g, ln1_b, w_router, b_router, w_gate, b_gate, w_up, b_up, w_down, b_down, ln2_g,
        ln2_b, w_ple, w_ple_gate, b_ple_gate, ln3_g, ln3_b)))
    for name in _BF16_WEIGHTS:
        w[name] = w[name].astype(BF16)
    batch, seq, d = x.shape
    depth = p.shape[0]
    x2d = x.reshape(batch * seq, d)
    p3d = p.reshape(depth, batch * seq, PLE_DIM)
    for li in range(depth):
        x2d = _layer(x2d, p3d, batch, seq, li, w)
    return x2d.reshape(batch, seq, d)
```

```python
import functools

import jax
import jax.numpy as jnp
from jax import lax
from jax.experimental import pallas as pl
from jax.experimental.pallas import tpu as pltpu

F32 = jnp.float32
BF16 = jnp.bfloat16
I32 = jnp.int32

D_MODEL = 1024
DEPTH = 2
D_RNN = 1024
RNN_BLOCKS = 4
RNN_BW = D_RNN // RNN_BLOCKS
CONV_W = 4
LRU_C = 8.0
ATT_GROUPS = ((128, 1), (512, 4), (2048, 16))
N_GROUPS = len(ATT_GROUPS)
ATT_HEADS = 8
ATT_HEAD_DIM = 64
ATT_W = ATT_HEADS * ATT_HEAD_DIM
ATT_BLOCK = 128
OFF_XR = 0
OFF_YR = OFF_XR + D_RNN
OFF_Q = OFF_YR + D_RNN
OFF_K = OFF_Q + N_GROUPS * ATT_W
OFF_V = OFF_K + N_GROUPS * ATT_W
OFF_GA = OFF_V + N_GROUPS * ATT_W
OFF_GB = OFF_GA + D_MODEL
N_IN = OFF_GB + D_MODEL
N_EXPERTS = 32
TOP_K = 4
D_FF = D_MODEL
SWIGLU_ALPHA = 1.702
SWIGLU_LIMIT = 7.0
PLE_DIM = 256
ALPHA = (2.0 * DEPTH) ** 0.25
LN_EPS = 1e-5

LANES = 128
SUBLANES = 8
ROW_TILES = D_MODEL // LANES
COLB = ATT_W
N_COLB = N_IN // COLB
MOE_ROWS = 256
VMEM_LIMIT = 56 * 1024 * 1024
NEG_BIG = -1e30

assert ROW_TILES == SUBLANES


def _cparams(sem):
    return pltpu.CompilerParams(dimension_semantics=sem, vmem_limit_bytes=VMEM_LIMIT)


def _layer_spec(li, shape):
    return pl.BlockSpec((None,) + tuple(shape), lambda *_: (li,) + (0,) * len(shape))


def _layer_norm(v, g, b):
    mu = jnp.mean(v, axis=-1, keepdims=True)
    c = v - mu
    var = jnp.mean(c * c, axis=-1, keepdims=True)
    return c * lax.rsqrt(var + LN_EPS) * g + b


def _sigmoid(v):
    return 1.0 / (1.0 + jnp.exp(-v))


def _to_row_tiles(ref, m, rows):
    for c in range(ROW_TILES):
        ref[pl.ds(c, rows, stride=ROW_TILES), :] = m[:, c * LANES:(c + 1) * LANES]


def _from_row_tiles(ref, rows):
    return jnp.concatenate([ref[pl.ds(c, rows, stride=ROW_TILES), :] for c in range(ROW_TILES)],
                           axis=-1)


PROJ_ROWS = 512
DENSE_W = 2 * D_RNN + 2 * D_MODEL


def _proj_dense_body(x_ref, wlo_ref, w0_ref, w1_ref, w2_ref, w3_ref, o_ref):
    x = x_ref[...].astype(BF16)
    lo = 2 * D_RNN
    o_ref[:, 0:lo] = jnp.dot(x, wlo_ref[...], preferred_element_type=F32).astype(o_ref.dtype)
    for c, w_ref in enumerate((w0_ref, w1_ref, w2_ref, w3_ref)):
        o_ref[:, lo + c * COLB:lo + (c + 1) * COLB] = jnp.dot(
            x, w_ref[...], preferred_element_type=F32).astype(o_ref.dtype)


def _proj_dense(x2d, w_in, li):
    t, d = x2d.shape
    tm = min(PROJ_ROWS, t)
    g0 = OFF_GA // COLB
    wcol = lambda c: pl.BlockSpec((None, d, COLB), lambda i: (li, 0, c))
    return pl.pallas_call(
        _proj_dense_body,
        grid=(t // tm,),
        in_specs=[pl.BlockSpec((tm, d), lambda i: (i, 0)),
                  pl.BlockSpec((None, d, 2 * D_RNN), lambda i: (li, 0, 0)),
                  wcol(g0), wcol(g0 + 1), wcol(g0 + 2), wcol(g0 + 3)],
        out_specs=pl.BlockSpec((tm, DENSE_W), lambda i: (i, 0)),
        out_shape=jax.ShapeDtypeStruct((t, DENSE_W), BF16),
        compiler_params=_cparams(("parallel",)),
        name="proj_dense",
    )(x2d, w_in, w_in, w_in, w_in, w_in)


def _proj_qkv_body(dil, x_ref, wq_ref, wk_ref, wv_ref, o_ref, res_s):
    x = x_ref[...].astype(BF16)
    rows = x.shape[0] // dil
    for c, w_ref in enumerate((wq_ref, wk_ref, wv_ref)):
        res = jnp.dot(x, w_ref[...], preferred_element_type=F32)
        cols = slice(c * ATT_W, (c + 1) * ATT_W)
        if dil == 1:
            o_ref[0, 0, :, cols] = res.astype(o_ref.dtype)
        else:
            for lc in range(ATT_W // LANES):
                res_s[lc] = res[:, lc * LANES:(lc + 1) * LANES]
            for r in range(dil):
                part = [res_s[lc, pl.ds(r, rows, stride=dil), :] for lc in range(ATT_W // LANES)]
                o_ref[0, r, :, cols] = jnp.concatenate(part, axis=-1).astype(o_ref.dtype)


def _proj_qkv(x2d, w_in, li, batch, seq, gi):
    d = x2d.shape[1]
    dil = ATT_GROUPS[gi][1]
    tm = min(PROJ_ROWS, seq)
    ns = seq // tm
    wcol = lambda off: pl.BlockSpec((None, d, ATT_W), lambda b, s: (li, 0, off // ATT_W + gi))
    return pl.pallas_call(
        functools.partial(_proj_qkv_body, dil),
        grid=(batch, ns),
        in_specs=[pl.BlockSpec((tm, d), lambda b, s: (b * ns + s, 0)),
                  wcol(OFF_Q), wcol(OFF_K), wcol(OFF_V)],
        out_specs=pl.BlockSpec((1, dil, tm // dil, 3 * ATT_W), lambda b, s: (b, 0, s, 0)),
        out_shape=jax.ShapeDtypeStruct((batch, dil, seq // dil, 3 * ATT_W), BF16),
        scratch_shapes=[pltpu.VMEM((ATT_W // LANES, tm, LANES), F32)],
        compiler_params=_cparams(("parallel", "parallel")),
        name=f"proj_qkv_g{gi}",
    )(x2d, w_in, w_in, w_in)


def _lru_body(xr_ref, yr_ref, cw_ref, cb_ref, wrg_ref, brg_ref, wig_ref, big_ref, lam_ref,
              wout_ref, o_ref, tail_ref, h_ref):
    s = pl.program_id(1)

    @pl.when(s == 0)
    def _():
        tail_ref[...] = jnp.zeros_like(tail_ref)
        h_ref[...] = jnp.zeros_like(h_ref)

    xr = xr_ref[...].astype(F32)
    ts = xr.shape[0]
    xe = jnp.concatenate([tail_ref[...], xr], axis=0)
    cw = cw_ref[...]
    xc = (cb_ref[...] + cw[3:4] * xr + cw[2:3] * xe[7:7 + ts]
          + cw[1:2] * xe[6:6 + ts] + cw[0:1] * xe[5:5 + ts])
    tail_ref[...] = xr[ts - 8:]

    xcb = xc.astype(BF16)

    def gate(w_ref, b_ref):
        parts = [jnp.dot(xcb[:, n * RNN_BW:(n + 1) * RNN_BW], w_ref[n],
                         preferred_element_type=F32) for n in range(RNN_BLOCKS)]
        return _sigmoid(jnp.concatenate(parts, axis=-1) + b_ref[...])

    r = gate(wrg_ref, brg_ref)
    i = gate(wig_ref, big_ref)
    nlam = -lam_ref[...]
    softplus = jnp.maximum(nlam, 0.0) + jnp.log1p(jnp.exp(-jnp.abs(nlam)))
    log_a = (-LRU_C) * r * softplus
    a = jnp.exp(log_a)
    bx = jnp.sqrt(jnp.tanh(-log_a) * (1.0 + a * a)) * (i * xc)

    groups = ts // SUBLANES
    a3 = a.reshape(groups, SUBLANES, a.shape[-1])
    b3 = bx.reshape(groups, SUBLANES, a.shape[-1])
    sub = lax.broadcasted_iota(I32, (1, SUBLANES, 1), 1)
    k = 1
    while k < SUBLANES:
        keep = sub >= k
        a_prev = pltpu.roll(a3, k, 1)
        b_prev = pltpu.roll(b3, k, 1)
        b3 = jnp.where(keep, a3 * b_prev, 0.0) + b3
        a3 = jnp.where(keep, a3 * a_prev, a3)
        k *= 2
    carry = h_ref[0:1]
    hs = []
    for gidx in range(groups):
        hg = a3[gidx] * carry + b3[gidx]
        hs.append(hg)
        carry = hg[SUBLANES - 1:SUBLANES]
    h = jnp.concatenate(hs, axis=0)
    h_ref[0:1] = carry

    yr = yr_ref[...].astype(F32)
    gelu = 0.5 * yr * (1.0 + jnp.tanh(0.7978845608028654 * (yr + 0.044715 * (yr * yr * yr))))
    o_ref[...] = jnp.dot((gelu * h).astype(BF16), wout_ref[...],
                         preferred_element_type=F32).astype(o_ref.dtype)


def _lru_branch(z, batch, seq, li, conv_w, conv_b, w_rg, b_rg, w_ig, b_ig, lam, w_rnn_out):
    ts = min(256, seq)
    ns = seq // ts
    c = D_RNN
    rows = lambda v: v.reshape(v.shape[0], 1, c)
    return pl.pallas_call(
        _lru_body,
        grid=(batch, ns),
        in_specs=[pl.BlockSpec((ts, c), lambda b, s: (b * ns + s, OFF_XR // c)),
                  pl.BlockSpec((ts, c), lambda b, s: (b * ns + s, OFF_YR // c)),
                  _layer_spec(li, (CONV_W, c)), _layer_spec(li, (1, c)),
                  _layer_spec(li, (RNN_BLOCKS, RNN_BW, RNN_BW)), _layer_spec(li, (1, c)),
                  _layer_spec(li, (RNN_BLOCKS, RNN_BW, RNN_BW)), _layer_spec(li, (1, c)),
                  _layer_spec(li, (1, c)), _layer_spec(li, (c, D_MODEL))],
        out_specs=pl.BlockSpec((ts, D_MODEL), lambda b, s: (b * ns + s, 0)),
        out_shape=jax.ShapeDtypeStruct((batch * seq, D_MODEL), BF16),
        scratch_shapes=[pltpu.VMEM((8, c), F32), pltpu.VMEM((8, c), F32)],
        compiler_params=_cparams(("arbitrary", "arbitrary")),
        name="lru_branch",
    )(z, z, conv_w, rows(conv_b), w_rg, rows(b_rg), w_ig, rows(b_ig), rows(lam), w_rnn_out)


ATT_QROWS = 512


def _attn_body(n_back, q_ref, kp_ref, kc_ref, vp_ref, vc_ref, o_ref, lse_ref, s_s, p_s):
    n = pl.program_id(2)
    blk = ATT_BLOCK
    nsub = q_ref.shape[2] // blk
    qi = lax.broadcasted_iota(I32, (blk, 2 * blk), 0)
    kj = lax.broadcasted_iota(I32, (blk, 2 * blk), 1)
    diff = blk + qi - kj
    in_window = (diff >= 0) & (diff <= n_back)
    lane = lax.broadcasted_iota(I32, (blk, LANES), 1)
    scale = jnp.asarray(ATT_HEAD_DIM ** -0.5, q_ref.dtype)
    low_half = lane < ATT_HEAD_DIM
    ones = jnp.ones((2 * blk, LANES), BF16)
    for sb in range(nsub):
        rows = slice(sb * blk, (sb + 1) * blk)
        q = q_ref[0, 0, rows, :] * scale
        if sb == 0:
            k = jnp.concatenate([kp_ref[0, 0], kc_ref[0, 0, rows, :]], axis=0)
            v = jnp.concatenate([vp_ref[0, 0], vc_ref[0, 0, rows, :]], axis=0)
            valid = in_window & ((kj >= blk) | (n > 0))
        else:
            k = kc_ref[0, 0, (sb - 1) * blk:(sb + 1) * blk, :]
            v = vc_ref[0, 0, (sb - 1) * blk:(sb + 1) * blk, :]
            valid = in_window
        for h in range(ATT_HEADS):
            tile = slice((h // 2) * LANES, (h // 2 + 1) * LANES)
            mine = low_half if h % 2 == 0 else ~low_half
            qh = jnp.where(mine, q[:, tile], jnp.zeros_like(q[:, tile]))
            s_s[h] = lax.dot_general(qh, k[:, tile], (((1,), (1,)), ((), ())),
                                     preferred_element_type=F32)
        m_tile = jnp.zeros((blk, LANES), F32)
        for h in range(ATT_HEADS):
            s = jnp.where(valid, s_s[h], NEG_BIG)
            m = jnp.max(s, axis=-1, keepdims=True)
            p_s[h] = jnp.exp(s - m).astype(BF16)
            m_tile = jnp.where(lane == h, m, m_tile)
        lse_tile = jnp.zeros((blk, LANES), F32)
        for pair in range(ATT_HEADS // 2):
            tile = slice(pair * LANES, (pair + 1) * LANES)
            o_pair, l_pair = [], []
            for h in (2 * pair, 2 * pair + 1):
                o_pair.append(jnp.dot(p_s[h], v[:, tile], preferred_element_type=F32))
                l_pair.append(jnp.dot(p_s[h], ones, preferred_element_type=F32))
                lse_tile = jnp.where(lane == h, m_tile + jnp.log(l_pair[-1]), lse_tile)
            o_ref[0, 0, rows, tile] = jnp.where(low_half, o_pair[0] / l_pair[0],
                                                o_pair[1] / l_pair[1]).astype(o_ref.dtype)
        lse_ref[0, 0, rows, :] = lse_tile


def _attention_group(qkv, gi):
    batch, dil, l, _ = qkv.shape
    n_back = ATT_GROUPS[gi][0] // dil
    qrows = min(ATT_QROWS, l)
    nsub = qrows // ATT_BLOCK
    cur = lambda c: pl.BlockSpec((1, 1, qrows, ATT_W), lambda b, r, n: (b, r, n, c))
    prev = lambda c: pl.BlockSpec((1, 1, ATT_BLOCK, ATT_W),
                                  lambda b, r, n: (b, r, jnp.maximum(n * nsub - 1, 0), c))
    return pl.pallas_call(
        functools.partial(_attn_body, n_back),
        grid=(batch, dil, l // qrows),
        in_specs=[cur(0), prev(1), cur(1), prev(2), cur(2)],
        out_specs=[pl.BlockSpec((1, 1, qrows, ATT_W), lambda b, r, n: (b, r, n, 0)),
                   pl.BlockSpec((1, 1, qrows, LANES), lambda b, r, n: (b, r, n, 0))],
        out_shape=[jax.ShapeDtypeStruct((batch, dil, l, ATT_W), BF16),
                   jax.ShapeDtypeStruct((batch, dil, l, LANES), F32)],
        scratch_shapes=[pltpu.VMEM((ATT_HEADS, ATT_BLOCK, 2 * ATT_BLOCK), F32),
                        pltpu.VMEM((ATT_HEADS, ATT_BLOCK, 2 * ATT_BLOCK), BF16)],
        compiler_params=_cparams(("parallel", "parallel", "arbitrary")),
        name=f"attn_g{gi}",
    )(qkv, qkv, qkv, qkv, qkv)


def _merge_body(o0_ref, o1_ref, o2_ref, l0_ref, l1_ref, l2_ref, ya_ref, ga_ref, gb_ref,
                x_ref, watt_ref, wout_ref, g_ref, b_ref, wr_ref, br_ref,
                x1_ref, x1t_ref, route_ref, cnt_ref, carry_ref, o_s, l_s):
    i = pl.program_id(0)

    @pl.when(i == 0)
    def _():
        carry_ref[...] = jnp.zeros_like(carry_ref)

    ts = x_ref.shape[0]

    def token_order(ref, scratch):
        dil = ref.shape[1]
        if dil == 1:
            return ref[0, 0].astype(F32)
        nl = ref.shape[3] // LANES
        for r in range(dil):
            v = ref[0, r].astype(F32)
            for lc in range(nl):
                scratch[lc, pl.ds(r, ts // dil, stride=dil), :] = v[:, lc * LANES:(lc + 1) * LANES]
        return jnp.concatenate([scratch[lc] for lc in range(nl)], axis=-1)

    group_o = [token_order(ref, o_s.at[gi]) for gi, ref in enumerate((o0_ref, o1_ref, o2_ref))]
    lses = [token_order(ref, l_s.at[gi]) for gi, ref in enumerate((l0_ref, l1_ref, l2_ref))]
    mx = jnp.maximum(jnp.maximum(lses[0], lses[1]), lses[2])
    es = [jnp.exp(v - mx) for v in lses]
    den = es[0] + es[1] + es[2]
    er = lax.broadcasted_iota(I32, (LANES, ATT_W), 0)
    ec = lax.broadcasted_iota(I32, (LANES, ATT_W), 1)
    expand = jnp.where(ec // ATT_HEAD_DIM == er, 1.0, 0.0).astype(BF16)
    o = jnp.zeros((ts, ATT_W), F32)
    for e, og in zip(es, group_o):
        w = e / den
        w_hi = w.astype(BF16)
        w_lo = (w - w_hi.astype(F32)).astype(BF16)
        wx = (jnp.dot(w_hi, expand, preferred_element_type=F32)
              + jnp.dot(w_lo, expand, preferred_element_type=F32))
        o = o + wx * og
    y_b = jnp.dot(o.astype(BF16), watt_ref[...], preferred_element_type=F32)
    ga = ga_ref[...].astype(F32)
    gb = gb_ref[...].astype(F32)
    merged = _sigmoid(ga) * ya_ref[...].astype(F32) + _sigmoid(gb) * y_b
    hmix = jnp.dot(merged.astype(BF16), wout_ref[...], preferred_element_type=F32)
    x1 = _layer_norm(ALPHA * x_ref[...] + hmix, g_ref[...], b_ref[...])
    x1_ref[...] = x1
    _to_row_tiles(x1t_ref, x1, ts)

    wr = wr_ref[...]
    wr_hi = wr.astype(BF16)
    wr_lo = (wr - wr_hi.astype(F32)).astype(BF16)
    x_hi = x1.astype(BF16)
    x_lo = (x1 - x_hi.astype(F32)).astype(BF16)
    logits = (jnp.dot(x_hi, wr_hi, preferred_element_type=F32)
              + jnp.dot(x_lo, wr_hi, preferred_element_type=F32)
              + jnp.dot(x_hi, wr_lo, preferred_element_type=F32)) + br_ref[...]

    el = lax.broadcasted_iota(I32, (ts, N_EXPERTS), 1)
    work = logits
    vals, idxs, hots = [], [], []
    for _ in range(TOP_K):
        m = jnp.max(work, axis=-1, keepdims=True)
        idx = jnp.min(jnp.where(work == m, el, N_EXPERTS), axis=-1, keepdims=True)
        hot = el == idx
        vals.append(m)
        idxs.append(idx)
        hots.append(hot)
        work = jnp.where(hot, NEG_BIG, work)
    exps = [jnp.exp(v - vals[0]) for v in vals]
    gden = exps[0] + exps[1] + exps[2] + exps[3]

    cnt = jnp.zeros((ts, N_EXPERTS), F32)
    for hot in hots:
        cnt = cnt + jnp.where(hot, 1.0, 0.0)
    tr = lax.broadcasted_iota(I32, (ts, ts), 0)
    tc = lax.broadcasted_iota(I32, (ts, ts), 1)
    tri = jnp.where(tc < tr, 1.0, 0.0).astype(BF16)
    before = jnp.dot(tri, cnt.astype(BF16), preferred_element_type=F32) + carry_ref[...]
    carry_ref[...] = carry_ref[...] + jnp.sum(cnt, axis=0, keepdims=True)
    cnt_ref[...] = carry_ref[...]

    lane = lax.broadcasted_iota(I32, (ts, LANES), 1)
    route = jnp.zeros((ts, LANES), I32)
    for kk in range(TOP_K):
        rank = jnp.sum(jnp.where(hots[kk], before, 0.0), axis=-1, keepdims=True).astype(I32)
        gate_bits = pltpu.bitcast(exps[kk] / gden, I32)
        route = jnp.where(lane == kk, idxs[kk], route)
        route = jnp.where(lane == TOP_K + kk, rank, route)
        route = jnp.where(lane == 2 * TOP_K + kk, gate_bits, route)
    route_ref[...] = route


def _merge(outs, lses, y_a, zd, x2d, seq, li, w_att_out, w_out, ln_g, ln_b, w_router, b_router):
    t = x2d.shape[0]
    ts = min(256, seq)
    ns = seq // ts
    rows = lambda v: v.reshape(v.shape[0], 1, v.shape[-1])
    tile = lambda w, c=0: pl.BlockSpec((ts, w), lambda i: (i, c))

    def by_residue(a):
        dil, w = a.shape[1], a.shape[3]
        return pl.BlockSpec((1, dil, ts // dil, w), lambda i: (i // ns, 0, i % ns, 0))

    return pl.pallas_call(
        _merge_body,
        grid=(t // ts,),
        in_specs=[by_residue(a) for a in (*outs, *lses)]
                 + [tile(D_MODEL), tile(D_MODEL, 2 * D_RNN // D_MODEL),
                    tile(D_MODEL, 2 * D_RNN // D_MODEL + 1), tile(D_MODEL),
                    _layer_spec(li, (ATT_W, D_MODEL)), _layer_spec(li, (D_MODEL, D_MODEL)),
                    _layer_spec(li, (1, D_MODEL)), _layer_spec(li, (1, D_MODEL)),
                    _layer_spec(li, (D_MODEL, N_EXPERTS)), _layer_spec(li, (1, N_EXPERTS))],
        out_specs=[tile(D_MODEL), pl.BlockSpec((ts * ROW_TILES, LANES), lambda i: (i, 0)),
                   tile(LANES), pl.BlockSpec((1, N_EXPERTS), lambda i: (0, 0))],
        out_shape=[jax.ShapeDtypeStruct((t, D_MODEL), F32),
                   jax.ShapeDtypeStruct((t * ROW_TILES, LANES), F32),
                   jax.ShapeDtypeStruct((t, LANES), I32),
                   jax.ShapeDtypeStruct((1, N_EXPERTS), F32)],
        scratch_shapes=[pltpu.VMEM((1, N_EXPERTS), F32),
                        pltpu.VMEM((N_GROUPS, ATT_W // LANES, ts, LANES), F32),
                        pltpu.VMEM((N_GROUPS, 1, ts, LANES), F32)],
        compiler_params=_cparams(("arbitrary",)),
        name="merge_ln1_router",
    )(*outs, *lses, y_a, zd, zd, x2d, w_att_out, w_out, rows(ln_g), rows(ln_b), w_router,
      rows(b_router))


INVERT_UNROLL = 32
ROW_BUFS = 3


def _build_row_order(dest_ref, fill_ref, n_ref, inv_ref):
    def zero_range(lo, hi):
        def z(row, carry):
            inv_ref[row] = 0
            return carry
        lax.fori_loop(lo, hi, z, 0)

    def per_expert(e, carry):
        zero_range(fill_ref[2 * e], fill_ref[2 * e + 1])
        return carry
    lax.fori_loop(0, N_EXPERTS, per_expert, 0)
    zero_range(fill_ref[2 * N_EXPERTS] * MOE_ROWS, fill_ref[2 * N_EXPERTS + 1] * MOE_ROWS)

    def chunk(c, carry):
        base = c * INVERT_UNROLL
        tok0 = c * (INVERT_UNROLL // TOP_K)
        for u in range(INVERT_UNROLL):
            inv_ref[dest_ref[base + u]] = tok0 + u // TOP_K
        return carry
    lax.fori_loop(0, n_ref[0] // INVERT_UNROLL, chunk, 0)


def _expert_body(li, n_blk, be_ref, nvb_ref, first_ref, wslot_ref, nxt_ref, dest_ref, fill_ref,
                 n_ref, x1t_hbm, wg_hbm, wu_hbm, wd_hbm, bg_ref, bu_ref, bd_ref, y_ref,
                 inv_s, wst, wg_s, wu_s, wd_s, rows0, rows1, rows2, sem, wsem):
    j = pl.program_id(0)
    nvb = nvb_ref[0]
    weights = (wg_hbm, wu_hbm, wd_hbm)
    bufs = (rows0, rows1, rows2)
    ahead = ROW_BUFS - 1

    def issue(blk, k):
        base = jnp.minimum(blk, n_blk - 1) * MOE_ROWS
        for r in range(MOE_ROWS):
            tok = inv_s[base + r]
            src = x1t_hbm.at[pl.ds(pl.multiple_of(tok * ROW_TILES, ROW_TILES), ROW_TILES)]
            pltpu.make_async_copy(src, bufs[k].at[pl.ds(r * ROW_TILES, ROW_TILES)],
                                  sem.at[k]).start()

    def wait_rows(k):
        pltpu.make_async_copy(bufs[k], bufs[k], sem.at[k]).wait()

    def weight_copies(e, slot):
        return [pltpu.make_async_copy(w.at[li, e], wst.at[slot, k], wsem.at[slot])
                for k, w in enumerate(weights)]

    @pl.when(j == 0)
    def _():
        for cp in weight_copies(be_ref[0], 0):
            cp.start()
        _build_row_order(dest_ref, fill_ref, n_ref, inv_s)
        for b in range(ahead):
            issue(b, b)

    @pl.when(first_ref[j] == 1)
    def _():
        slot = wslot_ref[j]
        for cp in weight_copies(0, slot):
            cp.wait()
        wg_s[...] = wst[slot, 0].astype(BF16)
        wu_s[...] = wst[slot, 1].astype(BF16)
        wd_s[...] = wst[slot, 2].astype(BF16)

        @pl.when(nxt_ref[j] >= 0)
        def _():
            for cp in weight_copies(nxt_ref[j], 1 - slot):
                cp.start()

    def compute(k):
        issue(j + ahead, (k + ahead) % ROW_BUFS)
        wait_rows(k)
        x = _from_row_tiles(bufs[k], MOE_ROWS).astype(BF16)
        g = jnp.dot(x, wg_s[...], preferred_element_type=F32) + bg_ref[...]
        up = jnp.dot(x, wu_s[...], preferred_element_type=F32) + bu_ref[...]
        g = jnp.minimum(g, SWIGLU_LIMIT)
        up = jnp.clip(up, -SWIGLU_LIMIT, SWIGLU_LIMIT)
        hdn = (up + 1.0) * (g * _sigmoid(SWIGLU_ALPHA * g))
        y = jnp.dot(hdn.astype(BF16), wd_s[...], preferred_element_type=F32) + bd_ref[...]
        _to_row_tiles(y_ref, y, MOE_ROWS)

    for k in range(ROW_BUFS):
        @pl.when((j < nvb) & (j % ROW_BUFS == k))
        def _(k=k):
            compute(k)

        @pl.when((j >= nvb) & (j < nvb + ahead) & (j % ROW_BUFS == k))
        def _(k=k):
            wait_rows(k)

    @pl.when(j >= nvb)
    def _():
        y_ref[...] = jnp.zeros_like(y_ref)


def _experts(x1t, tables, li, w_gate, b_gate, w_up, b_up, w_down, b_down):
    n_rows = tables['n_rows']
    n_blk = n_rows // MOE_ROWS
    d = D_MODEL
    blk = (MOE_ROWS * ROW_TILES, LANES)
    prefetch = (tables['blk_expert'], tables['n_used'], tables['first'], tables['wslot'],
                tables['next_expert'], tables['dest'], tables['fill'], tables['n_assign'])
    bspec = pl.BlockSpec((None, None, 1, d),
                         lambda j, be, *_: (li, be[jnp.minimum(j, n_blk - 1)], 0, 0))
    bias = lambda v: v.reshape(v.shape[0], v.shape[1], 1, v.shape[2])
    hbm = pl.BlockSpec(memory_space=pl.ANY)
    rows = pltpu.VMEM(blk, F32)
    return pl.pallas_call(
        functools.partial(_expert_body, li, n_blk),
        grid_spec=pltpu.PrefetchScalarGridSpec(
            num_scalar_prefetch=len(prefetch), grid=(n_blk + ROW_BUFS - 1,),
            in_specs=[hbm, hbm, hbm, hbm, bspec, bspec, bspec],
            out_specs=pl.BlockSpec(blk, lambda j, *_: (jnp.minimum(j, n_blk - 1), 0)),
            scratch_shapes=[pltpu.SMEM((n_rows,), I32), pltpu.VMEM((2, 3, d, D_FF), F32),
                            pltpu.VMEM((d, D_FF), BF16), pltpu.VMEM((d, D_FF), BF16),
                            pltpu.VMEM((D_FF, d), BF16)] + [rows] * ROW_BUFS
                           + [pltpu.SemaphoreType.DMA((ROW_BUFS,)), pltpu.SemaphoreType.DMA((2,))]),
        out_shape=jax.ShapeDtypeStruct((n_rows * ROW_TILES, LANES), F32),
        compiler_params=_cparams(("arbitrary",)),
        name="moe_experts",
    )(*prefetch, x1t, w_gate, w_up, w_down, bias(b_gate), bias(b_up), bias(b_down))


def _combine_body(dest_ref, yb_hbm, route_ref, x1_ref, p_ref, wple_ref, wpg_ref, bpg_ref,
                  g2_ref, b2_ref, g3_ref, b3_ref, o_ref, buf0, buf1, y_s, sem):
    i = pl.program_id(0)
    n = pl.num_programs(0)
    ts = x1_ref.shape[0]

    def issue(tile, buf, s):
        def tok(j, carry):
            a = (tile * ts + j) * TOP_K
            for kk in range(TOP_K):
                row = dest_ref[a + kk]
                src = yb_hbm.at[pl.ds(pl.multiple_of(row * ROW_TILES, ROW_TILES), ROW_TILES)]
                dst = buf.at[kk, pl.ds(pl.multiple_of(j * ROW_TILES, ROW_TILES), ROW_TILES)]
                pltpu.make_async_copy(src, dst, sem.at[s]).start()
            return carry
        lax.fori_loop(0, ts, tok, 0)

    @pl.when(i == 0)
    def _():
        issue(0, buf0, 0)

    route = route_ref[...]
    gates = [pltpu.bitcast(route[:, 2 * TOP_K + kk:2 * TOP_K + kk + 1], F32)
             for kk in range(TOP_K)]

    def phase(buf, s, other, so):
        @pl.when(i + 1 < n)
        def _():
            issue(i + 1, other, so)

        pltpu.make_async_copy(buf, buf, sem.at[s]).wait()
        for c in range(ROW_TILES):
            acc = jnp.zeros((ts, LANES), F32)
            for kk in range(TOP_K):
                acc = acc + gates[kk] * buf[kk, pl.ds(c, ts, stride=ROW_TILES), :]
            y_s[:, c * LANES:(c + 1) * LANES] = acc

    @pl.when(i % 2 == 0)
    def _():
        phase(buf0, 0, buf1, 1)

    @pl.when(i % 2 == 1)
    def _():
        phase(buf1, 1, buf0, 0)

    x2 = _layer_norm(ALPHA * x1_ref[...] + y_s[...], g2_ref[...], b2_ref[...])
    emb = jnp.dot(p_ref[...].astype(BF16), wple_ref[...], preferred_element_type=F32)
    gate_in = jnp.dot(x2.astype(BF16), wpg_ref[...], preferred_element_type=F32) + bpg_ref[...]
    ple = emb * _sigmoid(gate_in)
    o_ref[...] = _layer_norm(ALPHA * x2 + ple, g3_ref[...], b3_ref[...])


def _combine(dest_flat, yb, route, x1, p3d, li, w_ple, w_ple_gate, b_ple_gate, ln2_g, ln2_b,
             ln3_g, ln3_b):
    t, d = x1.shape
    ts = min(256, t)
    rows = lambda v: v.reshape(v.shape[0], 1, v.shape[-1])
    tile = lambda w: pl.BlockSpec((ts, w), lambda i, dst: (i, 0))
    buf = pltpu.VMEM((TOP_K, ts * ROW_TILES, LANES), F32)
    return pl.pallas_call(
        _combine_body,
        grid_spec=pltpu.PrefetchScalarGridSpec(
            num_scalar_prefetch=1, grid=(t // ts,),
            in_specs=[pl.BlockSpec(memory_space=pl.ANY), tile(LANES), tile(d),
                      pl.BlockSpec((None, ts, PLE_DIM), lambda i, dst: (li, i, 0)),
                      _layer_spec(li, (PLE_DIM, d)), _layer_spec(li, (d, d)),
                      _layer_spec(li, (1, d)), _layer_spec(li, (1, d)), _layer_spec(li, (1, d)),
                      _layer_spec(li, (1, d)), _layer_spec(li, (1, d))],
            out_specs=tile(d),
            scratch_shapes=[buf, buf, pltpu.VMEM((ts, d), F32), pltpu.SemaphoreType.DMA((2,))]),
        out_shape=jax.ShapeDtypeStruct((t, d), F32),
        compiler_params=_cparams(("arbitrary",)),
        name="moe_combine_ple",
    )(dest_flat, yb, route, x1, p3d, w_ple, w_ple_gate, rows(b_ple_gate), rows(ln2_g), rows(ln2_b),
      rows(ln3_g), rows(ln3_b))


def _routing_tables(route, counts, t):
    counts = counts.reshape(N_EXPERTS).astype(I32)
    padded = (counts + MOE_ROWS - 1) // MOE_ROWS * MOE_ROWS
    pend = jnp.cumsum(padded)
    pstart = pend - padded
    top_e = route[:, 0:TOP_K]
    rank = route[:, TOP_K:2 * TOP_K]
    onehot = top_e[:, :, None] == jnp.arange(N_EXPERTS, dtype=I32)
    dest = (jnp.sum(jnp.where(onehot, pstart, 0), axis=-1) + rank).reshape(t * TOP_K)
    n_blk = t * TOP_K // MOE_ROWS + N_EXPERTS
    n_used = pend[N_EXPERTS - 1] // MOE_ROWS
    fill = jnp.concatenate([jnp.stack([pstart + counts, pend], axis=1).reshape(2 * N_EXPERTS),
                            jnp.stack([n_used, jnp.asarray(n_blk, I32)])]).astype(I32)
    n_tab = n_blk + ROW_BUFS - 1
    blk = jnp.arange(n_tab, dtype=I32)
    blk_expert = jnp.minimum(jnp.sum((pend[None, :] <= (blk * MOE_ROWS)[:, None]).astype(I32),
                                     axis=1), N_EXPERTS - 1)
    first = (blk < n_used) & ((blk == 0) | (blk_expert != jnp.roll(blk_expert, 1)))
    wslot = (jnp.cumsum(first.astype(I32)) - 1) % 2
    first_at = jnp.where(first, blk, n_tab)
    next_first = jnp.concatenate([lax.cummin(first_at[::-1])[::-1][1:],
                                  jnp.full((1,), n_tab, I32)])
    next_expert = jnp.where(next_first < n_tab,
                            blk_expert[jnp.minimum(next_first, n_tab - 1)], -1)
    return dict(dest=dest, fill=fill, blk_expert=blk_expert, n_used=n_used.reshape(1),
                first=first.astype(I32), wslot=wslot.astype(I32),
                next_expert=next_expert.astype(I32), n_assign=jnp.full((1,), t * TOP_K, I32),
                n_rows=n_blk * MOE_ROWS)


def _layer(x2d, p3d, batch, seq, li, w):
    t = batch * seq
    zd = _proj_dense(x2d, w['w_in'], li)
    y_a = _lru_branch(zd, batch, seq, li, w['conv_w'], w['conv_b'], w['w_rg'], w['b_rg'],
                      w['w_ig'], w['b_ig'], w['lru_lambda'], w['w_rnn_out'])
    outs, lses = [], []
    for gi in range(N_GROUPS):
        o, lse = _attention_group(_proj_qkv(x2d, w['w_in'], li, batch, seq, gi), gi)
        outs.append(o)
        lses.append(lse)
    x1, x1t, route, counts = _merge(outs, lses, y_a, zd, x2d, seq, li, w['w_att_out'], w['w_out'],
                                    w['ln1_g'], w['ln1_b'], w['w_router'], w['b_router'])
    tables = _routing_tables(route, counts, t)
    yb = _experts(x1t, tables, li, w['w_gate'], w['b_gate'], w['w_up'], w['b_up'], w['w_down'],
                  w['b_down'])
    return _combine(tables['dest'], yb, route, x1, p3d, li, w['w_ple'], w['w_ple_gate'],
                    w['b_ple_gate'],
                    w['ln2_g'], w['ln2_b'], w['ln3_g'], w['ln3_b'])


_WEIGHT_NAMES = ('w_in', 'conv_w', 'conv_b', 'w_rg', 'b_rg', 'w_ig', 'b_ig', 'lru_lambda',
                 'w_rnn_out', 'w_att_out', 'w_out', 'ln1_g', 'ln1_b', 'w_router', 'b_router',
                 'w_gate', 'b_gate', 'w_up', 'b_up', 'w_down', 'b_down', 'ln2_g', 'ln2_b',
                 'w_ple', 'w_ple_gate', 'b_ple_gate', 'ln3_g', 'ln3_b')
_BF16_WEIGHTS = ('w_in', 'w_rg', 'w_ig', 'w_rnn_out', 'w_att_out', 'w_out', 'w_ple', 'w_ple_gate')


def kernel(x, p, w_in, conv_w, conv_b, w_rg, b_rg, w_ig, b_ig, lru_lambda, w_rnn_out, w_att_out,
           w_out, ln1_g, ln1_b, w_router, b_router, w_gate, b_gate, w_up, b_up, w_down, b_down,
           ln2_g, ln2_b, w_ple, w_ple_gate, b_ple_gate, ln3_g, ln3_b):
    w = dict(zip(_WEIGHT_NAMES, (
        w_in, conv_w, conv_b, w_rg, b_rg, w_ig, b_ig, lru_lambda, w_rnn_out, w_att_out, w_out,
        ln1_g, ln1_b, w_router, b_router, w_gate, b_gate, w_up, b_up, w_down, b_down, ln2_g,
        ln2_b, w_ple, w_ple_gate, b_ple_gate, ln3_g, ln3_b)))
    for name in _BF16_WEIGHTS:
        w[name] = w[name].astype(BF16)
    batch, seq, d = x.shape
    depth = p.shape[0]
    x2d = x.reshape(batch * seq, d)
    p3d = p.reshape(depth, batch * seq, PLE_DIM)
    for li in range(depth):
        x2d = _layer(x2d, p3d, batch, seq, li, w)
    return x2d.reshape(batch, seq, d)
```

```python
import functools

import jax
import jax.numpy as jnp
from jax import lax
from jax.experimental import pallas as pl
from jax.experimental.pallas import tpu as pltpu

F32 = jnp.float32
BF16 = jnp.bfloat16
I32 = jnp.int32

D_MODEL = 1024
DEPTH = 2
D_RNN = 1024
RNN_BLOCKS = 4
RNN_BW = D_RNN // RNN_BLOCKS
CONV_W = 4
LRU_C = 8.0
ATT_GROUPS = ((128, 1), (512, 4), (2048, 16))
N_GROUPS = len(ATT_GROUPS)
ATT_HEADS = 8
ATT_HEAD_DIM = 64
ATT_W = ATT_HEADS * ATT_HEAD_DIM
ATT_BLOCK = 128
OFF_XR = 0
OFF_YR = OFF_XR + D_RNN
OFF_Q = OFF_YR + D_RNN
OFF_K = OFF_Q + N_GROUPS * ATT_W
OFF_V = OFF_K + N_GROUPS * ATT_W
OFF_GA = OFF_V + N_GROUPS * ATT_W
OFF_GB = OFF_GA + D_MODEL
N_IN = OFF_GB + D_MODEL
N_EXPERTS = 32
TOP_K = 4
D_FF = D_MODEL
SWIGLU_ALPHA = 1.702
SWIGLU_LIMIT = 7.0
PLE_DIM = 256
ALPHA = (2.0 * DEPTH) ** 0.25
LN_EPS = 1e-5

LANES = 128
SUBLANES = 8
ROW_TILES = D_MODEL // LANES
COLB = ATT_W
N_COLB = N_IN // COLB
MOE_ROWS = 256
VMEM_LIMIT = 56 * 1024 * 1024
NEG_BIG = -1e30

assert ROW_TILES == SUBLANES


def _cparams(sem):
    return pltpu.CompilerParams(dimension_semantics=sem, vmem_limit_bytes=VMEM_LIMIT)


def _layer_spec(li, shape):
    return pl.BlockSpec((None,) + tuple(shape), lambda *_: (li,) + (0,) * len(shape))


def _layer_norm(v, g, b):
    mu = jnp.mean(v, axis=-1, keepdims=True)
    c = v - mu
    var = jnp.mean(c * c, axis=-1, keepdims=True)
    return c * lax.rsqrt(var + LN_EPS) * g + b


def _sigmoid(v):
    return 1.0 / (1.0 + jnp.exp(-v))


def _to_row_tiles(ref, m, rows):
    for c in range(ROW_TILES):
        ref[pl.ds(c, rows, stride=ROW_TILES), :] = m[:, c * LANES:(c + 1) * LANES]


def _from_row_tiles(ref, rows):
    return jnp.concatenate([ref[pl.ds(c, rows, stride=ROW_TILES), :] for c in range(ROW_TILES)],
                           axis=-1)


PROJ_ROWS = 512
DENSE_W = 2 * D_RNN + 2 * D_MODEL


def _proj_body(x_ref, w_ref, zd_ref, q0_ref, q1_ref, q2_ref, res_s):
    x = x_ref[...].astype(BF16)
    tm = x.shape[0]
    lo = 2 * D_RNN
    zd_ref[:, 0:lo] = jnp.dot(x, w_ref[:, 0:lo], preferred_element_type=F32).astype(zd_ref.dtype)
    zd_ref[:, lo:] = jnp.dot(x, w_ref[:, OFF_GA:N_IN],
                             preferred_element_type=F32).astype(zd_ref.dtype)
    for gi, o_ref in enumerate((q0_ref, q1_ref, q2_ref)):
        dil = ATT_GROUPS[gi][1]
        for c, off in enumerate((OFF_Q, OFF_K, OFF_V)):
            col = off + gi * ATT_W
            res = jnp.dot(x, w_ref[:, col:col + ATT_W], preferred_element_type=F32)
            cols = slice(c * ATT_W, (c + 1) * ATT_W)
            if dil == 1:
                o_ref[0, 0, :, cols] = res.astype(o_ref.dtype)
            else:
                for lc in range(ATT_W // LANES):
                    res_s[lc] = res[:, lc * LANES:(lc + 1) * LANES]
                for r in range(dil):
                    part = [res_s[lc, pl.ds(r, tm // dil, stride=dil), :]
                            for lc in range(ATT_W // LANES)]
                    o_ref[0, r, :, cols] = jnp.concatenate(part, axis=-1).astype(o_ref.dtype)


def _project(x2d, w_in, li, batch, seq):
    d = x2d.shape[1]
    tm = min(PROJ_ROWS, seq)
    ns = seq // tm
    dils = [dil for _, dil in ATT_GROUPS]
    return pl.pallas_call(
        _proj_body,
        grid=(batch, ns),
        in_specs=[pl.BlockSpec((tm, d), lambda b, s: (b * ns + s, 0)),
                  pl.BlockSpec((None, d, N_IN), lambda b, s: (li, 0, 0),
                               pipeline_mode=pl.Buffered(1))],
        out_specs=[pl.BlockSpec((tm, DENSE_W), lambda b, s: (b * ns + s, 0))]
                  + [pl.BlockSpec((1, dil, tm // dil, 3 * ATT_W), lambda b, s: (b, 0, s, 0))
                     for dil in dils],
        out_shape=[jax.ShapeDtypeStruct((batch * seq, DENSE_W), BF16)]
                  + [jax.ShapeDtypeStruct((batch, dil, seq // dil, 3 * ATT_W), BF16)
                     for dil in dils],
        scratch_shapes=[pltpu.VMEM((ATT_W // LANES, tm, LANES), F32)],
        compiler_params=_cparams(("parallel", "parallel")),
        name="in_proj",
    )(x2d, w_in)


def _lru_body(xr_ref, yr_ref, cw_ref, cb_ref, wrg_ref, brg_ref, wig_ref, big_ref, lam_ref,
              wout_ref, o_ref, tail_ref, h_ref):
    s = pl.program_id(1)

    @pl.when(s == 0)
    def _():
        tail_ref[...] = jnp.zeros_like(tail_ref)
        h_ref[...] = jnp.zeros_like(h_ref)

    xr = xr_ref[...].astype(F32)
    ts = xr.shape[0]
    xe = jnp.concatenate([tail_ref[...], xr], axis=0)
    cw = cw_ref[...]
    xc = (cb_ref[...] + cw[3:4] * xr + cw[2:3] * xe[7:7 + ts]
          + cw[1:2] * xe[6:6 + ts] + cw[0:1] * xe[5:5 + ts])
    tail_ref[...] = xr[ts - 8:]

    xcb = xc.astype(BF16)

    def gate(w_ref, b_ref):
        parts = [jnp.dot(xcb[:, n * RNN_BW:(n + 1) * RNN_BW], w_ref[n],
                         preferred_element_type=F32) for n in range(RNN_BLOCKS)]
        return _sigmoid(jnp.concatenate(parts, axis=-1) + b_ref[...])

    r = gate(wrg_ref, brg_ref)
    i = gate(wig_ref, big_ref)
    nlam = -lam_ref[...]
    softplus = jnp.maximum(nlam, 0.0) + jnp.log1p(jnp.exp(-jnp.abs(nlam)))
    log_a = (-LRU_C) * r * softplus
    a = jnp.exp(log_a)
    bx = jnp.sqrt(jnp.tanh(-log_a) * (1.0 + a * a)) * (i * xc)

    groups = ts // SUBLANES
    a3 = a.reshape(groups, SUBLANES, a.shape[-1])
    b3 = bx.reshape(groups, SUBLANES, a.shape[-1])
    sub = lax.broadcasted_iota(I32, (1, SUBLANES, 1), 1)
    k = 1
    while k < SUBLANES:
        keep = sub >= k
        a_prev = pltpu.roll(a3, k, 1)
        b_prev = pltpu.roll(b3, k, 1)
        b3 = jnp.where(keep, a3 * b_prev, 0.0) + b3
        a3 = jnp.where(keep, a3 * a_prev, a3)
        k *= 2
    carry = h_ref[0:1]
    hs = []
    for gidx in range(groups):
        hg = a3[gidx] * carry + b3[gidx]
        hs.append(hg)
        carry = hg[SUBLANES - 1:SUBLANES]
    h = jnp.concatenate(hs, axis=0)
    h_ref[0:1] = carry

    yr = yr_ref[...].astype(F32)
    gelu = 0.5 * yr * (1.0 + jnp.tanh(0.7978845608028654 * (yr + 0.044715 * (yr * yr * yr))))
    o_ref[...] = jnp.dot((gelu * h).astype(BF16), wout_ref[...],
                         preferred_element_type=F32).astype(o_ref.dtype)


def _lru_branch(z, batch, seq, li, conv_w, conv_b, w_rg, b_rg, w_ig, b_ig, lam, w_rnn_out):
    ts = min(256, seq)
    ns = seq // ts
    c = D_RNN
    rows = lambda v: v.reshape(v.shape[0], 1, c)
    return pl.pallas_call(
        _lru_body,
        grid=(batch, ns),
        in_specs=[pl.BlockSpec((ts, c), lambda b, s: (b * ns + s, OFF_XR // c)),
                  pl.BlockSpec((ts, c), lambda b, s: (b * ns + s, OFF_YR // c)),
                  _layer_spec(li, (CONV_W, c)), _layer_spec(li, (1, c)),
                  _layer_spec(li, (RNN_BLOCKS, RNN_BW, RNN_BW)), _layer_spec(li, (1, c)),
                  _layer_spec(li, (RNN_BLOCKS, RNN_BW, RNN_BW)), _layer_spec(li, (1, c)),
                  _layer_spec(li, (1, c)), _layer_spec(li, (c, D_MODEL))],
        out_specs=pl.BlockSpec((ts, D_MODEL), lambda b, s: (b * ns + s, 0)),
        out_shape=jax.ShapeDtypeStruct((batch * seq, D_MODEL), BF16),
        scratch_shapes=[pltpu.VMEM((8, c), F32), pltpu.VMEM((8, c), F32)],
        compiler_params=_cparams(("arbitrary", "arbitrary")),
        name="lru_branch",
    )(z, z, conv_w, rows(conv_b), w_rg, rows(b_rg), w_ig, rows(b_ig), rows(lam), w_rnn_out)


ATT_QROWS = 512


def _attn_body(n_back, q_ref, kp_ref, kc_ref, vp_ref, vc_ref, o_ref, lse_ref, s_s, p_s):
    n = pl.program_id(2)
    blk = ATT_BLOCK
    nsub = q_ref.shape[2] // blk
    qi = lax.broadcasted_iota(I32, (blk, 2 * blk), 0)
    kj = lax.broadcasted_iota(I32, (blk, 2 * blk), 1)
    diff = blk + qi - kj
    in_window = (diff >= 0) & (diff <= n_back)
    lane = lax.broadcasted_iota(I32, (blk, LANES), 1)
    scale = jnp.asarray(ATT_HEAD_DIM ** -0.5, q_ref.dtype)
    low_half = lane < ATT_HEAD_DIM
    ones = jnp.ones((2 * blk, LANES), BF16)
    for sb in range(nsub):
        rows = slice(sb * blk, (sb + 1) * blk)
        q = q_ref[0, 0, rows, :] * scale
        if sb == 0:
            k = jnp.concatenate([kp_ref[0, 0], kc_ref[0, 0, rows, :]], axis=0)
            v = jnp.concatenate([vp_ref[0, 0], vc_ref[0, 0, rows, :]], axis=0)
            valid = in_window & ((kj >= blk) | (n > 0))
        else:
            k = kc_ref[0, 0, (sb - 1) * blk:(sb + 1) * blk, :]
            v = vc_ref[0, 0, (sb - 1) * blk:(sb + 1) * blk, :]
            valid = in_window
        for h in range(ATT_HEADS):
            tile = slice((h // 2) * LANES, (h // 2 + 1) * LANES)
            mine = low_half if h % 2 == 0 else ~low_half
            qh = jnp.where(mine, q[:, tile], jnp.zeros_like(q[:, tile]))
            s_s[h] = lax.dot_general(qh, k[:, tile], (((1,), (1,)), ((), ())),
                                     preferred_element_type=F32)
        m_tile = jnp.zeros((blk, LANES), F32)
        for h in range(ATT_HEADS):
            s = jnp.where(valid, s_s[h], NEG_BIG)
            m = jnp.max(s, axis=-1, keepdims=True)
            p_s[h] = jnp.exp(s - m).astype(BF16)
            m_tile = jnp.where(lane == h, m, m_tile)
        lse_tile = jnp.zeros((blk, LANES), F32)
        for pair in range(ATT_HEADS // 2):
            tile = slice(pair * LANES, (pair + 1) * LANES)
            o_pair, l_pair = [], []
            for h in (2 * pair, 2 * pair + 1):
                o_pair.append(jnp.dot(p_s[h], v[:, tile], preferred_element_type=F32))
                l_pair.append(jnp.dot(p_s[h], ones, preferred_element_type=F32))
                lse_tile = jnp.where(lane == h, m_tile + jnp.log(l_pair[-1]), lse_tile)
            o_ref[0, 0, rows, tile] = jnp.where(low_half, o_pair[0] / l_pair[0],
                                                o_pair[1] / l_pair[1]).astype(o_ref.dtype)
        lse_ref[0, 0, rows, :] = lse_tile


def _attention_group(qkv, gi):
    batch, dil, l, _ = qkv.shape
    n_back = ATT_GROUPS[gi][0] // dil
    qrows = min(ATT_QROWS, l)
    nsub = qrows // ATT_BLOCK
    cur = lambda c: pl.BlockSpec((1, 1, qrows, ATT_W), lambda b, r, n: (b, r, n, c))
    prev = lambda c: pl.BlockSpec((1, 1, ATT_BLOCK, ATT_W),
                                  lambda b, r, n: (b, r, jnp.maximum(n * nsub - 1, 0), c))
    return pl.pallas_call(
        functools.partial(_attn_body, n_back),
        grid=(batch, dil, l // qrows),
        in_specs=[cur(0), prev(1), cur(1), prev(2), cur(2)],
        out_specs=[pl.BlockSpec((1, 1, qrows, ATT_W), lambda b, r, n: (b, r, n, 0)),
                   pl.BlockSpec((1, 1, qrows, LANES), lambda b, r, n: (b, r, n, 0))],
        out_shape=[jax.ShapeDtypeStruct((batch, dil, l, ATT_W), BF16),
                   jax.ShapeDtypeStruct((batch, dil, l, LANES), F32)],
        scratch_shapes=[pltpu.VMEM((ATT_HEADS, ATT_BLOCK, 2 * ATT_BLOCK), F32),
                        pltpu.VMEM((ATT_HEADS, ATT_BLOCK, 2 * ATT_BLOCK), BF16)],
        compiler_params=_cparams(("parallel", "parallel", "arbitrary")),
        name=f"attn_g{gi}",
    )(qkv, qkv, qkv, qkv, qkv)


def _merge_body(o0_ref, o1_ref, o2_ref, l0_ref, l1_ref, l2_ref, ya_ref, ga_ref, gb_ref,
                x_ref, watt_ref, wout_ref, g_ref, b_ref, wr_ref, br_ref,
                x1_ref, x1t_ref, route_ref, cnt_ref, carry_ref, o_s, l_s):
    i = pl.program_id(0)

    @pl.when(i == 0)
    def _():
        carry_ref[...] = jnp.zeros_like(carry_ref)

    ts = x_ref.shape[0]

    def token_order(ref, scratch):
        dil = ref.shape[1]
        if dil == 1:
            return ref[0, 0].astype(F32)
        nl = ref.shape[3] // LANES
        for r in range(dil):
            v = ref[0, r].astype(F32)
            for lc in range(nl):
                scratch[lc, pl.ds(r, ts // dil, stride=dil), :] = v[:, lc * LANES:(lc + 1) * LANES]
        return jnp.concatenate([scratch[lc] for lc in range(nl)], axis=-1)

    group_o = [token_order(ref, o_s.at[gi]) for gi, ref in enumerate((o0_ref, o1_ref, o2_ref))]
    lses = [token_order(ref, l_s.at[gi]) for gi, ref in enumerate((l0_ref, l1_ref, l2_ref))]
    mx = jnp.maximum(jnp.maximum(lses[0], lses[1]), lses[2])
    es = [jnp.exp(v - mx) for v in lses]
    den = es[0] + es[1] + es[2]
    er = lax.broadcasted_iota(I32, (LANES, ATT_W), 0)
    ec = lax.broadcasted_iota(I32, (LANES, ATT_W), 1)
    expand = jnp.where(ec // ATT_HEAD_DIM == er, 1.0, 0.0).astype(BF16)
    o = jnp.zeros((ts, ATT_W), F32)
    for e, og in zip(es, group_o):
        w = e / den
        w_hi = w.astype(BF16)
        w_lo = (w - w_hi.astype(F32)).astype(BF16)
        wx = (jnp.dot(w_hi, expand, preferred_element_type=F32)
              + jnp.dot(w_lo, expand, preferred_element_type=F32))
        o = o + wx * og
    y_b = jnp.dot(o.astype(BF16), watt_ref[...], preferred_element_type=F32)
    ga = ga_ref[...].astype(F32)
    gb = gb_ref[...].astype(F32)
    merged = _sigmoid(ga) * ya_ref[...].astype(F32) + _sigmoid(gb) * y_b
    hmix = jnp.dot(merged.astype(BF16), wout_ref[...], preferred_element_type=F32)
    x1 = _layer_norm(ALPHA * x_ref[...] + hmix, g_ref[...], b_ref[...])
    x1_ref[...] = x1
    _to_row_tiles(x1t_ref, x1, ts)

    wr = wr_ref[...]
    wr_hi = wr.astype(BF16)
    wr_lo = (wr - wr_hi.astype(F32)).astype(BF16)
    x_hi = x1.astype(BF16)
    x_lo = (x1 - x_hi.astype(F32)).astype(BF16)
    logits = (jnp.dot(x_hi, wr_hi, preferred_element_type=F32)
              + jnp.dot(x_lo, wr_hi, preferred_element_type=F32)
              + jnp.dot(x_hi, wr_lo, preferred_element_type=F32)) + br_ref[...]

    el = lax.broadcasted_iota(I32, (ts, N_EXPERTS), 1)
    work = logits
    vals, idxs, hots = [], [], []
    for _ in range(TOP_K):
        m = jnp.max(work, axis=-1, keepdims=True)
        idx = jnp.min(jnp.where(work == m, el, N_EXPERTS), axis=-1, keepdims=True)
        hot = el == idx
        vals.append(m)
        idxs.append(idx)
        hots.append(hot)
        work = jnp.where(hot, NEG_BIG, work)
    exps = [jnp.exp(v - vals[0]) for v in vals]
    gden = exps[0] + exps[1] + exps[2] + exps[3]

    cnt = jnp.zeros((ts, N_EXPERTS), F32)
    for hot in hots:
        cnt = cnt + jnp.where(hot, 1.0, 0.0)
    tr = lax.broadcasted_iota(I32, (ts, ts), 0)
    tc = lax.broadcasted_iota(I32, (ts, ts), 1)
    tri = jnp.where(tc < tr, 1.0, 0.0).astype(BF16)
    before = jnp.dot(tri, cnt.astype(BF16), preferred_element_type=F32) + carry_ref[...]
    carry_ref[...] = carry_ref[...] + jnp.sum(cnt, axis=0, keepdims=True)
    cnt_ref[...] = carry_ref[...]

    lane = lax.broadcasted_iota(I32, (ts, LANES), 1)
    route = jnp.zeros((ts, LANES), I32)
    for kk in range(TOP_K):
        rank = jnp.sum(jnp.where(hots[kk], before, 0.0), axis=-1, keepdims=True).astype(I32)
        gate_bits = pltpu.bitcast(exps[kk] / gden, I32)
        route = jnp.where(lane == kk, idxs[kk], route)
        route = jnp.where(lane == TOP_K + kk, rank, route)
        route = jnp.where(lane == 2 * TOP_K + kk, gate_bits, route)
    route_ref[...] = route


def _merge(outs, lses, y_a, zd, x2d, seq, li, w_att_out, w_out, ln_g, ln_b, w_router, b_router):
    t = x2d.shape[0]
    ts = min(256, seq)
    ns = seq // ts
    rows = lambda v: v.reshape(v.shape[0], 1, v.shape[-1])
    tile = lambda w, c=0: pl.BlockSpec((ts, w), lambda i: (i, c))

    def by_residue(a):
        dil, w = a.shape[1], a.shape[3]
        return pl.BlockSpec((1, dil, ts // dil, w), lambda i: (i // ns, 0, i % ns, 0))

    return pl.pallas_call(
        _merge_body,
        grid=(t // ts,),
        in_specs=[by_residue(a) for a in (*outs, *lses)]
                 + [tile(D_MODEL), tile(D_MODEL, 2 * D_RNN // D_MODEL),
                    tile(D_MODEL, 2 * D_RNN // D_MODEL + 1), tile(D_MODEL),
                    _layer_spec(li, (ATT_W, D_MODEL)), _layer_spec(li, (D_MODEL, D_MODEL)),
                    _layer_spec(li, (1, D_MODEL)), _layer_spec(li, (1, D_MODEL)),
                    _layer_spec(li, (D_MODEL, N_EXPERTS)), _layer_spec(li, (1, N_EXPERTS))],
        out_specs=[tile(D_MODEL), pl.BlockSpec((ts * ROW_TILES, LANES), lambda i: (i, 0)),
                   tile(LANES), pl.BlockSpec((1, N_EXPERTS), lambda i: (0, 0))],
        out_shape=[jax.ShapeDtypeStruct((t, D_MODEL), F32),
                   jax.ShapeDtypeStruct((t * ROW_TILES, LANES), F32),
                   jax.ShapeDtypeStruct((t, LANES), I32),
                   jax.ShapeDtypeStruct((1, N_EXPERTS), F32)],
        scratch_shapes=[pltpu.VMEM((1, N_EXPERTS), F32),
                        pltpu.VMEM((N_GROUPS, ATT_W // LANES, ts, LANES), F32),
                        pltpu.VMEM((N_GROUPS, 1, ts, LANES), F32)],
        compiler_params=_cparams(("arbitrary",)),
        name="merge_ln1_router",
    )(*outs, *lses, y_a, zd, zd, x2d, w_att_out, w_out, rows(ln_g), rows(ln_b), w_router,
      rows(b_router))


INVERT_UNROLL = 32
ROW_BUFS = 3


def _build_row_order(dest_ref, fill_ref, n_ref, inv_ref):
    def zero_range(lo, hi):
        def z(row, carry):
            inv_ref[row] = 0
            return carry
        lax.fori_loop(lo, hi, z, 0)

    def per_expert(e, carry):
        zero_range(fill_ref[2 * e], fill_ref[2 * e + 1])
        return carry
    lax.fori_loop(0, N_EXPERTS, per_expert, 0)
    zero_range(fill_ref[2 * N_EXPERTS] * MOE_ROWS, fill_ref[2 * N_EXPERTS + 1] * MOE_ROWS)

    def chunk(c, carry):
        base = c * INVERT_UNROLL
        tok0 = c * (INVERT_UNROLL // TOP_K)
        for u in range(INVERT_UNROLL):
            inv_ref[dest_ref[base + u]] = tok0 + u // TOP_K
        return carry
    lax.fori_loop(0, n_ref[0] // INVERT_UNROLL, chunk, 0)


def _expert_body(li, n_blk, be_ref, nvb_ref, first_ref, wslot_ref, nxt_ref, dest_ref, fill_ref,
                 n_ref, x1t_hbm, wg_hbm, wu_hbm, wd_hbm, bg_ref, bu_ref, bd_ref, y_ref,
                 inv_s, wst, wg_s, wu_s, wd_s, rows0, rows1, rows2, sem, wsem):
    j = pl.program_id(0)
    nvb = nvb_ref[0]
    weights = (wg_hbm, wu_hbm, wd_hbm)
    bufs = (rows0, rows1, rows2)
    ahead = ROW_BUFS - 1

    def issue(blk, k):
        base = jnp.minimum(blk, n_blk - 1) * MOE_ROWS
        for r in range(MOE_ROWS):
            tok = inv_s[base + r]
            src = x1t_hbm.at[pl.ds(pl.multiple_of(tok * ROW_TILES, ROW_TILES), ROW_TILES)]
            pltpu.make_async_copy(src, bufs[k].at[pl.ds(r * ROW_TILES, ROW_TILES)],
                                  sem.at[k]).start()

    def wait_rows(k):
        pltpu.make_async_copy(bufs[k], bufs[k], sem.at[k]).wait()

    def weight_copies(e, slot):
        return [pltpu.make_async_copy(w.at[li, e], wst.at[slot, k], wsem.at[slot])
                for k, w in enumerate(weights)]

    @pl.when(j == 0)
    def _():
        for cp in weight_copies(be_ref[0], 0):
            cp.start()
        _build_row_order(dest_ref, fill_ref, n_ref, inv_s)
        for b in range(ahead):
            issue(b, b)

    @pl.when(first_ref[j] == 1)
    def _():
        slot = wslot_ref[j]
        for cp in weight_copies(0, slot):
            cp.wait()
        wg_s[...] = wst[slot, 0].astype(BF16)
        wu_s[...] = wst[slot, 1].astype(BF16)
        wd_s[...] = wst[slot, 2].astype(BF16)

        @pl.when(nxt_ref[j] >= 0)
        def _():
            for cp in weight_copies(nxt_ref[j], 1 - slot):
                cp.start()

    def compute(k):
        issue(j + ahead, (k + ahead) % ROW_BUFS)
        wait_rows(k)
        x = _from_row_tiles(bufs[k], MOE_ROWS).astype(BF16)
        g = jnp.dot(x, wg_s[...], preferred_element_type=F32) + bg_ref[...]
        up = jnp.dot(x, wu_s[...], preferred_element_type=F32) + bu_ref[...]
        g = jnp.minimum(g, SWIGLU_LIMIT)
        up = jnp.clip(up, -SWIGLU_LIMIT, SWIGLU_LIMIT)
        hdn = (up + 1.0) * (g * _sigmoid(SWIGLU_ALPHA * g))
        y = jnp.dot(hdn.astype(BF16), wd_s[...], preferred_element_type=F32) + bd_ref[...]
        _to_row_tiles(y_ref, y, MOE_ROWS)

    for k in range(ROW_BUFS):
        @pl.when((j < nvb) & (j % ROW_BUFS == k))
        def _(k=k):
            compute(k)

        @pl.when((j >= nvb) & (j < nvb + ahead) & (j % ROW_BUFS == k))
        def _(k=k):
            wait_rows(k)

    @pl.when(j >= nvb)
    def _():
        y_ref[...] = jnp.zeros_like(y_ref)


def _experts(x1t, tables, li, w_gate, b_gate, w_up, b_up, w_down, b_down):
    n_rows = tables['n_rows']
    n_blk = n_rows // MOE_ROWS
    d = D_MODEL
    blk = (MOE_ROWS * ROW_TILES, LANES)
    prefetch = (tables['blk_expert'], tables['n_used'], tables['first'], tables['wslot'],
                tables['next_expert'], tables['dest'], tables['fill'], tables['n_assign'])
    bspec = pl.BlockSpec((None, None, 1, d),
                         lambda j, be, *_: (li, be[jnp.minimum(j, n_blk - 1)], 0, 0))
    bias = lambda v: v.reshape(v.shape[0], v.shape[1], 1, v.shape[2])
    hbm = pl.BlockSpec(memory_space=pl.ANY)
    rows = pltpu.VMEM(blk, F32)
    return pl.pallas_call(
        functools.partial(_expert_body, li, n_blk),
        grid_spec=pltpu.PrefetchScalarGridSpec(
            num_scalar_prefetch=len(prefetch), grid=(n_blk + ROW_BUFS - 1,),
            in_specs=[hbm, hbm, hbm, hbm, bspec, bspec, bspec],
            out_specs=pl.BlockSpec(blk, lambda j, *_: (jnp.minimum(j, n_blk - 1), 0)),
            scratch_shapes=[pltpu.SMEM((n_rows,), I32), pltpu.VMEM((2, 3, d, D_FF), F32),
                            pltpu.VMEM((d, D_FF), BF16), pltpu.VMEM((d, D_FF), BF16),
                            pltpu.VMEM((D_FF, d), BF16)] + [rows] * ROW_BUFS
                           + [pltpu.SemaphoreType.DMA((ROW_BUFS,)), pltpu.SemaphoreType.DMA((2,))]),
        out_shape=jax.ShapeDtypeStruct((n_rows * ROW_TILES, LANES), F32),
        compiler_params=_cparams(("arbitrary",)),
        name="moe_experts",
    )(*prefetch, x1t, w_gate, w_up, w_down, bias(b_gate), bias(b_up), bias(b_down))


COMBINE_BUFS = 3


def _combine_body(n_tiles, dest_ref, yb_hbm, route_ref, x1_ref, p_ref, wple_ref, wpg_ref, bpg_ref,
                  g2_ref, b2_ref, g3_ref, b3_ref, o_ref, buf0, buf1, buf2, sem):
    i = pl.program_id(0)
    ts = x1_ref.shape[0]
    bufs = (buf0, buf1, buf2)
    ahead = COMBINE_BUFS - 1

    def issue(tile, k):
        base = jnp.minimum(tile, n_tiles - 1) * (ts * TOP_K)
        for j in range(ts):
            for kk in range(TOP_K):
                row = dest_ref[base + j * TOP_K + kk]
                src = yb_hbm.at[pl.ds(pl.multiple_of(row * ROW_TILES, ROW_TILES), ROW_TILES)]
                pltpu.make_async_copy(src, bufs[k].at[kk, pl.ds(j * ROW_TILES, ROW_TILES)],
                                      sem.at[k]).start()

    def wait_rows(k):
        pltpu.make_async_copy(bufs[k], bufs[k], sem.at[k]).wait()

    @pl.when(i == 0)
    def _():
        for b in range(ahead):
            issue(b, b)

    def compute(k):
        issue(i + ahead, (k + ahead) % COMBINE_BUFS)
        wait_rows(k)
        route = route_ref[...]
        gates = [pltpu.bitcast(route[:, 2 * TOP_K + kk:2 * TOP_K + kk + 1], F32)
                 for kk in range(TOP_K)]
        parts = []
        for c in range(ROW_TILES):
            acc = jnp.zeros((ts, LANES), F32)
            for kk in range(TOP_K):
                acc = acc + gates[kk] * bufs[k][kk, pl.ds(c, ts, stride=ROW_TILES), :]
            parts.append(acc)
        y = jnp.concatenate(parts, axis=-1)
        x2 = _layer_norm(ALPHA * x1_ref[...] + y, g2_ref[...], b2_ref[...])
        emb = jnp.dot(p_ref[...].astype(BF16), wple_ref[...], preferred_element_type=F32)
        gate_in = jnp.dot(x2.astype(BF16), wpg_ref[...], preferred_element_type=F32) + bpg_ref[...]
        ple = emb * _sigmoid(gate_in)
        o_ref[...] = _layer_norm(ALPHA * x2 + ple, g3_ref[...], b3_ref[...])

    for k in range(COMBINE_BUFS):
        @pl.when((i < n_tiles) & (i % COMBINE_BUFS == k))
        def _(k=k):
            compute(k)

        @pl.when((i >= n_tiles) & (i % COMBINE_BUFS == k))
        def _(k=k):
            wait_rows(k)


def _combine(dest_flat, yb, route, x1, p3d, li, w_ple, w_ple_gate, b_ple_gate, ln2_g, ln2_b,
             ln3_g, ln3_b):
    t, d = x1.shape
    ts = min(256, t)
    n = t // ts
    rows = lambda v: v.reshape(v.shape[0], 1, v.shape[-1])
    tile = lambda w: pl.BlockSpec((ts, w), lambda i, dst: (jnp.minimum(i, n - 1), 0))
    buf = pltpu.VMEM((TOP_K, ts * ROW_TILES, LANES), F32)
    return pl.pallas_call(
        functools.partial(_combine_body, n),
        grid_spec=pltpu.PrefetchScalarGridSpec(
            num_scalar_prefetch=1, grid=(n + COMBINE_BUFS - 1,),
            in_specs=[pl.BlockSpec(memory_space=pl.ANY), tile(LANES), tile(d),
                      pl.BlockSpec((None, ts, PLE_DIM),
                                   lambda i, dst: (li, jnp.minimum(i, n - 1), 0)),
                      _layer_spec(li, (PLE_DIM, d)), _layer_spec(li, (d, d)),
                      _layer_spec(li, (1, d)), _layer_spec(li, (1, d)), _layer_spec(li, (1, d)),
                      _layer_spec(li, (1, d)), _layer_spec(li, (1, d))],
            out_specs=tile(d),
            scratch_shapes=[buf] * COMBINE_BUFS + [pltpu.SemaphoreType.DMA((COMBINE_BUFS,))]),
        out_shape=jax.ShapeDtypeStruct((t, d), F32),
        compiler_params=_cparams(("arbitrary",)),
        name="moe_combine_ple",
    )(dest_flat, yb, route, x1, p3d, w_ple, w_ple_gate, rows(b_ple_gate), rows(ln2_g), rows(ln2_b),
      rows(ln3_g), rows(ln3_b))


def _routing_tables(route, counts, t):
    counts = counts.reshape(N_EXPERTS).astype(I32)
    padded = (counts + MOE_ROWS - 1) // MOE_ROWS * MOE_ROWS
    pend = jnp.cumsum(padded)
    pstart = pend - padded
    top_e = route[:, 0:TOP_K]
    rank = route[:, TOP_K:2 * TOP_K]
    onehot = top_e[:, :, None] == jnp.arange(N_EXPERTS, dtype=I32)
    dest = (jnp.sum(jnp.where(onehot, pstart, 0), axis=-1) + rank).reshape(t * TOP_K)
    n_blk = t * TOP_K // MOE_ROWS + N_EXPERTS
    n_used = pend[N_EXPERTS - 1] // MOE_ROWS
    fill = jnp.concatenate([jnp.stack([pstart + counts, pend], axis=1).reshape(2 * N_EXPERTS),
                            jnp.stack([n_used, jnp.asarray(n_blk, I32)])]).astype(I32)
    n_tab = n_blk + ROW_BUFS - 1
    blk = jnp.arange(n_tab, dtype=I32)
    blk_expert = jnp.minimum(jnp.sum((pend[None, :] <= (blk * MOE_ROWS)[:, None]).astype(I32),
                                     axis=1), N_EXPERTS - 1)
    first = (blk < n_used) & ((blk == 0) | (blk_expert != jnp.roll(blk_expert, 1)))
    wslot = (jnp.cumsum(first.astype(I32)) - 1) % 2
    first_at = jnp.where(first, blk, n_tab)
    next_first = jnp.concatenate([lax.cummin(first_at[::-1])[::-1][1:],
                                  jnp.full((1,), n_tab, I32)])
    next_expert = jnp.where(next_first < n_tab,
                            blk_expert[jnp.minimum(next_first, n_tab - 1)], -1)
    return dict(dest=dest, fill=fill, blk_expert=blk_expert, n_used=n_used.reshape(1),
                first=first.astype(I32), wslot=wslot.astype(I32),
                next_expert=next_expert.astype(I32), n_assign=jnp.full((1,), t * TOP_K, I32),
                n_rows=n_blk * MOE_ROWS)


def _layer(x2d, p3d, batch, seq, li, w):
    t = batch * seq
    zd, *qkvs = _project(x2d, w['w_in'], li, batch, seq)
    y_a = _lru_branch(zd, batch, seq, li, w['conv_w'], w['conv_b'], w['w_rg'], w['b_rg'],
                      w['w_ig'], w['b_ig'], w['lru_lambda'], w['w_rnn_out'])
    outs, lses = [], []
    for gi in range(N_GROUPS):
        o, lse = _attention_group(qkvs[gi], gi)
        outs.append(o)
        lses.append(lse)
    x1, x1t, route, counts = _merge(outs, lses, y_a, zd, x2d, seq, li, w['w_att_out'], w['w_out'],
                                    w['ln1_g'], w['ln1_b'], w['w_router'], w['b_router'])
    tables = _routing_tables(route, counts, t)
    yb = _experts(x1t, tables, li, w['w_gate'], w['b_gate'], w['w_up'], w['b_up'], w['w_down'],
                  w['b_down'])
    return _combine(tables['dest'], yb, route, x1, p3d, li, w['w_ple'], w['w_ple_gate'],
                    w['b_ple_gate'],
                    w['ln2_g'], w['ln2_b'], w['ln3_g'], w['ln3_b'])


_WEIGHT_NAMES = ('w_in', 'conv_w', 'conv_b', 'w_rg', 'b_rg', 'w_ig', 'b_ig', 'lru_lambda',
                 'w_rnn_out', 'w_att_out', 'w_out', 'ln1_g', 'ln1_b', 'w_router', 'b_router',
                 'w_gate', 'b_gate', 'w_up', 'b_up', 'w_down', 'b_down', 'ln2_g', 'ln2_b',
                 'w_ple', 'w_ple_gate', 'b_ple_gate', 'ln3_g', 'ln3_b')
_BF16_WEIGHTS = ('w_in', 'w_rg', 'w_ig', 'w_rnn_out', 'w_att_out', 'w_out', 'w_ple', 'w_ple_gate')


def kernel(x, p, w_in, conv_w, conv_b, w_rg, b_rg, w_ig, b_ig, lru_lambda, w_rnn_out, w_att_out,
           w_out, ln1_g, ln1_b, w_router, b_router, w_gate, b_gate, w_up, b_up, w_down, b_down,
           ln2_g, ln2_b, w_ple, w_ple_gate, b_ple_gate, ln3_g, ln3_b):
    w = dict(zip(_WEIGHT_NAMES, (
        w_in, conv_w, conv_b, w_rg, b_rg, w_ig, b_ig, lru_lambda, w_rnn_out, w_att_out, w_out,
        ln1_g, ln1_b, w_router, b_router, w_gate, b_gate, w_up, b_up, w_down, b_down, ln2_g,
        ln2_b, w_ple, w_ple_gate, b_ple_gate, ln3_g, ln3_b)))
    for name in _BF16_WEIGHTS:
        w[name] = w[name].astype(BF16)
    batch, seq, d = x.shape
    depth = p.shape[0]
    x2d = x.reshape(batch * seq, d)
    p3d = p.reshape(depth, batch * seq, PLE_DIM)
    for li in range(depth):
        x2d = _layer(x2d, p3d, batch, seq, li, w)
    return x2d.reshape(batch, seq, d)
```

```python
import functools

import jax
import jax.numpy as jnp
from jax import lax
from jax.experimental import pallas as pl
from jax.experimental.pallas import tpu as pltpu

F32 = jnp.float32
BF16 = jnp.bfloat16
I32 = jnp.int32

D_MODEL = 1024
DEPTH = 2
D_RNN = 1024
RNN_BLOCKS = 4
RNN_BW = D_RNN // RNN_BLOCKS
CONV_W = 4
LRU_C = 8.0
ATT_GROUPS = ((128, 1), (512, 4), (2048, 16))
N_GROUPS = len(ATT_GROUPS)
ATT_HEADS = 8
ATT_HEAD_DIM = 64
ATT_W = ATT_HEADS * ATT_HEAD_DIM
ATT_BLOCK = 128
OFF_XR = 0
OFF_YR = OFF_XR + D_RNN
OFF_Q = OFF_YR + D_RNN
OFF_K = OFF_Q + N_GROUPS * ATT_W
OFF_V = OFF_K + N_GROUPS * ATT_W
OFF_GA = OFF_V + N_GROUPS * ATT_W
OFF_GB = OFF_GA + D_MODEL
N_IN = OFF_GB + D_MODEL
N_EXPERTS = 32
TOP_K = 4
D_FF = D_MODEL
SWIGLU_ALPHA = 1.702
SWIGLU_LIMIT = 7.0
PLE_DIM = 256
ALPHA = (2.0 * DEPTH) ** 0.25
LN_EPS = 1e-5

LANES = 128
SUBLANES = 8
ROW_TILES = D_MODEL // LANES
COLB = ATT_W
N_COLB = N_IN // COLB
MOE_ROWS = 256
VMEM_LIMIT = 56 * 1024 * 1024
NEG_BIG = -1e30

assert ROW_TILES == SUBLANES


def _cparams(sem):
    return pltpu.CompilerParams(dimension_semantics=sem, vmem_limit_bytes=VMEM_LIMIT)


def _layer_spec(li, shape):
    return pl.BlockSpec((None,) + tuple(shape), lambda *_: (li,) + (0,) * len(shape))


def _layer_norm(v, g, b):
    mu = jnp.mean(v, axis=-1, keepdims=True)
    c = v - mu
    var = jnp.mean(c * c, axis=-1, keepdims=True)
    return c * lax.rsqrt(var + LN_EPS) * g + b


def _sigmoid(v):
    return 0.5 * jnp.tanh(0.5 * v) + 0.5


def _to_row_tiles(ref, m, rows):
    for c in range(ROW_TILES):
        ref[pl.ds(c, rows, stride=ROW_TILES), :] = m[:, c * LANES:(c + 1) * LANES]


def _from_row_tiles(ref, rows):
    return jnp.concatenate([ref[pl.ds(c, rows, stride=ROW_TILES), :] for c in range(ROW_TILES)],
                           axis=-1)


PROJ_ROWS = 512
DENSE_W = 2 * D_RNN + 2 * D_MODEL


def _proj_body(x_ref, w_ref, zd_ref, q0_ref, q1_ref, q2_ref, res_s):
    x = x_ref[...].astype(BF16)
    tm = x.shape[0]
    lo = 2 * D_RNN
    zd_ref[:, 0:lo] = jnp.dot(x, w_ref[:, 0:lo], preferred_element_type=F32).astype(zd_ref.dtype)
    zd_ref[:, lo:] = jnp.dot(x, w_ref[:, OFF_GA:N_IN],
                             preferred_element_type=F32).astype(zd_ref.dtype)
    for gi, o_ref in enumerate((q0_ref, q1_ref, q2_ref)):
        dil = ATT_GROUPS[gi][1]
        for c, off in enumerate((OFF_Q, OFF_K, OFF_V)):
            col = off + gi * ATT_W
            res = jnp.dot(x, w_ref[:, col:col + ATT_W], preferred_element_type=F32)
            cols = slice(c * ATT_W, (c + 1) * ATT_W)
            if dil == 1:
                o_ref[0, 0, :, cols] = res.astype(o_ref.dtype)
            else:
                for lc in range(ATT_W // LANES):
                    res_s[lc] = res[:, lc * LANES:(lc + 1) * LANES]
                for r in range(dil):
                    part = [res_s[lc, pl.ds(r, tm // dil, stride=dil), :]
                            for lc in range(ATT_W // LANES)]
                    o_ref[0, r, :, cols] = jnp.concatenate(part, axis=-1).astype(o_ref.dtype)


def _project(x2d, w_in, li, batch, seq):
    d = x2d.shape[1]
    tm = min(PROJ_ROWS, seq)
    ns = seq // tm
    dils = [dil for _, dil in ATT_GROUPS]
    return pl.pallas_call(
        _proj_body,
        grid=(batch, ns),
        in_specs=[pl.BlockSpec((tm, d), lambda b, s: (b * ns + s, 0)),
                  pl.BlockSpec((None, d, N_IN), lambda b, s: (li, 0, 0),
                               pipeline_mode=pl.Buffered(1))],
        out_specs=[pl.BlockSpec((tm, DENSE_W), lambda b, s: (b * ns + s, 0))]
                  + [pl.BlockSpec((1, dil, tm // dil, 3 * ATT_W), lambda b, s: (b, 0, s, 0))
                     for dil in dils],
        out_shape=[jax.ShapeDtypeStruct((batch * seq, DENSE_W), BF16)]
                  + [jax.ShapeDtypeStruct((batch, dil, seq // dil, 3 * ATT_W), BF16)
                     for dil in dils],
        scratch_shapes=[pltpu.VMEM((ATT_W // LANES, tm, LANES), F32)],
        compiler_params=_cparams(("parallel", "parallel")),
        name="in_proj",
    )(x2d, w_in)


def _lru_body(xr_ref, yr_ref, cw_ref, cb_ref, wrg_ref, brg_ref, wig_ref, big_ref, lam_ref,
              wout_ref, o_ref, tail_ref, h_ref):
    s = pl.program_id(1)

    @pl.when(s == 0)
    def _():
        tail_ref[...] = jnp.zeros_like(tail_ref)
        h_ref[...] = jnp.zeros_like(h_ref)

    xr = xr_ref[...].astype(F32)
    ts = xr.shape[0]
    xe = jnp.concatenate([tail_ref[...], xr], axis=0)
    cw = cw_ref[...]
    xc = (cb_ref[...] + cw[3:4] * xr + cw[2:3] * xe[7:7 + ts]
          + cw[1:2] * xe[6:6 + ts] + cw[0:1] * xe[5:5 + ts])
    tail_ref[...] = xr[ts - 8:]

    xcb = xc.astype(BF16)

    def gate(w_ref, b_ref):
        parts = [jnp.dot(xcb[:, n * RNN_BW:(n + 1) * RNN_BW], w_ref[n],
                         preferred_element_type=F32) for n in range(RNN_BLOCKS)]
        return _sigmoid(jnp.concatenate(parts, axis=-1) + b_ref[...])

    r = gate(wrg_ref, brg_ref)
    i = gate(wig_ref, big_ref)
    nlam = -lam_ref[...]
    softplus = jnp.maximum(nlam, 0.0) + jnp.log1p(jnp.exp(-jnp.abs(nlam)))
    log_a = (-LRU_C) * r * softplus
    a = jnp.exp(log_a)
    bx = jnp.sqrt(jnp.tanh(-log_a) * (1.0 + a * a)) * (i * xc)

    groups = ts // SUBLANES
    a3 = a.reshape(groups, SUBLANES, a.shape[-1])
    b3 = bx.reshape(groups, SUBLANES, a.shape[-1])
    sub = lax.broadcasted_iota(I32, (1, SUBLANES, 1), 1)
    k = 1
    while k < SUBLANES:
        keep = sub >= k
        a_prev = pltpu.roll(a3, k, 1)
        b_prev = pltpu.roll(b3, k, 1)
        b3 = jnp.where(keep, a3 * b_prev, 0.0) + b3
        a3 = jnp.where(keep, a3 * a_prev, a3)
        k *= 2
    carry = h_ref[0:1]
    hs = []
    for gidx in range(groups):
        hg = a3[gidx] * carry + b3[gidx]
        hs.append(hg)
        carry = hg[SUBLANES - 1:SUBLANES]
    h = jnp.concatenate(hs, axis=0)
    h_ref[0:1] = carry

    yr = yr_ref[...].astype(F32)
    gelu = 0.5 * yr * (1.0 + jnp.tanh(0.7978845608028654 * (yr + 0.044715 * (yr * yr * yr))))
    o_ref[...] = jnp.dot((gelu * h).astype(BF16), wout_ref[...],
                         preferred_element_type=F32).astype(o_ref.dtype)


def _lru_branch(z, batch, seq, li, conv_w, conv_b, w_rg, b_rg, w_ig, b_ig, lam, w_rnn_out):
    ts = min(256, seq)
    ns = seq // ts
    c = D_RNN
    rows = lambda v: v.reshape(v.shape[0], 1, c)
    return pl.pallas_call(
        _lru_body,
        grid=(batch, ns),
        in_specs=[pl.BlockSpec((ts, c), lambda b, s: (b * ns + s, OFF_XR // c)),
                  pl.BlockSpec((ts, c), lambda b, s: (b * ns + s, OFF_YR // c)),
                  _layer_spec(li, (CONV_W, c)), _layer_spec(li, (1, c)),
                  _layer_spec(li, (RNN_BLOCKS, RNN_BW, RNN_BW)), _layer_spec(li, (1, c)),
                  _layer_spec(li, (RNN_BLOCKS, RNN_BW, RNN_BW)), _layer_spec(li, (1, c)),
                  _layer_spec(li, (1, c)), _layer_spec(li, (c, D_MODEL))],
        out_specs=pl.BlockSpec((ts, D_MODEL), lambda b, s: (b * ns + s, 0)),
        out_shape=jax.ShapeDtypeStruct((batch * seq, D_MODEL), BF16),
        scratch_shapes=[pltpu.VMEM((8, c), F32), pltpu.VMEM((8, c), F32)],
        compiler_params=_cparams(("arbitrary", "arbitrary")),
        name="lru_branch",
    )(z, z, conv_w, rows(conv_b), w_rg, rows(b_rg), w_ig, rows(b_ig), rows(lam), w_rnn_out)


ATT_QROWS = 512


def _attn_body(n_back, q_ref, kp_ref, kc_ref, vp_ref, vc_ref, o_ref, lse_ref, s_s, p_s):
    n = pl.program_id(2)
    blk = ATT_BLOCK
    nsub = q_ref.shape[2] // blk
    qi = lax.broadcasted_iota(I32, (blk, 2 * blk), 0)
    kj = lax.broadcasted_iota(I32, (blk, 2 * blk), 1)
    diff = blk + qi - kj
    in_window = (diff >= 0) & (diff <= n_back)
    lane = lax.broadcasted_iota(I32, (blk, LANES), 1)
    scale = jnp.asarray(ATT_HEAD_DIM ** -0.5, q_ref.dtype)
    low_half = lane < ATT_HEAD_DIM
    ones = jnp.ones((2 * blk, LANES), BF16)
    for sb in range(nsub):
        rows = slice(sb * blk, (sb + 1) * blk)
        q = q_ref[0, 0, rows, :] * scale
        if sb == 0:
            k = jnp.concatenate([kp_ref[0, 0], kc_ref[0, 0, rows, :]], axis=0)
            v = jnp.concatenate([vp_ref[0, 0], vc_ref[0, 0, rows, :]], axis=0)
            valid = in_window & ((kj >= blk) | (n > 0))
        else:
            k = kc_ref[0, 0, (sb - 1) * blk:(sb + 1) * blk, :]
            v = vc_ref[0, 0, (sb - 1) * blk:(sb + 1) * blk, :]
            valid = in_window
        for h in range(ATT_HEADS):
            tile = slice((h // 2) * LANES, (h // 2 + 1) * LANES)
            mine = low_half if h % 2 == 0 else ~low_half
            qh = jnp.where(mine, q[:, tile], jnp.zeros_like(q[:, tile]))
            s_s[h] = lax.dot_general(qh, k[:, tile], (((1,), (1,)), ((), ())),
                                     preferred_element_type=F32)
        m_tile = jnp.zeros((blk, LANES), F32)
        for h in range(ATT_HEADS):
            s = jnp.where(valid, s_s[h], NEG_BIG)
            m = jnp.max(s, axis=-1, keepdims=True)
            p_s[h] = jnp.exp(s - m).astype(BF16)
            m_tile = jnp.where(lane == h, m, m_tile)
        lse_tile = jnp.zeros((blk, LANES), F32)
        for pair in range(ATT_HEADS // 2):
            tile = slice(pair * LANES, (pair + 1) * LANES)
            o_pair, l_pair = [], []
            for h in (2 * pair, 2 * pair + 1):
                o_pair.append(jnp.dot(p_s[h], v[:, tile], preferred_element_type=F32))
                l_pair.append(jnp.dot(p_s[h], ones, preferred_element_type=F32))
                lse_tile = jnp.where(lane == h, m_tile + jnp.log(l_pair[-1]), lse_tile)
            o_ref[0, 0, rows, tile] = jnp.where(low_half, o_pair[0] / l_pair[0],
                                                o_pair[1] / l_pair[1]).astype(o_ref.dtype)
        lse_ref[0, 0, rows, :] = lse_tile


def _attention_group(qkv, gi):
    batch, dil, l, _ = qkv.shape
    n_back = ATT_GROUPS[gi][0] // dil
    qrows = min(ATT_QROWS, l)
    nsub = qrows // ATT_BLOCK
    cur = lambda c: pl.BlockSpec((1, 1, qrows, ATT_W), lambda b, r, n: (b, r, n, c))
    prev = lambda c: pl.BlockSpec((1, 1, ATT_BLOCK, ATT_W),
                                  lambda b, r, n: (b, r, jnp.maximum(n * nsub - 1, 0), c))
    return pl.pallas_call(
        functools.partial(_attn_body, n_back),
        grid=(batch, dil, l // qrows),
        in_specs=[cur(0), prev(1), cur(1), prev(2), cur(2)],
        out_specs=[pl.BlockSpec((1, 1, qrows, ATT_W), lambda b, r, n: (b, r, n, 0)),
                   pl.BlockSpec((1, 1, qrows, LANES), lambda b, r, n: (b, r, n, 0))],
        out_shape=[jax.ShapeDtypeStruct((batch, dil, l, ATT_W), BF16),
                   jax.ShapeDtypeStruct((batch, dil, l, LANES), F32)],
        scratch_shapes=[pltpu.VMEM((ATT_HEADS, ATT_BLOCK, 2 * ATT_BLOCK), F32),
                        pltpu.VMEM((ATT_HEADS, ATT_BLOCK, 2 * ATT_BLOCK), BF16)],
        compiler_params=_cparams(("parallel", "parallel", "arbitrary")),
        name=f"attn_g{gi}",
    )(qkv, qkv, qkv, qkv, qkv)


def _merge_body(o0_ref, o1_ref, o2_ref, l0_ref, l1_ref, l2_ref, ya_ref, ga_ref, gb_ref,
                x_ref, watt_ref, wout_ref, g_ref, b_ref, wr_ref, br_ref,
                x1_ref, x1t_ref, route_ref, cnt_ref, carry_ref, o_s, l_s):
    i = pl.program_id(0)

    @pl.when(i == 0)
    def _():
        carry_ref[...] = jnp.zeros_like(carry_ref)

    ts = x_ref.shape[0]

    def token_order(ref, scratch):
        dil = ref.shape[1]
        if dil == 1:
            return ref[0, 0].astype(F32)
        nl = ref.shape[3] // LANES
        for r in range(dil):
            v = ref[0, r].astype(F32)
            for lc in range(nl):
                scratch[lc, pl.ds(r, ts // dil, stride=dil), :] = v[:, lc * LANES:(lc + 1) * LANES]
        return jnp.concatenate([scratch[lc] for lc in range(nl)], axis=-1)

    group_o = [token_order(ref, o_s.at[gi]) for gi, ref in enumerate((o0_ref, o1_ref, o2_ref))]
    lses = [token_order(ref, l_s.at[gi]) for gi, ref in enumerate((l0_ref, l1_ref, l2_ref))]
    mx = jnp.maximum(jnp.maximum(lses[0], lses[1]), lses[2])
    es = [jnp.exp(v - mx) for v in lses]
    den = es[0] + es[1] + es[2]
    er = lax.broadcasted_iota(I32, (LANES, ATT_W), 0)
    ec = lax.broadcasted_iota(I32, (LANES, ATT_W), 1)
    expand = jnp.where(ec // ATT_HEAD_DIM == er, 1.0, 0.0).astype(BF16)
    o = jnp.zeros((ts, ATT_W), F32)
    for e, og in zip(es, group_o):
        w = e / den
        w_hi = w.astype(BF16)
        w_lo = (w - w_hi.astype(F32)).astype(BF16)
        wx = (jnp.dot(w_hi, expand, preferred_element_type=F32)
              + jnp.dot(w_lo, expand, preferred_element_type=F32))
        o = o + wx * og
    y_b = jnp.dot(o.astype(BF16), watt_ref[...], preferred_element_type=F32)
    ga = ga_ref[...].astype(F32)
    gb = gb_ref[...].astype(F32)
    merged = _sigmoid(ga) * ya_ref[...].astype(F32) + _sigmoid(gb) * y_b
    hmix = jnp.dot(merged.astype(BF16), wout_ref[...], preferred_element_type=F32)
    x1 = _layer_norm(ALPHA * x_ref[...] + hmix, g_ref[...], b_ref[...])
    x1_ref[...] = x1
    _to_row_tiles(x1t_ref, x1, ts)

    wr = wr_ref[...]
    wr_hi = wr.astype(BF16)
    wr_lo = (wr - wr_hi.astype(F32)).astype(BF16)
    x_hi = x1.astype(BF16)
    x_lo = (x1 - x_hi.astype(F32)).astype(BF16)
    logits = (jnp.dot(x_hi, wr_hi, preferred_element_type=F32)
              + jnp.dot(x_lo, wr_hi, preferred_element_type=F32)
              + jnp.dot(x_hi, wr_lo, preferred_element_type=F32)) + br_ref[...]

    el = lax.broadcasted_iota(I32, (ts, N_EXPERTS), 1)
    work = logits
    vals, idxs, hots = [], [], []
    for _ in range(TOP_K):
        m = jnp.max(work, axis=-1, keepdims=True)
        idx = jnp.min(jnp.where(work == m, el, N_EXPERTS), axis=-1, keepdims=True)
        hot = el == idx
        vals.append(m)
        idxs.append(idx)
        hots.append(hot)
        work = jnp.where(hot, NEG_BIG, work)
    exps = [jnp.exp(v - vals[0]) for v in vals]
    gden = exps[0] + exps[1] + exps[2] + exps[3]

    cnt = jnp.zeros((ts, N_EXPERTS), F32)
    for hot in hots:
        cnt = cnt + jnp.where(hot, 1.0, 0.0)
    tr = lax.broadcasted_iota(I32, (ts, ts), 0)
    tc = lax.broadcasted_iota(I32, (ts, ts), 1)
    tri = jnp.where(tc < tr, 1.0, 0.0).astype(BF16)
    before = jnp.dot(tri, cnt.astype(BF16), preferred_element_type=F32) + carry_ref[...]
    carry_ref[...] = carry_ref[...] + jnp.sum(cnt, axis=0, keepdims=True)
    cnt_ref[...] = carry_ref[...]

    lane = lax.broadcasted_iota(I32, (ts, LANES), 1)
    route = jnp.zeros((ts, LANES), I32)
    for kk in range(TOP_K):
        rank = jnp.sum(jnp.where(hots[kk], before, 0.0), axis=-1, keepdims=True).astype(I32)
        gate_bits = pltpu.bitcast(exps[kk] / gden, I32)
        route = jnp.where(lane == kk, idxs[kk], route)
        route = jnp.where(lane == TOP_K + kk, rank, route)
        route = jnp.where(lane == 2 * TOP_K + kk, gate_bits, route)
    route_ref[...] = route


def _merge(outs, lses, y_a, zd, x2d, seq, li, w_att_out, w_out, ln_g, ln_b, w_router, b_router):
    t = x2d.shape[0]
    ts = min(256, seq)
    ns = seq // ts
    rows = lambda v: v.reshape(v.shape[0], 1, v.shape[-1])
    tile = lambda w, c=0: pl.BlockSpec((ts, w), lambda i: (i, c))

    def by_residue(a):
        dil, w = a.shape[1], a.shape[3]
        return pl.BlockSpec((1, dil, ts // dil, w), lambda i: (i // ns, 0, i % ns, 0))

    return pl.pallas_call(
        _merge_body,
        grid=(t // ts,),
        in_specs=[by_residue(a) for a in (*outs, *lses)]
                 + [tile(D_MODEL), tile(D_MODEL, 2 * D_RNN // D_MODEL),
                    tile(D_MODEL, 2 * D_RNN // D_MODEL + 1), tile(D_MODEL),
                    _layer_spec(li, (ATT_W, D_MODEL)), _layer_spec(li, (D_MODEL, D_MODEL)),
                    _layer_spec(li, (1, D_MODEL)), _layer_spec(li, (1, D_MODEL)),
                    _layer_spec(li, (D_MODEL, N_EXPERTS)), _layer_spec(li, (1, N_EXPERTS))],
        out_specs=[tile(D_MODEL), pl.BlockSpec((ts * ROW_TILES, LANES), lambda i: (i, 0)),
                   tile(LANES), pl.BlockSpec((1, N_EXPERTS), lambda i: (0, 0))],
        out_shape=[jax.ShapeDtypeStruct((t, D_MODEL), F32),
                   jax.ShapeDtypeStruct((t * ROW_TILES, LANES), F32),
                   jax.ShapeDtypeStruct((t, LANES), I32),
                   jax.ShapeDtypeStruct((1, N_EXPERTS), F32)],
        scratch_shapes=[pltpu.VMEM((1, N_EXPERTS), F32),
                        pltpu.VMEM((N_GROUPS, ATT_W // LANES, ts, LANES), F32),
                        pltpu.VMEM((N_GROUPS, 1, ts, LANES), F32)],
        compiler_params=_cparams(("arbitrary",)),
        name="merge_ln1_router",
    )(*outs, *lses, y_a, zd, zd, x2d, w_att_out, w_out, rows(ln_g), rows(ln_b), w_router,
      rows(b_router))


INVERT_UNROLL = 32
ROW_BUFS = 4


def _build_row_order(dest_ref, fill_ref, n_ref, inv_ref):
    def zero_range(lo, hi):
        def z(row, carry):
            inv_ref[row] = 0
            return carry
        lax.fori_loop(lo, hi, z, 0)

    def per_expert(e, carry):
        zero_range(fill_ref[2 * e], fill_ref[2 * e + 1])
        return carry
    lax.fori_loop(0, N_EXPERTS, per_expert, 0)
    zero_range(fill_ref[2 * N_EXPERTS] * MOE_ROWS, fill_ref[2 * N_EXPERTS + 1] * MOE_ROWS)

    def chunk(c, carry):
        base = c * INVERT_UNROLL
        tok0 = c * (INVERT_UNROLL // TOP_K)
        for u in range(INVERT_UNROLL):
            inv_ref[dest_ref[base + u]] = tok0 + u // TOP_K
        return carry
    lax.fori_loop(0, n_ref[0] // INVERT_UNROLL, chunk, 0)


def _expert_body(li, n_blk, be_ref, nvb_ref, first_ref, wslot_ref, nxt_ref, dest_ref, fill_ref,
                 n_ref, x1t_hbm, wg_hbm, wu_hbm, wd_hbm, bg_ref, bu_ref, bd_ref, y_ref,
                 inv_s, wst, wg_s, wu_s, wd_s, *rest):
    j = pl.program_id(0)
    nvb = nvb_ref[0]
    weights = (wg_hbm, wu_hbm, wd_hbm)
    bufs = rest[:ROW_BUFS]
    sem, wsem = rest[ROW_BUFS:]
    ahead = ROW_BUFS - 1

    def issue(blk, k):
        base = jnp.minimum(blk, n_blk - 1) * MOE_ROWS
        for r in range(MOE_ROWS):
            tok = inv_s[base + r]
            src = x1t_hbm.at[pl.ds(pl.multiple_of(tok * ROW_TILES, ROW_TILES), ROW_TILES)]
            pltpu.make_async_copy(src, bufs[k].at[pl.ds(r * ROW_TILES, ROW_TILES)],
                                  sem.at[k]).start()

    def wait_rows(k):
        pltpu.make_async_copy(bufs[k], bufs[k], sem.at[k]).wait()

    def weight_copies(e, slot):
        return [pltpu.make_async_copy(w.at[li, e], wst.at[slot, k], wsem.at[slot])
                for k, w in enumerate(weights)]

    @pl.when(j == 0)
    def _():
        for cp in weight_copies(be_ref[0], 0):
            cp.start()
        _build_row_order(dest_ref, fill_ref, n_ref, inv_s)
        for b in range(ahead):
            issue(b, b)

    @pl.when(first_ref[j] == 1)
    def _():
        slot = wslot_ref[j]
        for cp in weight_copies(0, slot):
            cp.wait()
        wg_s[...] = wst[slot, 0].astype(BF16)
        wu_s[...] = wst[slot, 1].astype(BF16)
        wd_s[...] = wst[slot, 2].astype(BF16)

        @pl.when(nxt_ref[j] >= 0)
        def _():
            for cp in weight_copies(nxt_ref[j], 1 - slot):
                cp.start()

    def compute(k):
        issue(j + ahead, (k + ahead) % ROW_BUFS)
        wait_rows(k)
        x = _from_row_tiles(bufs[k], MOE_ROWS).astype(BF16)
        g = jnp.dot(x, wg_s[...], preferred_element_type=F32) + bg_ref[...]
        up = jnp.dot(x, wu_s[...], preferred_element_type=F32) + bu_ref[...]
        g = jnp.minimum(g, SWIGLU_LIMIT)
        up = jnp.clip(up, -SWIGLU_LIMIT, SWIGLU_LIMIT)
        hdn = (up + 1.0) * (g * _sigmoid(SWIGLU_ALPHA * g))
        y = jnp.dot(hdn.astype(BF16), wd_s[...], preferred_element_type=F32) + bd_ref[...]
        _to_row_tiles(y_ref, y, MOE_ROWS)

    for k in range(ROW_BUFS):
        @pl.when((j < nvb) & (j % ROW_BUFS == k))
        def _(k=k):
            compute(k)

        @pl.when((j >= nvb) & (j < nvb + ahead) & (j % ROW_BUFS == k))
        def _(k=k):
            wait_rows(k)

    @pl.when(j >= nvb)
    def _():
        y_ref[...] = jnp.zeros_like(y_ref)


def _experts(x1t, tables, li, w_gate, b_gate, w_up, b_up, w_down, b_down):
    n_rows = tables['n_rows']
    n_blk = n_rows // MOE_ROWS
    d = D_MODEL
    blk = (MOE_ROWS * ROW_TILES, LANES)
    prefetch = (tables['blk_expert'], tables['n_used'], tables['first'], tables['wslot'],
                tables['next_expert'], tables['dest'], tables['fill'], tables['n_assign'])
    bspec = pl.BlockSpec((None, None, 1, d),
                         lambda j, be, *_: (li, be[jnp.minimum(j, n_blk - 1)], 0, 0))
    bias = lambda v: v.reshape(v.shape[0], v.shape[1], 1, v.shape[2])
    hbm = pl.BlockSpec(memory_space=pl.ANY)
    rows = pltpu.VMEM(blk, F32)
    return pl.pallas_call(
        functools.partial(_expert_body, li, n_blk),
        grid_spec=pltpu.PrefetchScalarGridSpec(
            num_scalar_prefetch=len(prefetch), grid=(n_blk + ROW_BUFS - 1,),
            in_specs=[hbm, hbm, hbm, hbm, bspec, bspec, bspec],
            out_specs=pl.BlockSpec(blk, lambda j, *_: (jnp.minimum(j, n_blk - 1), 0)),
            scratch_shapes=[pltpu.SMEM((n_rows,), I32), pltpu.VMEM((2, 3, d, D_FF), F32),
                            pltpu.VMEM((d, D_FF), BF16), pltpu.VMEM((d, D_FF), BF16),
                            pltpu.VMEM((D_FF, d), BF16)] + [rows] * ROW_BUFS
                           + [pltpu.SemaphoreType.DMA((ROW_BUFS,)), pltpu.SemaphoreType.DMA((2,))]),
        out_shape=jax.ShapeDtypeStruct((n_rows * ROW_TILES, LANES), F32),
        compiler_params=_cparams(("arbitrary",)),
        name="moe_experts",
    )(*prefetch, x1t, w_gate, w_up, w_down, bias(b_gate), bias(b_up), bias(b_down))


COMBINE_BUFS = 3


def _combine_body(n_tiles, dest_ref, yb_hbm, route_ref, x1_ref, p_ref, wple_ref, wpg_ref, bpg_ref,
                  g2_ref, b2_ref, g3_ref, b3_ref, o_ref, buf0, buf1, buf2, sem):
    i = pl.program_id(0)
    ts = x1_ref.shape[0]
    bufs = (buf0, buf1, buf2)
    ahead = COMBINE_BUFS - 1

    def issue(tile, k):
        base = jnp.minimum(tile, n_tiles - 1) * (ts * TOP_K)
        for j in range(ts):
            for kk in range(TOP_K):
                row = dest_ref[base + j * TOP_K + kk]
                src = yb_hbm.at[pl.ds(pl.multiple_of(row * ROW_TILES, ROW_TILES), ROW_TILES)]
                pltpu.make_async_copy(src, bufs[k].at[kk, pl.ds(j * ROW_TILES, ROW_TILES)],
                                      sem.at[k]).start()

    def wait_rows(k):
        pltpu.make_async_copy(bufs[k], bufs[k], sem.at[k]).wait()

    @pl.when(i == 0)
    def _():
        for b in range(ahead):
            issue(b, b)

    def compute(k):
        issue(i + ahead, (k + ahead) % COMBINE_BUFS)
        wait_rows(k)
        route = route_ref[...]
        gates = [pltpu.bitcast(route[:, 2 * TOP_K + kk:2 * TOP_K + kk + 1], F32)
                 for kk in range(TOP_K)]
        parts = []
        for c in range(ROW_TILES):
            acc = jnp.zeros((ts, LANES), F32)
            for kk in range(TOP_K):
                acc = acc + gates[kk] * bufs[k][kk, pl.ds(c, ts, stride=ROW_TILES), :]
            parts.append(acc)
        y = jnp.concatenate(parts, axis=-1)
        x2 = _layer_norm(ALPHA * x1_ref[...] + y, g2_ref[...], b2_ref[...])
        emb = jnp.dot(p_ref[...].astype(BF16), wple_ref[...], preferred_element_type=F32)
        gate_in = jnp.dot(x2.astype(BF16), wpg_ref[...], preferred_element_type=F32) + bpg_ref[...]
        ple = emb * _sigmoid(gate_in)
        o_ref[...] = _layer_norm(ALPHA * x2 + ple, g3_ref[...], b3_ref[...])

    for k in range(COMBINE_BUFS):
        @pl.when((i < n_tiles) & (i % COMBINE_BUFS == k))
        def _(k=k):
            compute(k)

        @pl.when((i >= n_tiles) & (i % COMBINE_BUFS == k))
        def _(k=k):
            wait_rows(k)


def _combine(dest_flat, yb, route, x1, p3d, li, w_ple, w_ple_gate, b_ple_gate, ln2_g, ln2_b,
             ln3_g, ln3_b):
    t, d = x1.shape
    ts = min(256, t)
    n = t // ts
    rows = lambda v: v.reshape(v.shape[0], 1, v.shape[-1])
    tile = lambda w: pl.BlockSpec((ts, w), lambda i, dst: (jnp.minimum(i, n - 1), 0))
    buf = pltpu.VMEM((TOP_K, ts * ROW_TILES, LANES), F32)
    return pl.pallas_call(
        functools.partial(_combine_body, n),
        grid_spec=pltpu.PrefetchScalarGridSpec(
            num_scalar_prefetch=1, grid=(n + COMBINE_BUFS - 1,),
            in_specs=[pl.BlockSpec(memory_space=pl.ANY), tile(LANES), tile(d),
                      pl.BlockSpec((None, ts, PLE_DIM),
                                   lambda i, dst: (li, jnp.minimum(i, n - 1), 0)),
                      _layer_spec(li, (PLE_DIM, d)), _layer_spec(li, (d, d)),
                      _layer_spec(li, (1, d)), _layer_spec(li, (1, d)), _layer_spec(li, (1, d)),
                      _layer_spec(li, (1, d)), _layer_spec(li, (1, d))],
            out_specs=tile(d),
            scratch_shapes=[buf] * COMBINE_BUFS + [pltpu.SemaphoreType.DMA((COMBINE_BUFS,))]),
        out_shape=jax.ShapeDtypeStruct((t, d), F32),
        compiler_params=_cparams(("arbitrary",)),
        name="moe_combine_ple",
    )(dest_flat, yb, route, x1, p3d, w_ple, w_ple_gate, rows(b_ple_gate), rows(ln2_g), rows(ln2_b),
      rows(ln3_g), rows(ln3_b))


def _routing_tables(route, counts, t):
    counts = counts.reshape(N_EXPERTS).astype(I32)
    padded = (counts + MOE_ROWS - 1) // MOE_ROWS * MOE_ROWS
    pend = jnp.cumsum(padded)
    pstart = pend - padded
    top_e = route[:, 0:TOP_K]
    rank = route[:, TOP_K:2 * TOP_K]
    onehot = top_e[:, :, None] == jnp.arange(N_EXPERTS, dtype=I32)
    dest = (jnp.sum(jnp.where(onehot, pstart, 0), axis=-1) + rank).reshape(t * TOP_K)
    n_blk = t * TOP_K // MOE_ROWS + N_EXPERTS
    n_used = pend[N_EXPERTS - 1] // MOE_ROWS
    fill = jnp.concatenate([jnp.stack([pstart + counts, pend], axis=1).reshape(2 * N_EXPERTS),
                            jnp.stack([n_used, jnp.asarray(n_blk, I32)])]).astype(I32)
    n_tab = n_blk + ROW_BUFS - 1
    blk = jnp.arange(n_tab, dtype=I32)
    blk_expert = jnp.minimum(jnp.sum((pend[None, :] <= (blk * MOE_ROWS)[:, None]).astype(I32),
                                     axis=1), N_EXPERTS - 1)
    first = (blk < n_used) & ((blk == 0) | (blk_expert != jnp.roll(blk_expert, 1)))
    wslot = (jnp.cumsum(first.astype(I32)) - 1) % 2
    first_at = jnp.where(first, blk, n_tab)
    next_first = jnp.concatenate([lax.cummin(first_at[::-1])[::-1][1:],
                                  jnp.full((1,), n_tab, I32)])
    next_expert = jnp.where(next_first < n_tab,
                            blk_expert[jnp.minimum(next_first, n_tab - 1)], -1)
    return dict(dest=dest, fill=fill, blk_expert=blk_expert, n_used=n_used.reshape(1),
                first=first.astype(I32), wslot=wslot.astype(I32),
                next_expert=next_expert.astype(I32), n_assign=jnp.full((1,), t * TOP_K, I32),
                n_rows=n_blk * MOE_ROWS)


def _layer(x2d, p3d, batch, seq, li, w):
    t = batch * seq
    zd, *qkvs = _project(x2d, w['w_in'], li, batch, seq)
    y_a = _lru_branch(zd, batch, seq, li, w['conv_w'], w['conv_b'], w['w_rg'], w['b_rg'],
                      w['w_ig'], w['b_ig'], w['lru_lambda'], w['w_rnn_out'])
    outs, lses = [], []
    for gi in range(N_GROUPS):
        o, lse = _attention_group(qkvs[gi], gi)
        outs.append(o)
        lses.append(lse)
    x1, x1t, route, counts = _merge(outs, lses, y_a, zd, x2d, seq, li, w['w_att_out'], w['w_out'],
                                    w['ln1_g'], w['ln1_b'], w['w_router'], w['b_router'])
    tables = _routing_tables(route, counts, t)
    yb = _experts(x1t, tables, li, w['w_gate'], w['b_gate'], w['w_up'], w['b_up'], w['w_down'],
                  w['b_down'])
    return _combine(tables['dest'], yb, route, x1, p3d, li, w['w_ple'], w['w_ple_gate'],
                    w['b_ple_gate'],
                    w['ln2_g'], w['ln2_b'], w['ln3_g'], w['ln3_b'])


_WEIGHT_NAMES = ('w_in', 'conv_w', 'conv_b', 'w_rg', 'b_rg', 'w_ig', 'b_ig', 'lru_lambda',
                 'w_rnn_out', 'w_att_out', 'w_out', 'ln1_g', 'ln1_b', 'w_router', 'b_router',
                 'w_gate', 'b_gate', 'w_up', 'b_up', 'w_down', 'b_down', 'ln2_g', 'ln2_b',
                 'w_ple', 'w_ple_gate', 'b_ple_gate', 'ln3_g', 'ln3_b')
_BF16_WEIGHTS = ('w_in', 'w_rg', 'w_ig', 'w_rnn_out', 'w_att_out', 'w_out', 'w_ple', 'w_ple_gate')


def kernel(x, p, w_in, conv_w, conv_b, w_rg, b_rg, w_ig, b_ig, lru_lambda, w_rnn_out, w_att_out,
           w_out, ln1_g, ln1_b, w_router, b_router, w_gate, b_gate, w_up, b_up, w_down, b_down,
           ln2_g, ln2_b, w_ple, w_ple_gate, b_ple_gate, ln3_g, ln3_b):
    w = dict(zip(_WEIGHT_NAMES, (
        w_in, conv_w, conv_b, w_rg, b_rg, w_ig, b_ig, lru_lambda, w_rnn_out, w_att_out, w_out,
        ln1_g, ln1_b, w_router, b_router, w_gate, b_gate, w_up, b_up, w_down, b_down, ln2_g,
        ln2_b, w_ple, w_ple_gate, b_ple_gate, ln3_g, ln3_b)))
    for name in _BF16_WEIGHTS:
        w[name] = w[name].astype(BF16)
    batch, seq, d = x.shape
    depth = p.shape[0]
    x2d = x.reshape(batch * seq, d)
    p3d = p.reshape(depth, batch * seq, PLE_DIM)
    for li in range(depth):
        x2d = _layer(x2d, p3d, batch, seq, li, w)
    return x2d.reshape(batch, seq, d)
```

```python
import functools

import jax
import jax.numpy as jnp
from jax import lax
from jax.experimental import pallas as pl
from jax.experimental.pallas import tpu as pltpu

F32 = jnp.float32
BF16 = jnp.bfloat16
I32 = jnp.int32

D_MODEL = 1024
DEPTH = 2
D_RNN = 1024
RNN_BLOCKS = 4
RNN_BW = D_RNN // RNN_BLOCKS
CONV_W = 4
LRU_C = 8.0
ATT_GROUPS = ((128, 1), (512, 4), (2048, 16))
N_GROUPS = len(ATT_GROUPS)
ATT_HEADS = 8
ATT_HEAD_DIM = 64
ATT_W = ATT_HEADS * ATT_HEAD_DIM
ATT_BLOCK = 128
OFF_XR = 0
OFF_YR = OFF_XR + D_RNN
OFF_Q = OFF_YR + D_RNN
OFF_K = OFF_Q + N_GROUPS * ATT_W
OFF_V = OFF_K + N_GROUPS * ATT_W
OFF_GA = OFF_V + N_GROUPS * ATT_W
OFF_GB = OFF_GA + D_MODEL
N_IN = OFF_GB + D_MODEL
N_EXPERTS = 32
TOP_K = 4
D_FF = D_MODEL
SWIGLU_ALPHA = 1.702
SWIGLU_LIMIT = 7.0
PLE_DIM = 256
ALPHA = (2.0 * DEPTH) ** 0.25
LN_EPS = 1e-5

LANES = 128
SUBLANES = 8
ROW_TILES = D_MODEL // LANES
MOE_ROWS = 256
V7X_VMEM_BYTES = 64 * 1024 * 1024
VMEM_LIMIT = V7X_VMEM_BYTES - 8 * 1024 * 1024
NEG_BIG = -1e30

assert ROW_TILES == SUBLANES


def _cparams(sem):
    return pltpu.CompilerParams(dimension_semantics=sem, vmem_limit_bytes=VMEM_LIMIT)


def _layer_spec(li, shape):
    return pl.BlockSpec((None,) + tuple(shape), lambda *_: (li,) + (0,) * len(shape))


def _layer_norm(v, g, b):
    mu = jnp.mean(v, axis=-1, keepdims=True)
    c = v - mu
    var = jnp.mean(c * c, axis=-1, keepdims=True)
    return c * lax.rsqrt(var + LN_EPS) * g + b


def _sigmoid(v):
    return 0.5 * jnp.tanh(0.5 * v) + 0.5


def _to_row_tiles(ref, m, rows):
    for c in range(ROW_TILES):
        ref[pl.ds(c, rows, stride=ROW_TILES), :] = m[:, c * LANES:(c + 1) * LANES]


def _from_row_tiles(ref, rows):
    return jnp.concatenate([ref[pl.ds(c, rows, stride=ROW_TILES), :] for c in range(ROW_TILES)],
                           axis=-1)


PROJ_ROWS = 512
DENSE_W = 2 * D_RNN + 2 * D_MODEL


def _proj_body(x_ref, w_ref, zd_ref, q0_ref, q1_ref, q2_ref, res_s):
    x = x_ref[...].astype(BF16)
    tm = x.shape[0]
    lo = 2 * D_RNN
    zd_ref[:, 0:lo] = jnp.dot(x, w_ref[:, 0:lo], preferred_element_type=F32).astype(zd_ref.dtype)
    zd_ref[:, lo:] = jnp.dot(x, w_ref[:, OFF_GA:N_IN],
                             preferred_element_type=F32).astype(zd_ref.dtype)
    for gi, o_ref in enumerate((q0_ref, q1_ref, q2_ref)):
        dil = ATT_GROUPS[gi][1]
        for c, off in enumerate((OFF_Q, OFF_K, OFF_V)):
            col = off + gi * ATT_W
            res = jnp.dot(x, w_ref[:, col:col + ATT_W], preferred_element_type=F32)
            cols = slice(c * ATT_W, (c + 1) * ATT_W)
            if dil == 1:
                o_ref[0, 0, :, cols] = res.astype(o_ref.dtype)
            else:
                for lc in range(ATT_W // LANES):
                    res_s[lc] = res[:, lc * LANES:(lc + 1) * LANES]
                for r in range(dil):
                    part = [res_s[lc, pl.ds(r, tm // dil, stride=dil), :]
                            for lc in range(ATT_W // LANES)]
                    o_ref[0, r, :, cols] = jnp.concatenate(part, axis=-1).astype(o_ref.dtype)


def _project(x2d, w_in, li, batch, seq):
    d = x2d.shape[1]
    tm = min(PROJ_ROWS, seq)
    ns = seq // tm
    dils = [dil for _, dil in ATT_GROUPS]
    return pl.pallas_call(
        _proj_body,
        grid=(batch, ns),
        in_specs=[pl.BlockSpec((tm, d), lambda b, s: (b * ns + s, 0)),
                  pl.BlockSpec((None, d, N_IN), lambda b, s: (li, 0, 0),
                               pipeline_mode=pl.Buffered(1))],
        out_specs=[pl.BlockSpec((tm, DENSE_W), lambda b, s: (b * ns + s, 0))]
                  + [pl.BlockSpec((1, dil, tm // dil, 3 * ATT_W), lambda b, s: (b, 0, s, 0))
                     for dil in dils],
        out_shape=[jax.ShapeDtypeStruct((batch * seq, DENSE_W), BF16)]
                  + [jax.ShapeDtypeStruct((batch, dil, seq // dil, 3 * ATT_W), BF16)
                     for dil in dils],
        scratch_shapes=[pltpu.VMEM((ATT_W // LANES, tm, LANES), F32)],
        compiler_params=_cparams(("parallel", "parallel")),
        name="in_proj",
    )(x2d, w_in)


def _lru_body(xr_ref, yr_ref, cw_ref, cb_ref, wrg_ref, brg_ref, wig_ref, big_ref, lam_ref,
              wout_ref, o_ref, tail_ref, h_ref):
    s = pl.program_id(1)

    @pl.when(s == 0)
    def _():
        tail_ref[...] = jnp.zeros_like(tail_ref)
        h_ref[...] = jnp.zeros_like(h_ref)

    xr = xr_ref[...].astype(F32)
    ts, c = xr.shape
    groups = ts // SUBLANES
    sub = lax.broadcasted_iota(I32, (1, SUBLANES, 1), 1)
    xe3 = jnp.concatenate([tail_ref[...], xr], axis=0).reshape(groups + 1, SUBLANES, c)
    cw = cw_ref[...]
    xc3 = (cb_ref[...] + cw[CONV_W - 1:CONV_W] * xr).reshape(groups, SUBLANES, c)
    for k in range(1, CONV_W):
        rot = pltpu.roll(xe3, k, 1)
        xc3 = xc3 + cw[CONV_W - 1 - k:CONV_W - k] * jnp.where(sub >= k, rot[1:], rot[:-1])
    xc = xc3.reshape(ts, c)
    tail_ref[...] = xr[ts - 8:]

    xcb = xc.astype(BF16)

    def gate(w_ref, b_ref):
        parts = [jnp.dot(xcb[:, n * RNN_BW:(n + 1) * RNN_BW], w_ref[n],
                         preferred_element_type=F32) for n in range(RNN_BLOCKS)]
        return _sigmoid(jnp.concatenate(parts, axis=-1) + b_ref[...])

    r = gate(wrg_ref, brg_ref)
    i = gate(wig_ref, big_ref)
    nlam = -lam_ref[...]
    softplus = jnp.maximum(nlam, 0.0) + jnp.log1p(jnp.exp(-jnp.abs(nlam)))
    log_a = (-LRU_C) * r * softplus
    a = jnp.exp(log_a)
    bx = jnp.sqrt(jnp.tanh(-log_a) * (1.0 + a * a)) * (i * xc)

    a3 = a.reshape(groups, SUBLANES, c)
    b3 = bx.reshape(groups, SUBLANES, c)
    k = 1
    while k < SUBLANES:
        keep = sub >= k
        a_prev = pltpu.roll(a3, k, 1)
        b_prev = pltpu.roll(b3, k, 1)
        b3 = jnp.where(keep, a3 * b_prev, 0.0) + b3
        a3 = jnp.where(keep, a3 * a_prev, a3)
        k *= 2
    carry = h_ref[0:1]
    hs = []
    for gidx in range(groups):
        hg = a3[gidx] * carry + b3[gidx]
        hs.append(hg)
        carry = hg[SUBLANES - 1:SUBLANES]
    h = jnp.concatenate(hs, axis=0)
    h_ref[0:1] = carry

    yr = yr_ref[...].astype(F32)
    gelu = 0.5 * yr * (1.0 + jnp.tanh(0.7978845608028654 * (yr + 0.044715 * (yr * yr * yr))))
    o_ref[...] = jnp.dot((gelu * h).astype(BF16), wout_ref[...],
                         preferred_element_type=F32).astype(o_ref.dtype)


def _lru_branch(z, batch, seq, li, conv_w, conv_b, w_rg, b_rg, w_ig, b_ig, lam, w_rnn_out):
    ts = min(512, seq)
    ns = seq // ts
    c = D_RNN
    rows = lambda v: v.reshape(v.shape[0], 1, c)
    return pl.pallas_call(
        _lru_body,
        grid=(batch, ns),
        in_specs=[pl.BlockSpec((ts, c), lambda b, s: (b * ns + s, OFF_XR // c)),
                  pl.BlockSpec((ts, c), lambda b, s: (b * ns + s, OFF_YR // c)),
                  _layer_spec(li, (CONV_W, c)), _layer_spec(li, (1, c)),
                  _layer_spec(li, (RNN_BLOCKS, RNN_BW, RNN_BW)), _layer_spec(li, (1, c)),
                  _layer_spec(li, (RNN_BLOCKS, RNN_BW, RNN_BW)), _layer_spec(li, (1, c)),
                  _layer_spec(li, (1, c)), _layer_spec(li, (c, D_MODEL))],
        out_specs=pl.BlockSpec((ts, D_MODEL), lambda b, s: (b * ns + s, 0)),
        out_shape=jax.ShapeDtypeStruct((batch * seq, D_MODEL), BF16),
        scratch_shapes=[pltpu.VMEM((8, c), F32), pltpu.VMEM((8, c), F32)],
        compiler_params=_cparams(("arbitrary", "arbitrary")),
        name="lru_branch",
    )(z, z, conv_w, rows(conv_b), w_rg, rows(b_rg), w_ig, rows(b_ig), rows(lam), w_rnn_out)


ATT_QROWS = 512


def _attn_body(n_back, q_ref, kp_ref, kc_ref, vp_ref, vc_ref, o_ref, lse_ref, s_s, p_s):
    n = pl.program_id(2)
    blk = ATT_BLOCK
    nsub = q_ref.shape[2] // blk
    qi = lax.broadcasted_iota(I32, (blk, 2 * blk), 0)
    kj = lax.broadcasted_iota(I32, (blk, 2 * blk), 1)
    diff = blk + qi - kj
    in_window = (diff >= 0) & (diff <= n_back)
    lane = lax.broadcasted_iota(I32, (blk, LANES), 1)
    scale = jnp.asarray(ATT_HEAD_DIM ** -0.5, q_ref.dtype)
    low_half = lane < ATT_HEAD_DIM
    ones = jnp.ones((2 * blk, LANES), BF16)
    for sb in range(nsub):
        rows = slice(sb * blk, (sb + 1) * blk)
        q = q_ref[0, 0, rows, :] * scale
        if sb == 0:
            k = jnp.concatenate([kp_ref[0, 0], kc_ref[0, 0, rows, :]], axis=0)
            v = jnp.concatenate([vp_ref[0, 0], vc_ref[0, 0, rows, :]], axis=0)
            valid = in_window & ((kj >= blk) | (n > 0))
        else:
            k = kc_ref[0, 0, (sb - 1) * blk:(sb + 1) * blk, :]
            v = vc_ref[0, 0, (sb - 1) * blk:(sb + 1) * blk, :]
            valid = in_window
        for h in range(ATT_HEADS):
            tile = slice((h // 2) * LANES, (h // 2 + 1) * LANES)
            mine = low_half if h % 2 == 0 else ~low_half
            qh = jnp.where(mine, q[:, tile], jnp.zeros_like(q[:, tile]))
            s_s[h] = lax.dot_general(qh, k[:, tile], (((1,), (1,)), ((), ())),
                                     preferred_element_type=F32)
        m_tile = jnp.zeros((blk, LANES), F32)
        for h in range(ATT_HEADS):
            s = jnp.where(valid, s_s[h], NEG_BIG)
            m = jnp.max(s, axis=-1, keepdims=True)
            p_s[h] = jnp.exp(s - m).astype(BF16)
            m_tile = jnp.where(lane == h, m, m_tile)
        lse_tile = jnp.zeros((blk, LANES), F32)
        for pair in range(ATT_HEADS // 2):
            tile = slice(pair * LANES, (pair + 1) * LANES)
            o_pair, l_pair = [], []
            for h in (2 * pair, 2 * pair + 1):
                o_pair.append(jnp.dot(p_s[h], v[:, tile], preferred_element_type=F32))
                l_pair.append(jnp.dot(p_s[h], ones, preferred_element_type=F32))
                lse_tile = jnp.where(lane == h, m_tile + jnp.log(l_pair[-1]), lse_tile)
            o_ref[0, 0, rows, tile] = jnp.where(low_half, o_pair[0] / l_pair[0],
                                                o_pair[1] / l_pair[1]).astype(o_ref.dtype)
        lse_ref[0, 0, rows, :] = lse_tile


def _attention_group(qkv, gi):
    batch, dil, l, _ = qkv.shape
    n_back = ATT_GROUPS[gi][0] // dil
    qrows = min(ATT_QROWS, l)
    nsub = qrows // ATT_BLOCK
    cur = lambda c: pl.BlockSpec((1, 1, qrows, ATT_W), lambda b, r, n: (b, r, n, c))
    prev = lambda c: pl.BlockSpec((1, 1, ATT_BLOCK, ATT_W),
                                  lambda b, r, n: (b, r, jnp.maximum(n * nsub - 1, 0), c))
    return pl.pallas_call(
        functools.partial(_attn_body, n_back),
        grid=(batch, dil, l // qrows),
        in_specs=[cur(0), prev(1), cur(1), prev(2), cur(2)],
        out_specs=[pl.BlockSpec((1, 1, qrows, ATT_W), lambda b, r, n: (b, r, n, 0)),
                   pl.BlockSpec((1, 1, qrows, LANES), lambda b, r, n: (b, r, n, 0))],
        out_shape=[jax.ShapeDtypeStruct((batch, dil, l, ATT_W), BF16),
                   jax.ShapeDtypeStruct((batch, dil, l, LANES), F32)],
        scratch_shapes=[pltpu.VMEM((ATT_HEADS, ATT_BLOCK, 2 * ATT_BLOCK), F32),
                        pltpu.VMEM((ATT_HEADS, ATT_BLOCK, 2 * ATT_BLOCK), BF16)],
        compiler_params=_cparams(("parallel", "parallel", "arbitrary")),
        name=f"attn_g{gi}",
    )(qkv, qkv, qkv, qkv, qkv)


def _merge_body(o0_ref, o1_ref, o2_ref, l0_ref, l1_ref, l2_ref, ya_ref, ga_ref, gb_ref,
                x_ref, watt_ref, wout_ref, g_ref, b_ref, wr_ref, br_ref,
                x1_ref, x1t_ref, route_ref, cnt_ref, carry_ref, o_s, l_s):
    i = pl.program_id(0)

    @pl.when(i == 0)
    def _():
        carry_ref[...] = jnp.zeros_like(carry_ref)

    ts = x_ref.shape[0]

    def token_order(ref, scratch):
        dil = ref.shape[1]
        if dil == 1:
            return ref[0, 0].astype(F32)
        nl = ref.shape[3] // LANES
        for r in range(dil):
            v = ref[0, r].astype(F32)
            for lc in range(nl):
                scratch[lc, pl.ds(r, ts // dil, stride=dil), :] = v[:, lc * LANES:(lc + 1) * LANES]
        return jnp.concatenate([scratch[lc] for lc in range(nl)], axis=-1)

    group_o = [token_order(ref, o_s.at[gi]) for gi, ref in enumerate((o0_ref, o1_ref, o2_ref))]
    lses = [token_order(ref, l_s.at[gi]) for gi, ref in enumerate((l0_ref, l1_ref, l2_ref))]
    mx = jnp.maximum(jnp.maximum(lses[0], lses[1]), lses[2])
    es = [jnp.exp(v - mx) for v in lses]
    den = es[0] + es[1] + es[2]
    er = lax.broadcasted_iota(I32, (LANES, ATT_W), 0)
    ec = lax.broadcasted_iota(I32, (LANES, ATT_W), 1)
    expand = jnp.where(ec // ATT_HEAD_DIM == er, 1.0, 0.0).astype(BF16)
    o = jnp.zeros((ts, ATT_W), F32)
    for e, og in zip(es, group_o):
        w = e / den
        w_hi = w.astype(BF16)
        w_lo = (w - w_hi.astype(F32)).astype(BF16)
        wx = (jnp.dot(w_hi, expand, preferred_element_type=F32)
              + jnp.dot(w_lo, expand, preferred_element_type=F32))
        o = o + wx * og
    y_b = jnp.dot(o.astype(BF16), watt_ref[...], preferred_element_type=F32)
    ga = ga_ref[...].astype(F32)
    gb = gb_ref[...].astype(F32)
    merged = _sigmoid(ga) * ya_ref[...].astype(F32) + _sigmoid(gb) * y_b
    hmix = jnp.dot(merged.astype(BF16), wout_ref[...], preferred_element_type=F32)
    x1 = _layer_norm(ALPHA * x_ref[...] + hmix, g_ref[...], b_ref[...])
    x1_ref[...] = x1
    _to_row_tiles(x1t_ref, x1, ts)

    wr = wr_ref[...]
    wr_hi = wr.astype(BF16)
    wr_lo = (wr - wr_hi.astype(F32)).astype(BF16)
    x_hi = x1.astype(BF16)
    x_lo = (x1 - x_hi.astype(F32)).astype(BF16)
    logits = (jnp.dot(x_hi, wr_hi, preferred_element_type=F32)
              + jnp.dot(x_lo, wr_hi, preferred_element_type=F32)
              + jnp.dot(x_hi, wr_lo, preferred_element_type=F32)) + br_ref[...]

    el = lax.broadcasted_iota(I32, (ts, N_EXPERTS), 1)
    work = logits
    vals, idxs, hots = [], [], []
    for _ in range(TOP_K):
        m = jnp.max(work, axis=-1, keepdims=True)
        idx = jnp.min(jnp.where(work == m, el, N_EXPERTS), axis=-1, keepdims=True)
        hot = el == idx
        vals.append(m)
        idxs.append(idx)
        hots.append(hot)
        work = jnp.where(hot, NEG_BIG, work)
    exps = [jnp.exp(v - vals[0]) for v in vals]
    gden = exps[0] + exps[1] + exps[2] + exps[3]

    cnt = jnp.zeros((ts, N_EXPERTS), F32)
    for hot in hots:
        cnt = cnt + jnp.where(hot, 1.0, 0.0)
    tr = lax.broadcasted_iota(I32, (ts, ts), 0)
    tc = lax.broadcasted_iota(I32, (ts, ts), 1)
    tri = jnp.where(tc < tr, 1.0, 0.0).astype(BF16)
    before = jnp.dot(tri, cnt.astype(BF16), preferred_element_type=F32) + carry_ref[...]
    carry_ref[...] = carry_ref[...] + jnp.sum(cnt, axis=0, keepdims=True)
    cnt_ref[...] = carry_ref[...]

    lane = lax.broadcasted_iota(I32, (ts, LANES), 1)
    route = jnp.zeros((ts, LANES), I32)
    for kk in range(TOP_K):
        rank = jnp.sum(jnp.where(hots[kk], before, 0.0), axis=-1, keepdims=True).astype(I32)
        gate_bits = pltpu.bitcast(exps[kk] / gden, I32)
        route = jnp.where(lane == kk, idxs[kk], route)
        route = jnp.where(lane == TOP_K + kk, rank, route)
        route = jnp.where(lane == 2 * TOP_K + kk, gate_bits, route)
    route_ref[...] = route


def _merge(outs, lses, y_a, zd, x2d, seq, li, w_att_out, w_out, ln_g, ln_b, w_router, b_router):
    t = x2d.shape[0]
    ts = min(512, seq)
    ns = seq // ts
    rows = lambda v: v.reshape(v.shape[0], 1, v.shape[-1])
    tile = lambda w, c=0: pl.BlockSpec((ts, w), lambda i: (i, c))

    def by_residue(a):
        dil, w = a.shape[1], a.shape[3]
        return pl.BlockSpec((1, dil, ts // dil, w), lambda i: (i // ns, 0, i % ns, 0))

    return pl.pallas_call(
        _merge_body,
        grid=(t // ts,),
        in_specs=[by_residue(a) for a in (*outs, *lses)]
                 + [tile(D_MODEL), tile(D_MODEL, 2 * D_RNN // D_MODEL),
                    tile(D_MODEL, 2 * D_RNN // D_MODEL + 1), tile(D_MODEL),
                    _layer_spec(li, (ATT_W, D_MODEL)), _layer_spec(li, (D_MODEL, D_MODEL)),
                    _layer_spec(li, (1, D_MODEL)), _layer_spec(li, (1, D_MODEL)),
                    _layer_spec(li, (D_MODEL, N_EXPERTS)), _layer_spec(li, (1, N_EXPERTS))],
        out_specs=[tile(D_MODEL), pl.BlockSpec((ts * ROW_TILES, LANES), lambda i: (i, 0)),
                   tile(LANES), pl.BlockSpec((1, N_EXPERTS), lambda i: (0, 0))],
        out_shape=[jax.ShapeDtypeStruct((t, D_MODEL), F32),
                   jax.ShapeDtypeStruct((t * ROW_TILES, LANES), F32),
                   jax.ShapeDtypeStruct((t, LANES), I32),
                   jax.ShapeDtypeStruct((1, N_EXPERTS), F32)],
        scratch_shapes=[pltpu.VMEM((1, N_EXPERTS), F32),
                        pltpu.VMEM((N_GROUPS, ATT_W // LANES, ts, LANES), F32),
                        pltpu.VMEM((N_GROUPS, 1, ts, LANES), F32)],
        compiler_params=_cparams(("arbitrary",)),
        name="merge_ln1_router",
    )(*outs, *lses, y_a, zd, zd, x2d, w_att_out, w_out, rows(ln_g), rows(ln_b), w_router,
      rows(b_router))


INVERT_UNROLL = 32
ROW_BUFS = 4


def _build_row_order(dest_ref, fill_ref, n_ref, inv_ref):
    def zero_range(lo, hi):
        def z(row, carry):
            inv_ref[row] = 0
            return carry
        lax.fori_loop(lo, hi, z, 0)

    def per_expert(e, carry):
        zero_range(fill_ref[2 * e], fill_ref[2 * e + 1])
        return carry
    lax.fori_loop(0, N_EXPERTS, per_expert, 0)
    zero_range(fill_ref[2 * N_EXPERTS] * MOE_ROWS, fill_ref[2 * N_EXPERTS + 1] * MOE_ROWS)

    def chunk(c, carry):
        base = c * INVERT_UNROLL
        tok0 = c * (INVERT_UNROLL // TOP_K)
        for u in range(INVERT_UNROLL):
            inv_ref[dest_ref[base + u]] = tok0 + u // TOP_K
        return carry
    lax.fori_loop(0, n_ref[0] // INVERT_UNROLL, chunk, 0)


def _expert_body(li, n_blk, be_ref, nvb_ref, first_ref, wslot_ref, nxt_ref, dest_ref, fill_ref,
                 n_ref, x1t_hbm, wg_hbm, wu_hbm, wd_hbm, bg_ref, bu_ref, bd_ref, y_ref,
                 inv_s, wst, wg_s, wu_s, wd_s, *rest):
    j = pl.program_id(0)
    nvb = nvb_ref[0]
    weights = (wg_hbm, wu_hbm, wd_hbm)
    bufs = rest[:ROW_BUFS]
    sem, wsem = rest[ROW_BUFS:]
    ahead = ROW_BUFS - 1

    def issue(blk, k):
        base = jnp.minimum(blk, n_blk - 1) * MOE_ROWS
        for r in range(MOE_ROWS):
            tok = inv_s[base + r]
            src = x1t_hbm.at[pl.ds(pl.multiple_of(tok * ROW_TILES, ROW_TILES), ROW_TILES)]
            pltpu.make_async_copy(src, bufs[k].at[pl.ds(r * ROW_TILES, ROW_TILES)],
                                  sem.at[k]).start()

    def wait_rows(k):
        pltpu.make_async_copy(bufs[k], bufs[k], sem.at[k]).wait()

    def weight_copies(e, slot):
        return [pltpu.make_async_copy(w.at[li, e], wst.at[slot, k], wsem.at[slot])
                for k, w in enumerate(weights)]

    @pl.when(j == 0)
    def _():
        for cp in weight_copies(be_ref[0], 0):
            cp.start()
        _build_row_order(dest_ref, fill_ref, n_ref, inv_s)
        for b in range(ahead):
            issue(b, b)

    @pl.when(first_ref[j] == 1)
    def _():
        slot = wslot_ref[j]
        for cp in weight_copies(0, slot):
            cp.wait()
        wg_s[...] = wst[slot, 0].astype(BF16)
        wu_s[...] = wst[slot, 1].astype(BF16)
        wd_s[...] = wst[slot, 2].astype(BF16)

        @pl.when(nxt_ref[j] >= 0)
        def _():
            for cp in weight_copies(nxt_ref[j], 1 - slot):
                cp.start()

    def compute(k):
        issue(j + ahead, (k + ahead) % ROW_BUFS)
        wait_rows(k)
        x = _from_row_tiles(bufs[k], MOE_ROWS).astype(BF16)
        g = jnp.dot(x, wg_s[...], preferred_element_type=F32) + bg_ref[...]
        up = jnp.dot(x, wu_s[...], preferred_element_type=F32) + bu_ref[...]
        g = jnp.minimum(g, SWIGLU_LIMIT)
        up = jnp.clip(up, -SWIGLU_LIMIT, SWIGLU_LIMIT)
        hdn = (up + 1.0) * (g * _sigmoid(SWIGLU_ALPHA * g))
        y = jnp.dot(hdn.astype(BF16), wd_s[...], preferred_element_type=F32) + bd_ref[...]
        _to_row_tiles(y_ref, y, MOE_ROWS)

    for k in range(ROW_BUFS):
        @pl.when((j < nvb) & (j % ROW_BUFS == k))
        def _(k=k):
            compute(k)

        @pl.when((j >= nvb) & (j < nvb + ahead) & (j % ROW_BUFS == k))
        def _(k=k):
            wait_rows(k)

    @pl.when(j >= nvb)
    def _():
        y_ref[...] = jnp.zeros_like(y_ref)


def _experts(x1t, tables, li, w_gate, b_gate, w_up, b_up, w_down, b_down):
    n_rows = tables['n_rows']
    n_blk = n_rows // MOE_ROWS
    d = D_MODEL
    blk = (MOE_ROWS * ROW_TILES, LANES)
    prefetch = (tables['blk_expert'], tables['n_used'], tables['first'], tables['wslot'],
                tables['next_expert'], tables['dest'], tables['fill'], tables['n_assign'])
    bspec = pl.BlockSpec((None, None, 1, d),
                         lambda j, be, *_: (li, be[jnp.minimum(j, n_blk - 1)], 0, 0))
    bias = lambda v: v.reshape(v.shape[0], v.shape[1], 1, v.shape[2])
    hbm = pl.BlockSpec(memory_space=pl.ANY)
    rows = pltpu.VMEM(blk, F32)
    return pl.pallas_call(
        functools.partial(_expert_body, li, n_blk),
        grid_spec=pltpu.PrefetchScalarGridSpec(
            num_scalar_prefetch=len(prefetch), grid=(n_blk + ROW_BUFS - 1,),
            in_specs=[hbm, hbm, hbm, hbm, bspec, bspec, bspec],
            out_specs=pl.BlockSpec(blk, lambda j, *_: (jnp.minimum(j, n_blk - 1), 0)),
            scratch_shapes=[pltpu.SMEM((n_rows,), I32), pltpu.VMEM((2, 3, d, D_FF), F32),
                            pltpu.VMEM((d, D_FF), BF16), pltpu.VMEM((d, D_FF), BF16),
                            pltpu.VMEM((D_FF, d), BF16)] + [rows] * ROW_BUFS
                           + [pltpu.SemaphoreType.DMA((ROW_BUFS,)), pltpu.SemaphoreType.DMA((2,))]),
        out_shape=jax.ShapeDtypeStruct((n_rows * ROW_TILES, LANES), F32),
        compiler_params=_cparams(("arbitrary",)),
        name="moe_experts",
    )(*prefetch, x1t, w_gate, w_up, w_down, bias(b_gate), bias(b_up), bias(b_down))


COMBINE_BUFS = 3


def _combine_body(n_tiles, dest_ref, yb_hbm, route_ref, x1_ref, p_ref, wple_ref, wpg_ref, bpg_ref,
                  g2_ref, b2_ref, g3_ref, b3_ref, o_ref, buf0, buf1, buf2, sem):
    i = pl.program_id(0)
    ts = x1_ref.shape[0]
    bufs = (buf0, buf1, buf2)
    ahead = COMBINE_BUFS - 1

    def issue(tile, k):
        base = jnp.minimum(tile, n_tiles - 1) * (ts * TOP_K)
        for j in range(ts):
            for kk in range(TOP_K):
                row = dest_ref[base + j * TOP_K + kk]
                src = yb_hbm.at[pl.ds(pl.multiple_of(row * ROW_TILES, ROW_TILES), ROW_TILES)]
                pltpu.make_async_copy(src, bufs[k].at[kk, pl.ds(j * ROW_TILES, ROW_TILES)],
                                      sem.at[k]).start()

    def wait_rows(k):
        pltpu.make_async_copy(bufs[k], bufs[k], sem.at[k]).wait()

    @pl.when(i == 0)
    def _():
        for b in range(ahead):
            issue(b, b)

    def compute(k):
        issue(i + ahead, (k + ahead) % COMBINE_BUFS)
        wait_rows(k)
        route = route_ref[...]
        gates = [pltpu.bitcast(route[:, 2 * TOP_K + kk:2 * TOP_K + kk + 1], F32)
                 for kk in range(TOP_K)]
        parts = []
        for c in range(ROW_TILES):
            acc = jnp.zeros((ts, LANES), F32)
            for kk in range(TOP_K):
                acc = acc + gates[kk] * bufs[k][kk, pl.ds(c, ts, stride=ROW_TILES), :]
            parts.append(acc)
        y = jnp.concatenate(parts, axis=-1)
        x2 = _layer_norm(ALPHA * x1_ref[...] + y, g2_ref[...], b2_ref[...])
        emb = jnp.dot(p_ref[...].astype(BF16), wple_ref[...], preferred_element_type=F32)
        gate_in = jnp.dot(x2.astype(BF16), wpg_ref[...], preferred_element_type=F32) + bpg_ref[...]
        ple = emb * _sigmoid(gate_in)
        o_ref[...] = _layer_norm(ALPHA * x2 + ple, g3_ref[...], b3_ref[...])

    for k in range(COMBINE_BUFS):
        @pl.when((i < n_tiles) & (i % COMBINE_BUFS == k))
        def _(k=k):
            compute(k)

        @pl.when((i >= n_tiles) & (i % COMBINE_BUFS == k))
        def _(k=k):
            wait_rows(k)


def _combine(dest_flat, yb, route, x1, p3d, li, w_ple, w_ple_gate, b_ple_gate, ln2_g, ln2_b,
             ln3_g, ln3_b):
    t, d = x1.shape
    ts = min(256, t)
    n = t // ts
    rows = lambda v: v.reshape(v.shape[0], 1, v.shape[-1])
    tile = lambda w: pl.BlockSpec((ts, w), lambda i, dst: (jnp.minimum(i, n - 1), 0))
    buf = pltpu.VMEM((TOP_K, ts * ROW_TILES, LANES), F32)
    return pl.pallas_call(
        functools.partial(_combine_body, n),
        grid_spec=pltpu.PrefetchScalarGridSpec(
            num_scalar_prefetch=1, grid=(n + COMBINE_BUFS - 1,),
            in_specs=[pl.BlockSpec(memory_space=pl.ANY), tile(LANES), tile(d),
                      pl.BlockSpec((None, ts, PLE_DIM),
                                   lambda i, dst: (li, jnp.minimum(i, n - 1), 0)),
                      _layer_spec(li, (PLE_DIM, d)), _layer_spec(li, (d, d)),
                      _layer_spec(li, (1, d)), _layer_spec(li, (1, d)), _layer_spec(li, (1, d)),
                      _layer_spec(li, (1, d)), _layer_spec(li, (1, d))],
            out_specs=tile(d),
            scratch_shapes=[buf] * COMBINE_BUFS + [pltpu.SemaphoreType.DMA((COMBINE_BUFS,))]),
        out_shape=jax.ShapeDtypeStruct((t, d), F32),
        compiler_params=_cparams(("arbitrary",)),
        name="moe_combine_ple",
    )(dest_flat, yb, route, x1, p3d, w_ple, w_ple_gate, rows(b_ple_gate), rows(ln2_g), rows(ln2_b),
      rows(ln3_g), rows(ln3_b))


def _routing_tables(route, counts, t):
    counts = counts.reshape(N_EXPERTS).astype(I32)
    padded = (counts + MOE_ROWS - 1) // MOE_ROWS * MOE_ROWS
    pend = jnp.cumsum(padded)
    pstart = pend - padded
    top_e = route[:, 0:TOP_K]
    rank = route[:, TOP_K:2 * TOP_K]
    onehot = top_e[:, :, None] == jnp.arange(N_EXPERTS, dtype=I32)
    dest = (jnp.sum(jnp.where(onehot, pstart, 0), axis=-1) + rank).reshape(t * TOP_K)
    n_blk = t * TOP_K // MOE_ROWS + N_EXPERTS
    n_used = pend[N_EXPERTS - 1] // MOE_ROWS
    fill = jnp.concatenate([jnp.stack([pstart + counts, pend], axis=1).reshape(2 * N_EXPERTS),
                            jnp.stack([n_used, jnp.asarray(n_blk, I32)])]).astype(I32)
    n_tab = n_blk + ROW_BUFS - 1
    blk = jnp.arange(n_tab, dtype=I32)
    blk_expert = jnp.minimum(jnp.sum((pend[None, :] <= (blk * MOE_ROWS)[:, None]).astype(I32),
                                     axis=1), N_EXPERTS - 1)
    first = (blk < n_used) & ((blk == 0) | (blk_expert != jnp.roll(blk_expert, 1)))
    wslot = (jnp.cumsum(first.astype(I32)) - 1) % 2
    first_at = jnp.where(first, blk, n_tab)
    next_first = jnp.concatenate([lax.cummin(first_at[::-1])[::-1][1:],
                                  jnp.full((1,), n_tab, I32)])
    next_expert = jnp.where(next_first < n_tab,
                            blk_expert[jnp.minimum(next_first, n_tab - 1)], -1)
    return dict(dest=dest, fill=fill, blk_expert=blk_expert, n_used=n_used.reshape(1),
                first=first.astype(I32), wslot=wslot.astype(I32),
                next_expert=next_expert.astype(I32), n_assign=jnp.full((1,), t * TOP_K, I32),
                n_rows=n_blk * MOE_ROWS)


def _layer(x2d, p3d, batch, seq, li, w):
    t = batch * seq
    zd, *qkvs = _project(x2d, w['w_in'], li, batch, seq)
    y_a = _lru_branch(zd, batch, seq, li, w['conv_w'], w['conv_b'], w['w_rg'], w['b_rg'],
                      w['w_ig'], w['b_ig'], w['lru_lambda'], w['w_rnn_out'])
    outs, lses = [], []
    for gi in range(N_GROUPS):
        o, lse = _attention_group(qkvs[gi], gi)
        outs.append(o)
        lses.append(lse)
    x1, x1t, route, counts = _merge(outs, lses, y_a, zd, x2d, seq, li, w['w_att_out'], w['w_out'],
                                    w['ln1_g'], w['ln1_b'], w['w_router'], w['b_router'])
    tables = _routing_tables(route, counts, t)
    yb = _experts(x1t, tables, li, w['w_gate'], w['b_gate'], w['w_up'], w['b_up'], w['w_down'],
                  w['b_down'])
    return _combine(tables['dest'], yb, route, x1, p3d, li, w['w_ple'], w['w_ple_gate'],
                    w['b_ple_gate'],
                    w['ln2_g'], w['ln2_b'], w['ln3_g'], w['ln3_b'])


_WEIGHT_NAMES = ('w_in', 'conv_w', 'conv_b', 'w_rg', 'b_rg', 'w_ig', 'b_ig', 'lru_lambda',
                 'w_rnn_out', 'w_att_out', 'w_out', 'ln1_g', 'ln1_b', 'w_router', 'b_router',
                 'w_gate', 'b_gate', 'w_up', 'b_up', 'w_down', 'b_down', 'ln2_g', 'ln2_b',
                 'w_ple', 'w_ple_gate', 'b_ple_gate', 'ln3_g', 'ln3_b')
_BF16_WEIGHTS = ('w_in', 'w_rg', 'w_ig', 'w_rnn_out', 'w_att_out', 'w_out', 'w_ple', 'w_ple_gate')


def kernel(x, p, w_in, conv_w, conv_b, w_rg, b_rg, w_ig, b_ig, lru_lambda, w_rnn_out, w_att_out,
           w_out, ln1_g, ln1_b, w_router, b_router, w_gate, b_gate, w_up, b_up, w_down, b_down,
           ln2_g, ln2_b, w_ple, w_ple_gate, b_ple_gate, ln3_g, ln3_b):
    w = dict(zip(_WEIGHT_NAMES, (
        w_in, conv_w, conv_b, w_rg, b_rg, w_ig, b_ig, lru_lambda, w_rnn_out, w_att_out, w_out,
        ln1_g, ln1_b, w_router, b_router, w_gate, b_gate, w_up, b_up, w_down, b_down, ln2_g,
        ln2_b, w_ple, w_ple_gate, b_ple_gate, ln3_g, ln3_b)))
    for name in _BF16_WEIGHTS:
        w[name] = w[name].astype(BF16)
    batch, seq, d = x.shape
    depth = p.shape[0]
    x2d = x.reshape(batch * seq, d)
    p3d = p.reshape(depth, batch * seq, PLE_DIM)
    for li in range(depth):
        x2d = _layer(x2d, p3d, batch, seq, li, w)
    return x2d.reshape(batch, seq, d)
```

```python
import functools

import jax
import jax.numpy as jnp
from jax import lax
from jax.experimental import pallas as pl
from jax.experimental.pallas import tpu as pltpu

F32 = jnp.float32
BF16 = jnp.bfloat16
I32 = jnp.int32

D_MODEL = 1024
DEPTH = 2
D_RNN = 1024
RNN_BLOCKS = 4
RNN_BW = D_RNN // RNN_BLOCKS
CONV_W = 4
LRU_C = 8.0
ATT_GROUPS = ((128, 1), (512, 4), (2048, 16))
N_GROUPS = len(ATT_GROUPS)
ATT_HEADS = 8
ATT_HEAD_DIM = 64
ATT_W = ATT_HEADS * ATT_HEAD_DIM
ATT_BLOCK = 128
OFF_XR = 0
OFF_YR = OFF_XR + D_RNN
OFF_Q = OFF_YR + D_RNN
OFF_K = OFF_Q + N_GROUPS * ATT_W
OFF_V = OFF_K + N_GROUPS * ATT_W
OFF_GA = OFF_V + N_GROUPS * ATT_W
OFF_GB = OFF_GA + D_MODEL
N_IN = OFF_GB + D_MODEL
N_EXPERTS = 32
TOP_K = 4
D_FF = D_MODEL
SWIGLU_ALPHA = 1.702
SWIGLU_LIMIT = 7.0
PLE_DIM = 256
ALPHA = (2.0 * DEPTH) ** 0.25
LN_EPS = 1e-5

LANES = 128
SUBLANES = 8
ROW_TILES = D_MODEL // LANES
MOE_ROWS = 256
V7X_VMEM_BYTES = 64 * 1024 * 1024
VMEM_LIMIT = V7X_VMEM_BYTES - 8 * 1024 * 1024
NEG_BIG = -1e30

assert ROW_TILES == SUBLANES


def _cparams(sem):
    return pltpu.CompilerParams(dimension_semantics=sem, vmem_limit_bytes=VMEM_LIMIT)


def _layer_spec(li, shape):
    return pl.BlockSpec((None,) + tuple(shape), lambda *_: (li,) + (0,) * len(shape))


def _layer_norm(v, g, b):
    mu = jnp.mean(v, axis=-1, keepdims=True)
    c = v - mu
    var = jnp.mean(c * c, axis=-1, keepdims=True)
    return c * lax.rsqrt(var + LN_EPS) * g + b


def _sigmoid(v):
    return 0.5 * jnp.tanh(0.5 * v) + 0.5


def _to_row_tiles(ref, m, rows):
    for c in range(ROW_TILES):
        ref[pl.ds(c, rows, stride=ROW_TILES), :] = m[:, c * LANES:(c + 1) * LANES]


def _from_row_tiles(ref, rows):
    return jnp.concatenate([ref[pl.ds(c, rows, stride=ROW_TILES), :] for c in range(ROW_TILES)],
                           axis=-1)


PROJ_ROWS = 512
DENSE_W = 2 * D_RNN + 2 * D_MODEL


def _proj_body(x_ref, w_ref, zd_ref, q0_ref, q1_ref, q2_ref, res_s):
    x = x_ref[...].astype(BF16)
    tm = x.shape[0]
    lo = 2 * D_RNN
    zd_ref[:, 0:lo] = jnp.dot(x, w_ref[:, 0:lo], preferred_element_type=F32).astype(zd_ref.dtype)
    zd_ref[:, lo:] = jnp.dot(x, w_ref[:, OFF_GA:N_IN],
                             preferred_element_type=F32).astype(zd_ref.dtype)
    for gi, o_ref in enumerate((q0_ref, q1_ref, q2_ref)):
        dil = ATT_GROUPS[gi][1]
        for c, off in enumerate((OFF_Q, OFF_K, OFF_V)):
            col = off + gi * ATT_W
            res = jnp.dot(x, w_ref[:, col:col + ATT_W], preferred_element_type=F32)
            cols = slice(c * ATT_W, (c + 1) * ATT_W)
            if dil == 1:
                o_ref[0, 0, :, cols] = res.astype(o_ref.dtype)
            else:
                for lc in range(ATT_W // LANES):
                    res_s[lc] = res[:, lc * LANES:(lc + 1) * LANES]
                for r in range(dil):
                    part = [res_s[lc, pl.ds(r, tm // dil, stride=dil), :]
                            for lc in range(ATT_W // LANES)]
                    o_ref[0, r, :, cols] = jnp.concatenate(part, axis=-1).astype(o_ref.dtype)


def _project(x2d, w_in, li, batch, seq):
    d = x2d.shape[1]
    tm = min(PROJ_ROWS, seq)
    ns = seq // tm
    dils = [dil for _, dil in ATT_GROUPS]
    return pl.pallas_call(
        _proj_body,
        grid=(batch, ns),
        in_specs=[pl.BlockSpec((tm, d), lambda b, s: (b * ns + s, 0)),
                  pl.BlockSpec((None, d, N_IN), lambda b, s: (li, 0, 0),
                               pipeline_mode=pl.Buffered(1))],
        out_specs=[pl.BlockSpec((tm, DENSE_W), lambda b, s: (b * ns + s, 0))]
                  + [pl.BlockSpec((1, dil, tm // dil, 3 * ATT_W), lambda b, s: (b, 0, s, 0))
                     for dil in dils],
        out_shape=[jax.ShapeDtypeStruct((batch * seq, DENSE_W), BF16)]
                  + [jax.ShapeDtypeStruct((batch, dil, seq // dil, 3 * ATT_W), BF16)
                     for dil in dils],
        scratch_shapes=[pltpu.VMEM((ATT_W // LANES, tm, LANES), F32)],
        compiler_params=_cparams(("parallel", "parallel")),
        name="in_proj",
    )(x2d, w_in)


def _lru_body(xr_ref, yr_ref, cw_ref, cb_ref, wrg_ref, brg_ref, wig_ref, big_ref, lam_ref,
              wout_ref, o_ref, tail_ref, h_ref):
    s = pl.program_id(1)

    @pl.when(s == 0)
    def _():
        tail_ref[...] = jnp.zeros_like(tail_ref)
        h_ref[...] = jnp.zeros_like(h_ref)

    xr = xr_ref[...].astype(F32)
    ts, c = xr.shape
    groups = ts // SUBLANES
    sub = lax.broadcasted_iota(I32, (1, SUBLANES, 1), 1)
    xe3 = jnp.concatenate([tail_ref[...], xr], axis=0).reshape(groups + 1, SUBLANES, c)
    cw = cw_ref[...]
    xc3 = (cb_ref[...] + cw[CONV_W - 1:CONV_W] * xr).reshape(groups, SUBLANES, c)
    for k in range(1, CONV_W):
        rot = pltpu.roll(xe3, k, 1)
        xc3 = xc3 + cw[CONV_W - 1 - k:CONV_W - k] * jnp.where(sub >= k, rot[1:], rot[:-1])
    xc = xc3.reshape(ts, c)
    tail_ref[...] = xr[ts - 8:]

    xcb = xc.astype(BF16)

    def gate(w_ref, b_ref):
        parts = [jnp.dot(xcb[:, n * RNN_BW:(n + 1) * RNN_BW], w_ref[n],
                         preferred_element_type=F32) for n in range(RNN_BLOCKS)]
        return _sigmoid(jnp.concatenate(parts, axis=-1) + b_ref[...])

    r = gate(wrg_ref, brg_ref)
    i = gate(wig_ref, big_ref)
    nlam = -lam_ref[...]
    softplus = jnp.maximum(nlam, 0.0) + jnp.log1p(jnp.exp(-jnp.abs(nlam)))
    log_a = (-LRU_C) * r * softplus
    a = jnp.exp(log_a)
    bx = jnp.sqrt(jnp.tanh(-log_a) * (1.0 + a * a)) * (i * xc)

    a3 = a.reshape(groups, SUBLANES, c)
    b3 = bx.reshape(groups, SUBLANES, c)
    k = 1
    while k < SUBLANES:
        keep = sub >= k
        a_prev = pltpu.roll(a3, k, 1)
        b_prev = pltpu.roll(b3, k, 1)
        b3 = jnp.where(keep, a3 * b_prev, 0.0) + b3
        a3 = jnp.where(keep, a3 * a_prev, a3)
        k *= 2
    carry = h_ref[0:1]
    hs = []
    for gidx in range(groups):
        hg = a3[gidx] * carry + b3[gidx]
        hs.append(hg)
        carry = hg[SUBLANES - 1:SUBLANES]
    h = jnp.concatenate(hs, axis=0)
    h_ref[0:1] = carry

    yr = yr_ref[...].astype(F32)
    gelu = 0.5 * yr * (1.0 + jnp.tanh(0.7978845608028654 * (yr + 0.044715 * (yr * yr * yr))))
    o_ref[...] = jnp.dot((gelu * h).astype(BF16), wout_ref[...],
                         preferred_element_type=F32).astype(o_ref.dtype)


def _lru_branch(z, batch, seq, li, conv_w, conv_b, w_rg, b_rg, w_ig, b_ig, lam, w_rnn_out):
    ts = min(512, seq)
    ns = seq // ts
    c = D_RNN
    rows = lambda v: v.reshape(v.shape[0], 1, c)
    return pl.pallas_call(
        _lru_body,
        grid=(batch, ns),
        in_specs=[pl.BlockSpec((ts, c), lambda b, s: (b * ns + s, OFF_XR // c)),
                  pl.BlockSpec((ts, c), lambda b, s: (b * ns + s, OFF_YR // c)),
                  _layer_spec(li, (CONV_W, c)), _layer_spec(li, (1, c)),
                  _layer_spec(li, (RNN_BLOCKS, RNN_BW, RNN_BW)), _layer_spec(li, (1, c)),
                  _layer_spec(li, (RNN_BLOCKS, RNN_BW, RNN_BW)), _layer_spec(li, (1, c)),
                  _layer_spec(li, (1, c)), _layer_spec(li, (c, D_MODEL))],
        out_specs=pl.BlockSpec((ts, D_MODEL), lambda b, s: (b * ns + s, 0)),
        out_shape=jax.ShapeDtypeStruct((batch * seq, D_MODEL), BF16),
        scratch_shapes=[pltpu.VMEM((8, c), F32), pltpu.VMEM((8, c), F32)],
        compiler_params=_cparams(("arbitrary", "arbitrary")),
        name="lru_branch",
    )(z, z, conv_w, rows(conv_b), w_rg, rows(b_rg), w_ig, rows(b_ig), rows(lam), w_rnn_out)


ATT_QROWS = 1024


def _attn_body(n_back, q_ref, kp_ref, kc_ref, vp_ref, vc_ref, o_ref, lse_ref, s_s, p_s):
    n = pl.program_id(2)
    blk = ATT_BLOCK
    nsub = q_ref.shape[2] // blk
    qi = lax.broadcasted_iota(I32, (blk, 2 * blk), 0)
    kj = lax.broadcasted_iota(I32, (blk, 2 * blk), 1)
    diff = blk + qi - kj
    in_window = (diff >= 0) & (diff <= n_back)
    lane = lax.broadcasted_iota(I32, (blk, LANES), 1)
    scale = jnp.asarray(ATT_HEAD_DIM ** -0.5, q_ref.dtype)
    low_half = lane < ATT_HEAD_DIM
    ones = jnp.ones((2 * blk, LANES), BF16)
    for sb in range(nsub):
        rows = slice(sb * blk, (sb + 1) * blk)
        q = q_ref[0, 0, rows, :] * scale
        if sb == 0:
            k = jnp.concatenate([kp_ref[0, 0], kc_ref[0, 0, rows, :]], axis=0)
            v = jnp.concatenate([vp_ref[0, 0], vc_ref[0, 0, rows, :]], axis=0)
            valid = in_window & ((kj >= blk) | (n > 0))
        else:
            k = kc_ref[0, 0, (sb - 1) * blk:(sb + 1) * blk, :]
            v = vc_ref[0, 0, (sb - 1) * blk:(sb + 1) * blk, :]
            valid = in_window
        for h in range(ATT_HEADS):
            tile = slice((h // 2) * LANES, (h // 2 + 1) * LANES)
            mine = low_half if h % 2 == 0 else ~low_half
            qh = jnp.where(mine, q[:, tile], jnp.zeros_like(q[:, tile]))
            s_s[h] = lax.dot_general(qh, k[:, tile], (((1,), (1,)), ((), ())),
                                     preferred_element_type=F32)
        m_tile = jnp.zeros((blk, LANES), F32)
        for h in range(ATT_HEADS):
            s = jnp.where(valid, s_s[h], NEG_BIG)
            m = jnp.max(s, axis=-1, keepdims=True)
            p_s[h] = jnp.exp(s - m).astype(BF16)
            m_tile = jnp.where(lane == h, m, m_tile)
        lse_tile = jnp.zeros((blk, LANES), F32)
        for pair in range(ATT_HEADS // 2):
            tile = slice(pair * LANES, (pair + 1) * LANES)
            o_pair, l_pair = [], []
            for h in (2 * pair, 2 * pair + 1):
                o_pair.append(jnp.dot(p_s[h], v[:, tile], preferred_element_type=F32))
                l_pair.append(jnp.dot(p_s[h], ones, preferred_element_type=F32))
                lse_tile = jnp.where(lane == h, m_tile + jnp.log(l_pair[-1]), lse_tile)
            o_ref[0, 0, rows, tile] = jnp.where(low_half, o_pair[0] / l_pair[0],
                                                o_pair[1] / l_pair[1]).astype(o_ref.dtype)
        lse_ref[0, 0, rows, :] = lse_tile


def _attention_group(qkv, gi):
    batch, dil, l, _ = qkv.shape
    n_back = ATT_GROUPS[gi][0] // dil
    qrows = min(ATT_QROWS, l)
    nsub = qrows // ATT_BLOCK
    cur = lambda c: pl.BlockSpec((1, 1, qrows, ATT_W), lambda b, r, n: (b, r, n, c))
    prev = lambda c: pl.BlockSpec((1, 1, ATT_BLOCK, ATT_W),
                                  lambda b, r, n: (b, r, jnp.maximum(n * nsub - 1, 0), c))
    return pl.pallas_call(
        functools.partial(_attn_body, n_back),
        grid=(batch, dil, l // qrows),
        in_specs=[cur(0), prev(1), cur(1), prev(2), cur(2)],
        out_specs=[pl.BlockSpec((1, 1, qrows, ATT_W), lambda b, r, n: (b, r, n, 0)),
                   pl.BlockSpec((1, 1, qrows, LANES), lambda b, r, n: (b, r, n, 0))],
        out_shape=[jax.ShapeDtypeStruct((batch, dil, l, ATT_W), BF16),
                   jax.ShapeDtypeStruct((batch, dil, l, LANES), F32)],
        scratch_shapes=[pltpu.VMEM((ATT_HEADS, ATT_BLOCK, 2 * ATT_BLOCK), F32),
                        pltpu.VMEM((ATT_HEADS, ATT_BLOCK, 2 * ATT_BLOCK), BF16)],
        compiler_params=_cparams(("parallel", "parallel", "arbitrary")),
        name=f"attn_g{gi}",
    )(qkv, qkv, qkv, qkv, qkv)


def _merge_body(o0_ref, o1_ref, o2_ref, l0_ref, l1_ref, l2_ref, ya_ref, ga_ref, gb_ref,
                x_ref, watt_ref, wout_ref, g_ref, b_ref, wr_ref, br_ref,
                x1_ref, x1t_ref, route_ref, cnt_ref, carry_ref, o_s, l_s):
    i = pl.program_id(0)

    @pl.when(i == 0)
    def _():
        carry_ref[...] = jnp.zeros_like(carry_ref)

    ts = x_ref.shape[0]

    def token_order(ref, scratch):
        dil = ref.shape[1]
        if dil == 1:
            return ref[0, 0].astype(F32)
        nl = ref.shape[3] // LANES
        for r in range(dil):
            v = ref[0, r].astype(F32)
            for lc in range(nl):
                scratch[lc, pl.ds(r, ts // dil, stride=dil), :] = v[:, lc * LANES:(lc + 1) * LANES]
        return jnp.concatenate([scratch[lc] for lc in range(nl)], axis=-1)

    group_o = [token_order(ref, o_s.at[gi]) for gi, ref in enumerate((o0_ref, o1_ref, o2_ref))]
    lses = [token_order(ref, l_s.at[gi]) for gi, ref in enumerate((l0_ref, l1_ref, l2_ref))]
    mx = jnp.maximum(jnp.maximum(lses[0], lses[1]), lses[2])
    es = [jnp.exp(v - mx) for v in lses]
    den = es[0] + es[1] + es[2]
    er = lax.broadcasted_iota(I32, (LANES, ATT_W), 0)
    ec = lax.broadcasted_iota(I32, (LANES, ATT_W), 1)
    expand = jnp.where(ec // ATT_HEAD_DIM == er, 1.0, 0.0).astype(BF16)
    o = jnp.zeros((ts, ATT_W), F32)
    for e, og in zip(es, group_o):
        w = e / den
        w_hi = w.astype(BF16)
        w_lo = (w - w_hi.astype(F32)).astype(BF16)
        wx = (jnp.dot(w_hi, expand, preferred_element_type=F32)
              + jnp.dot(w_lo, expand, preferred_element_type=F32))
        o = o + wx * og
    y_b = jnp.dot(o.astype(BF16), watt_ref[...], preferred_element_type=F32)
    ga = ga_ref[...].astype(F32)
    gb = gb_ref[...].astype(F32)
    merged = _sigmoid(ga) * ya_ref[...].astype(F32) + _sigmoid(gb) * y_b
    hmix = jnp.dot(merged.astype(BF16), wout_ref[...], preferred_element_type=F32)
    x1 = _layer_norm(ALPHA * x_ref[...] + hmix, g_ref[...], b_ref[...])
    x1_ref[...] = x1
    _to_row_tiles(x1t_ref, x1, ts)

    wr = wr_ref[...]
    wr_hi = wr.astype(BF16)
    wr_lo = (wr - wr_hi.astype(F32)).astype(BF16)
    x_hi = x1.astype(BF16)
    x_lo = (x1 - x_hi.astype(F32)).astype(BF16)
    logits = (jnp.dot(x_hi, wr_hi, preferred_element_type=F32)
              + jnp.dot(x_lo, wr_hi, preferred_element_type=F32)
              + jnp.dot(x_hi, wr_lo, preferred_element_type=F32)) + br_ref[...]

    el = lax.broadcasted_iota(I32, (ts, N_EXPERTS), 1)
    work = logits
    vals, idxs, hots = [], [], []
    for _ in range(TOP_K):
        m = jnp.max(work, axis=-1, keepdims=True)
        idx = jnp.min(jnp.where(work == m, el, N_EXPERTS), axis=-1, keepdims=True)
        hot = el == idx
        vals.append(m)
        idxs.append(idx)
        hots.append(hot)
        work = jnp.where(hot, NEG_BIG, work)
    exps = [jnp.exp(v - vals[0]) for v in vals]
    gden = exps[0] + exps[1] + exps[2] + exps[3]

    cnt = jnp.zeros((ts, N_EXPERTS), F32)
    for hot in hots:
        cnt = cnt + jnp.where(hot, 1.0, 0.0)
    tr = lax.broadcasted_iota(I32, (ts, ts), 0)
    tc = lax.broadcasted_iota(I32, (ts, ts), 1)
    tri = jnp.where(tc < tr, 1.0, 0.0).astype(BF16)
    before = jnp.dot(tri, cnt.astype(BF16), preferred_element_type=F32) + carry_ref[...]
    carry_ref[...] = carry_ref[...] + jnp.sum(cnt, axis=0, keepdims=True)
    cnt_ref[...] = carry_ref[...]

    lane = lax.broadcasted_iota(I32, (ts, LANES), 1)
    route = jnp.zeros((ts, LANES), I32)
    for kk in range(TOP_K):
        rank = jnp.sum(jnp.where(hots[kk], before, 0.0), axis=-1, keepdims=True).astype(I32)
        gate_bits = pltpu.bitcast(exps[kk] / gden, I32)
        route = jnp.where(lane == kk, idxs[kk], route)
        route = jnp.where(lane == TOP_K + kk, rank, route)
        route = jnp.where(lane == 2 * TOP_K + kk, gate_bits, route)
    route_ref[...] = route


def _merge(outs, lses, y_a, zd, x2d, seq, li, w_att_out, w_out, ln_g, ln_b, w_router, b_router):
    t = x2d.shape[0]
    ts = min(512, seq)
    ns = seq // ts
    rows = lambda v: v.reshape(v.shape[0], 1, v.shape[-1])
    tile = lambda w, c=0: pl.BlockSpec((ts, w), lambda i: (i, c))

    def by_residue(a):
        dil, w = a.shape[1], a.shape[3]
        return pl.BlockSpec((1, dil, ts // dil, w), lambda i: (i // ns, 0, i % ns, 0))

    return pl.pallas_call(
        _merge_body,
        grid=(t // ts,),
        in_specs=[by_residue(a) for a in (*outs, *lses)]
                 + [tile(D_MODEL), tile(D_MODEL, 2 * D_RNN // D_MODEL),
                    tile(D_MODEL, 2 * D_RNN // D_MODEL + 1), tile(D_MODEL),
                    _layer_spec(li, (ATT_W, D_MODEL)), _layer_spec(li, (D_MODEL, D_MODEL)),
                    _layer_spec(li, (1, D_MODEL)), _layer_spec(li, (1, D_MODEL)),
                    _layer_spec(li, (D_MODEL, N_EXPERTS)), _layer_spec(li, (1, N_EXPERTS))],
        out_specs=[tile(D_MODEL), pl.BlockSpec((ts * ROW_TILES, LANES), lambda i: (i, 0)),
                   tile(LANES), pl.BlockSpec((1, N_EXPERTS), lambda i: (0, 0))],
        out_shape=[jax.ShapeDtypeStruct((t, D_MODEL), F32),
                   jax.ShapeDtypeStruct((t * ROW_TILES, LANES), F32),
                   jax.ShapeDtypeStruct((t, LANES), I32),
                   jax.ShapeDtypeStruct((1, N_EXPERTS), F32)],
        scratch_shapes=[pltpu.VMEM((1, N_EXPERTS), F32),
                        pltpu.VMEM((N_GROUPS, ATT_W // LANES, ts, LANES), F32),
                        pltpu.VMEM((N_GROUPS, 1, ts, LANES), F32)],
        compiler_params=_cparams(("arbitrary",)),
        name="merge_ln1_router",
    )(*outs, *lses, y_a, zd, zd, x2d, w_att_out, w_out, rows(ln_g), rows(ln_b), w_router,
      rows(b_router))


INVERT_UNROLL = 32
ROW_BUFS = 4


def _build_row_order(dest_ref, fill_ref, n_ref, inv_ref):
    def zero_range(lo, hi):
        def z(row, carry):
            inv_ref[row] = 0
            return carry
        lax.fori_loop(lo, hi, z, 0)

    def per_expert(e, carry):
        zero_range(fill_ref[2 * e], fill_ref[2 * e + 1])
        return carry
    lax.fori_loop(0, N_EXPERTS, per_expert, 0)
    zero_range(fill_ref[2 * N_EXPERTS] * MOE_ROWS, fill_ref[2 * N_EXPERTS + 1] * MOE_ROWS)

    def chunk(c, carry):
        base = c * INVERT_UNROLL
        tok0 = c * (INVERT_UNROLL // TOP_K)
        for u in range(INVERT_UNROLL):
            inv_ref[dest_ref[base + u]] = tok0 + u // TOP_K
        return carry
    lax.fori_loop(0, n_ref[0] // INVERT_UNROLL, chunk, 0)


def _expert_body(li, n_blk, be_ref, nvb_ref, first_ref, wslot_ref, nxt_ref, dest_ref, fill_ref,
                 n_ref, x1t_hbm, wg_hbm, wu_hbm, wd_hbm, bg_ref, bu_ref, bd_ref, y_ref,
                 inv_s, wst, wg_s, wu_s, wd_s, *rest):
    j = pl.program_id(0)
    nvb = nvb_ref[0]
    weights = (wg_hbm, wu_hbm, wd_hbm)
    bufs = rest[:ROW_BUFS]
    sem, wsem = rest[ROW_BUFS:]
    ahead = ROW_BUFS - 1

    def issue(blk, k):
        base = jnp.minimum(blk, n_blk - 1) * MOE_ROWS
        for r in range(MOE_ROWS):
            tok = inv_s[base + r]
            src = x1t_hbm.at[pl.ds(pl.multiple_of(tok * ROW_TILES, ROW_TILES), ROW_TILES)]
            pltpu.make_async_copy(src, bufs[k].at[pl.ds(r * ROW_TILES, ROW_TILES)],
                                  sem.at[k]).start()

    def wait_rows(k):
        pltpu.make_async_copy(bufs[k], bufs[k], sem.at[k]).wait()

    def weight_copies(e, slot):
        return [pltpu.make_async_copy(w.at[li, e], wst.at[slot, k], wsem.at[slot])
                for k, w in enumerate(weights)]

    @pl.when(j == 0)
    def _():
        for cp in weight_copies(be_ref[0], 0):
            cp.start()
        _build_row_order(dest_ref, fill_ref, n_ref, inv_s)
        for b in range(ahead):
            issue(b, b)

    @pl.when(first_ref[j] == 1)
    def _():
        slot = wslot_ref[j]
        for cp in weight_copies(0, slot):
            cp.wait()
        wg_s[...] = wst[slot, 0].astype(BF16)
        wu_s[...] = wst[slot, 1].astype(BF16)
        wd_s[...] = wst[slot, 2].astype(BF16)

        @pl.when(nxt_ref[j] >= 0)
        def _():
            for cp in weight_copies(nxt_ref[j], 1 - slot):
                cp.start()

    def compute(k):
        issue(j + ahead, (k + ahead) % ROW_BUFS)
        wait_rows(k)
        x = _from_row_tiles(bufs[k], MOE_ROWS).astype(BF16)
        g = jnp.dot(x, wg_s[...], preferred_element_type=F32) + bg_ref[...]
        up = jnp.dot(x, wu_s[...], preferred_element_type=F32) + bu_ref[...]
        g = jnp.minimum(g, SWIGLU_LIMIT)
        up = jnp.clip(up, -SWIGLU_LIMIT, SWIGLU_LIMIT)
        hdn = (up + 1.0) * (g * _sigmoid(SWIGLU_ALPHA * g))
        y = jnp.dot(hdn.astype(BF16), wd_s[...], preferred_element_type=F32) + bd_ref[...]
        _to_row_tiles(y_ref, y, MOE_ROWS)

    for k in range(ROW_BUFS):
        @pl.when((j < nvb) & (j % ROW_BUFS == k))
        def _(k=k):
            compute(k)

        @pl.when((j >= nvb) & (j < nvb + ahead) & (j % ROW_BUFS == k))
        def _(k=k):
            wait_rows(k)

    @pl.when(j >= nvb)
    def _():
        y_ref[...] = jnp.zeros_like(y_ref)


def _experts(x1t, tables, li, w_gate, b_gate, w_up, b_up, w_down, b_down):
    n_rows = tables['n_rows']
    n_blk = n_rows // MOE_ROWS
    d = D_MODEL
    blk = (MOE_ROWS * ROW_TILES, LANES)
    prefetch = (tables['blk_expert'], tables['n_used'], tables['first'], tables['wslot'],
                tables['next_expert'], tables['dest'], tables['fill'], tables['n_assign'])
    bspec = pl.BlockSpec((None, None, 1, d),
                         lambda j, be, *_: (li, be[jnp.minimum(j, n_blk - 1)], 0, 0))
    bias = lambda v: v.reshape(v.shape[0], v.shape[1], 1, v.shape[2])
    hbm = pl.BlockSpec(memory_space=pl.ANY)
    rows = pltpu.VMEM(blk, F32)
    return pl.pallas_call(
        functools.partial(_expert_body, li, n_blk),
        grid_spec=pltpu.PrefetchScalarGridSpec(
            num_scalar_prefetch=len(prefetch), grid=(n_blk + ROW_BUFS - 1,),
            in_specs=[hbm, hbm, hbm, hbm, bspec, bspec, bspec],
            out_specs=pl.BlockSpec(blk, lambda j, *_: (jnp.minimum(j, n_blk - 1), 0)),
            scratch_shapes=[pltpu.SMEM((n_rows,), I32), pltpu.VMEM((2, 3, d, D_FF), F32),
                            pltpu.VMEM((d, D_FF), BF16), pltpu.VMEM((d, D_FF), BF16),
                            pltpu.VMEM((D_FF, d), BF16)] + [rows] * ROW_BUFS
                           + [pltpu.SemaphoreType.DMA((ROW_BUFS,)), pltpu.SemaphoreType.DMA((2,))]),
        out_shape=jax.ShapeDtypeStruct((n_rows * ROW_TILES, LANES), F32),
        compiler_params=_cparams(("arbitrary",)),
        name="moe_experts",
    )(*prefetch, x1t, w_gate, w_up, w_down, bias(b_gate), bias(b_up), bias(b_down))


COMBINE_BUFS = 3


def _combine_body(n_tiles, dest_ref, yb_hbm, route_ref, x1_ref, p_ref, wple_ref, wpg_ref, bpg_ref,
                  g2_ref, b2_ref, g3_ref, b3_ref, o_ref, buf0, buf1, buf2, sem):
    i = pl.program_id(0)
    ts = x1_ref.shape[0]
    bufs = (buf0, buf1, buf2)
    ahead = COMBINE_BUFS - 1

    def issue(tile, k):
        base = jnp.minimum(tile, n_tiles - 1) * (ts * TOP_K)
        for j in range(ts):
            for kk in range(TOP_K):
                row = dest_ref[base + j * TOP_K + kk]
                src = yb_hbm.at[pl.ds(pl.multiple_of(row * ROW_TILES, ROW_TILES), ROW_TILES)]
                pltpu.make_async_copy(src, bufs[k].at[kk, pl.ds(j * ROW_TILES, ROW_TILES)],
                                      sem.at[k]).start()

    def wait_rows(k):
        pltpu.make_async_copy(bufs[k], bufs[k], sem.at[k]).wait()

    @pl.when(i == 0)
    def _():
        for b in range(ahead):
            issue(b, b)

    def compute(k):
        issue(i + ahead, (k + ahead) % COMBINE_BUFS)
        wait_rows(k)
        route = route_ref[...]
        gates = [pltpu.bitcast(route[:, 2 * TOP_K + kk:2 * TOP_K + kk + 1], F32)
                 for kk in range(TOP_K)]
        parts = []
        for c in range(ROW_TILES):
            acc = jnp.zeros((ts, LANES), F32)
            for kk in range(TOP_K):
                acc = acc + gates[kk] * bufs[k][kk, pl.ds(c, ts, stride=ROW_TILES), :]
            parts.append(acc)
        y = jnp.concatenate(parts, axis=-1)
        x2 = _layer_norm(ALPHA * x1_ref[...] + y, g2_ref[...], b2_ref[...])
        emb = jnp.dot(p_ref[...].astype(BF16), wple_ref[...], preferred_element_type=F32)
        gate_in = jnp.dot(x2.astype(BF16), wpg_ref[...], preferred_element_type=F32) + bpg_ref[...]
        ple = emb * _sigmoid(gate_in)
        o_ref[...] = _layer_norm(ALPHA * x2 + ple, g3_ref[...], b3_ref[...])

    for k in range(COMBINE_BUFS):
        @pl.when((i < n_tiles) & (i % COMBINE_BUFS == k))
        def _(k=k):
            compute(k)

        @pl.when((i >= n_tiles) & (i % COMBINE_BUFS == k))
        def _(k=k):
            wait_rows(k)


def _combine(dest_flat, yb, route, x1, p3d, li, w_ple, w_ple_gate, b_ple_gate, ln2_g, ln2_b,
             ln3_g, ln3_b):
    t, d = x1.shape
    ts = min(256, t)
    n = t // ts
    rows = lambda v: v.reshape(v.shape[0], 1, v.shape[-1])
    tile = lambda w: pl.BlockSpec((ts, w), lambda i, dst: (jnp.minimum(i, n - 1), 0))
    buf = pltpu.VMEM((TOP_K, ts * ROW_TILES, LANES), F32)
    return pl.pallas_call(
        functools.partial(_combine_body, n),
        grid_spec=pltpu.PrefetchScalarGridSpec(
            num_scalar_prefetch=1, grid=(n + COMBINE_BUFS - 1,),
            in_specs=[pl.BlockSpec(memory_space=pl.ANY), tile(LANES), tile(d),
                      pl.BlockSpec((None, ts, PLE_DIM),
                                   lambda i, dst: (li, jnp.minimum(i, n - 1), 0)),
                      _layer_spec(li, (PLE_DIM, d)), _layer_spec(li, (d, d)),
                      _layer_spec(li, (1, d)), _layer_spec(li, (1, d)), _layer_spec(li, (1, d)),
                      _layer_spec(li, (1, d)), _layer_spec(li, (1, d))],
            out_specs=tile(d),
            scratch_shapes=[buf] * COMBINE_BUFS + [pltpu.SemaphoreType.DMA((COMBINE_BUFS,))]),
        out_shape=jax.ShapeDtypeStruct((t, d), F32),
        compiler_params=_cparams(("arbitrary",)),
        name="moe_combine_ple",
    )(dest_flat, yb, route, x1, p3d, w_ple, w_ple_gate, rows(b_ple_gate), rows(ln2_g), rows(ln2_b),
      rows(ln3_g), rows(ln3_b))


def _routing_tables(route, counts, t):
    counts = counts.reshape(N_EXPERTS).astype(I32)
    padded = (counts + MOE_ROWS - 1) // MOE_ROWS * MOE_ROWS
    pend = jnp.cumsum(padded)
    pstart = pend - padded
    top_e = route[:, 0:TOP_K]
    rank = route[:, TOP_K:2 * TOP_K]
    onehot = top_e[:, :, None] == jnp.arange(N_EXPERTS, dtype=I32)
    dest = (jnp.sum(jnp.where(onehot, pstart, 0), axis=-1) + rank).reshape(t * TOP_K)
    n_blk = t * TOP_K // MOE_ROWS + N_EXPERTS
    n_used = pend[N_EXPERTS - 1] // MOE_ROWS
    fill = jnp.concatenate([jnp.stack([pstart + counts, pend], axis=1).reshape(2 * N_EXPERTS),
                            jnp.stack([n_used, jnp.asarray(n_blk, I32)])]).astype(I32)
    n_tab = n_blk + ROW_BUFS - 1
    blk = jnp.arange(n_tab, dtype=I32)
    blk_expert = jnp.minimum(jnp.sum((pend[None, :] <= (blk * MOE_ROWS)[:, None]).astype(I32),
                                     axis=1), N_EXPERTS - 1)
    first = (blk < n_used) & ((blk == 0) | (blk_expert != jnp.roll(blk_expert, 1)))
    wslot = (jnp.cumsum(first.astype(I32)) - 1) % 2
    first_at = jnp.where(first, blk, n_tab)
    next_first = jnp.concatenate([lax.cummin(first_at[::-1])[::-1][1:],
                                  jnp.full((1,), n_tab, I32)])
    next_expert = jnp.where(next_first < n_tab,
                            blk_expert[jnp.minimum(next_first, n_tab - 1)], -1)
    return dict(dest=dest, fill=fill, blk_expert=blk_expert, n_used=n_used.reshape(1),
                first=first.astype(I32), wslot=wslot.astype(I32),
                next_expert=next_expert.astype(I32), n_assign=jnp.full((1,), t * TOP_K, I32),
                n_rows=n_blk * MOE_ROWS)


def _layer(x2d, p3d, batch, seq, li, w):
    t = batch * seq
    zd, *qkvs = _project(x2d, w['w_in'], li, batch, seq)
    y_a = _lru_branch(zd, batch, seq, li, w['conv_w'], w['conv_b'], w['w_rg'], w['b_rg'],
                      w['w_ig'], w['b_ig'], w['lru_lambda'], w['w_rnn_out'])
    outs, lses = [], []
    for gi in range(N_GROUPS):
        o, lse = _attention_group(qkvs[gi], gi)
        outs.append(o)
        lses.append(lse)
    x1, x1t, route, counts = _merge(outs, lses, y_a, zd, x2d, seq, li, w['w_att_out'], w['w_out'],
                                    w['ln1_g'], w['ln1_b'], w['w_router'], w['b_router'])
    tables = _routing_tables(route, counts, t)
    yb = _experts(x1t, tables, li, w['w_gate'], w['b_gate'], w['w_up'], w['b_up'], w['w_down'],
                  w['b_down'])
    return _combine(tables['dest'], yb, route, x1, p3d, li, w['w_ple'], w['w_ple_gate'],
                    w['b_ple_gate'],
                    w['ln2_g'], w['ln2_b'], w['ln3_g'], w['ln3_b'])


_WEIGHT_NAMES = ('w_in', 'conv_w', 'conv_b', 'w_rg', 'b_rg', 'w_ig', 'b_ig', 'lru_lambda',
                 'w_rnn_out', 'w_att_out', 'w_out', 'ln1_g', 'ln1_b', 'w_router', 'b_router',
                 'w_gate', 'b_gate', 'w_up', 'b_up', 'w_down', 'b_down', 'ln2_g', 'ln2_b',
                 'w_ple', 'w_ple_gate', 'b_ple_gate', 'ln3_g', 'ln3_b')
_BF16_WEIGHTS = ('w_in', 'w_rg', 'w_ig', 'w_rnn_out', 'w_att_out', 'w_out', 'w_ple', 'w_ple_gate')


def kernel(x, p, w_in, conv_w, conv_b, w_rg, b_rg, w_ig, b_ig, lru_lambda, w_rnn_out, w_att_out,
           w_out, ln1_g, ln1_b, w_router, b_router, w_gate, b_gate, w_up, b_up, w_down, b_down,
           ln2_g, ln2_b, w_ple, w_ple_gate, b_ple_gate, ln3_g, ln3_b):
    w = dict(zip(_WEIGHT_NAMES, (
        w_in, conv_w, conv_b, w_rg, b_rg, w_ig, b_ig, lru_lambda, w_rnn_out, w_att_out, w_out,
        ln1_g, ln1_b, w_router, b_router, w_gate, b_gate, w_up, b_up, w_down, b_down, ln2_g,
        ln2_b, w_ple, w_ple_gate, b_ple_gate, ln3_g, ln3_b)))
    for name in _BF16_WEIGHTS:
        w[name] = w[name].astype(BF16)
    batch, seq, d = x.shape
    depth = p.shape[0]
    x2d = x.reshape(batch * seq, d)
    p3d = p.reshape(depth, batch * seq, PLE_DIM)
    for li in range(depth):
        x2d = _layer(x2d, p3d, batch, seq, li, w)
    return x2d.reshape(batch, seq, d)
```

```python
import functools

import jax
import jax.numpy as jnp
from jax import lax
from jax.experimental import pallas as pl
from jax.experimental.pallas import tpu as pltpu

F32 = jnp.float32
BF16 = jnp.bfloat16
I32 = jnp.int32

D_MODEL = 1024
DEPTH = 2
D_RNN = 1024
RNN_BLOCKS = 4
RNN_BW = D_RNN // RNN_BLOCKS
CONV_W = 4
LRU_C = 8.0
ATT_GROUPS = ((128, 1), (512, 4), (2048, 16))
N_GROUPS = len(ATT_GROUPS)
ATT_HEADS = 8
ATT_HEAD_DIM = 64
ATT_W = ATT_HEADS * ATT_HEAD_DIM
ATT_BLOCK = 128
OFF_XR = 0
OFF_YR = OFF_XR + D_RNN
OFF_Q = OFF_YR + D_RNN
OFF_K = OFF_Q + N_GROUPS * ATT_W
OFF_V = OFF_K + N_GROUPS * ATT_W
OFF_GA = OFF_V + N_GROUPS * ATT_W
OFF_GB = OFF_GA + D_MODEL
N_IN = OFF_GB + D_MODEL
N_EXPERTS = 32
TOP_K = 4
D_FF = D_MODEL
SWIGLU_ALPHA = 1.702
SWIGLU_LIMIT = 7.0
PLE_DIM = 256
ALPHA = (2.0 * DEPTH) ** 0.25
LN_EPS = 1e-5

LANES = 128
SUBLANES = 8
ROW_TILES = D_MODEL // LANES
MOE_ROWS = 256
V7X_VMEM_BYTES = 64 * 1024 * 1024
VMEM_LIMIT = V7X_VMEM_BYTES - 8 * 1024 * 1024
NEG_BIG = -1e30

assert ROW_TILES == SUBLANES


def _cparams(sem):
    return pltpu.CompilerParams(dimension_semantics=sem, vmem_limit_bytes=VMEM_LIMIT)


def _layer_spec(li, shape):
    return pl.BlockSpec((None,) + tuple(shape), lambda *_: (li,) + (0,) * len(shape))


def _layer_norm(v, g, b):
    mu = jnp.mean(v, axis=-1, keepdims=True)
    c = v - mu
    var = jnp.mean(c * c, axis=-1, keepdims=True)
    return c * lax.rsqrt(var + LN_EPS) * g + b


def _sigmoid(v):
    return 0.5 * jnp.tanh(0.5 * v) + 0.5


def _to_row_tiles(ref, m, rows):
    for c in range(ROW_TILES):
        ref[pl.ds(c, rows, stride=ROW_TILES), :] = m[:, c * LANES:(c + 1) * LANES]


def _from_row_tiles(ref, rows):
    return jnp.concatenate([ref[pl.ds(c, rows, stride=ROW_TILES), :] for c in range(ROW_TILES)],
                           axis=-1)


PROJ_ROWS = 512
DENSE_W = 2 * D_RNN + 2 * D_MODEL


def _proj_body(x_ref, w_ref, zd_ref, q0_ref, q1_ref, q2_ref, res_s):
    x = x_ref[...].astype(BF16)
    tm = x.shape[0]
    lo = 2 * D_RNN
    zd_ref[:, 0:lo] = jnp.dot(x, w_ref[:, 0:lo], preferred_element_type=F32).astype(zd_ref.dtype)
    zd_ref[:, lo:] = jnp.dot(x, w_ref[:, OFF_GA:N_IN],
                             preferred_element_type=F32).astype(zd_ref.dtype)
    for gi, o_ref in enumerate((q0_ref, q1_ref, q2_ref)):
        dil = ATT_GROUPS[gi][1]
        for c, off in enumerate((OFF_Q, OFF_K, OFF_V)):
            col = off + gi * ATT_W
            res = jnp.dot(x, w_ref[:, col:col + ATT_W], preferred_element_type=F32)
            cols = slice(c * ATT_W, (c + 1) * ATT_W)
            if dil == 1:
                o_ref[0, 0, :, cols] = res.astype(o_ref.dtype)
            else:
                for lc in range(ATT_W // LANES):
                    res_s[lc] = res[:, lc * LANES:(lc + 1) * LANES]
                for r in range(dil):
                    part = [res_s[lc, pl.ds(r, tm // dil, stride=dil), :]
                            for lc in range(ATT_W // LANES)]
                    o_ref[0, r, :, cols] = jnp.concatenate(part, axis=-1).astype(o_ref.dtype)


def _project(x2d, w_in, li, batch, seq):
    d = x2d.shape[1]
    tm = min(PROJ_ROWS, seq)
    ns = seq // tm
    dils = [dil for _, dil in ATT_GROUPS]
    return pl.pallas_call(
        _proj_body,
        grid=(batch, ns),
        in_specs=[pl.BlockSpec((tm, d), lambda b, s: (b * ns + s, 0)),
                  pl.BlockSpec((None, d, N_IN), lambda b, s: (li, 0, 0),
                               pipeline_mode=pl.Buffered(1))],
        out_specs=[pl.BlockSpec((tm, DENSE_W), lambda b, s: (b * ns + s, 0))]
                  + [pl.BlockSpec((1, dil, tm // dil, 3 * ATT_W), lambda b, s: (b, 0, s, 0))
                     for dil in dils],
        out_shape=[jax.ShapeDtypeStruct((batch * seq, DENSE_W), BF16)]
                  + [jax.ShapeDtypeStruct((batch, dil, seq // dil, 3 * ATT_W), BF16)
                     for dil in dils],
        scratch_shapes=[pltpu.VMEM((ATT_W // LANES, tm, LANES), F32)],
        compiler_params=_cparams(("parallel", "parallel")),
        name="in_proj",
    )(x2d, w_in)


def _lru_body(xr_ref, yr_ref, cw_ref, cb_ref, wrg_ref, brg_ref, wig_ref, big_ref, lam_ref,
              wout_ref, o_ref, tail_ref, h_ref):
    s = pl.program_id(1)

    @pl.when(s == 0)
    def _():
        tail_ref[...] = jnp.zeros_like(tail_ref)
        h_ref[...] = jnp.zeros_like(h_ref)

    xr = xr_ref[...].astype(F32)
    ts, c = xr.shape
    groups = ts // SUBLANES
    sub = lax.broadcasted_iota(I32, (1, SUBLANES, 1), 1)
    xe3 = jnp.concatenate([tail_ref[...], xr], axis=0).reshape(groups + 1, SUBLANES, c)
    cw = cw_ref[...]
    xc3 = (cb_ref[...] + cw[CONV_W - 1:CONV_W] * xr).reshape(groups, SUBLANES, c)
    for k in range(1, CONV_W):
        rot = pltpu.roll(xe3, k, 1)
        xc3 = xc3 + cw[CONV_W - 1 - k:CONV_W - k] * jnp.where(sub >= k, rot[1:], rot[:-1])
    xc = xc3.reshape(ts, c)
    tail_ref[...] = xr[ts - 8:]

    xcb = xc.astype(BF16)

    def gate(w_ref, b_ref):
        parts = [jnp.dot(xcb[:, n * RNN_BW:(n + 1) * RNN_BW], w_ref[n],
                         preferred_element_type=F32) for n in range(RNN_BLOCKS)]
        return _sigmoid(jnp.concatenate(parts, axis=-1) + b_ref[...])

    r = gate(wrg_ref, brg_ref)
    i = gate(wig_ref, big_ref)
    nlam = -lam_ref[...]
    softplus = jnp.maximum(nlam, 0.0) + jnp.log1p(jnp.exp(-jnp.abs(nlam)))
    log_a = (-LRU_C) * r * softplus
    a = jnp.exp(log_a)
    bx = jnp.sqrt(jnp.tanh(-log_a) * (1.0 + a * a)) * (i * xc)

    a3 = a.reshape(groups, SUBLANES, c)
    b3 = bx.reshape(groups, SUBLANES, c)
    k = 1
    while k < SUBLANES:
        keep = sub >= k
        a_prev = pltpu.roll(a3, k, 1)
        b_prev = pltpu.roll(b3, k, 1)
        b3 = jnp.where(keep, a3 * b_prev, 0.0) + b3
        a3 = jnp.where(keep, a3 * a_prev, a3)
        k *= 2
    carry = h_ref[0:1]
    hs = []
    for gidx in range(groups):
        hg = a3[gidx] * carry + b3[gidx]
        hs.append(hg)
        carry = hg[SUBLANES - 1:SUBLANES]
    h = jnp.concatenate(hs, axis=0)
    h_ref[0:1] = carry

    yr = yr_ref[...].astype(F32)
    gelu = 0.5 * yr * (1.0 + jnp.tanh(0.7978845608028654 * (yr + 0.044715 * (yr * yr * yr))))
    o_ref[...] = jnp.dot((gelu * h).astype(BF16), wout_ref[...],
                         preferred_element_type=F32).astype(o_ref.dtype)


def _lru_branch(z, batch, seq, li, conv_w, conv_b, w_rg, b_rg, w_ig, b_ig, lam, w_rnn_out):
    ts = min(512, seq)
    ns = seq // ts
    c = D_RNN
    rows = lambda v: v.reshape(v.shape[0], 1, c)
    return pl.pallas_call(
        _lru_body,
        grid=(batch, ns),
        in_specs=[pl.BlockSpec((ts, c), lambda b, s: (b * ns + s, OFF_XR // c)),
                  pl.BlockSpec((ts, c), lambda b, s: (b * ns + s, OFF_YR // c)),
                  _layer_spec(li, (CONV_W, c)), _layer_spec(li, (1, c)),
                  _layer_spec(li, (RNN_BLOCKS, RNN_BW, RNN_BW)), _layer_spec(li, (1, c)),
                  _layer_spec(li, (RNN_BLOCKS, RNN_BW, RNN_BW)), _layer_spec(li, (1, c)),
                  _layer_spec(li, (1, c)), _layer_spec(li, (c, D_MODEL))],
        out_specs=pl.BlockSpec((ts, D_MODEL), lambda b, s: (b * ns + s, 0)),
        out_shape=jax.ShapeDtypeStruct((batch * seq, D_MODEL), BF16),
        scratch_shapes=[pltpu.VMEM((8, c), F32), pltpu.VMEM((8, c), F32)],
        compiler_params=_cparams(("arbitrary", "arbitrary")),
        name="lru_branch",
    )(z, z, conv_w, rows(conv_b), w_rg, rows(b_rg), w_ig, rows(b_ig), rows(lam), w_rnn_out)


ATT_QROWS = 1024


def _attn_body(n_back, q_ref, kp_ref, kc_ref, vp_ref, vc_ref, o_ref, lse_ref, s_s, p_s):
    n = pl.program_id(2)
    blk = ATT_BLOCK
    nsub = q_ref.shape[2] // blk
    qi = lax.broadcasted_iota(I32, (blk, 2 * blk), 0)
    kj = lax.broadcasted_iota(I32, (blk, 2 * blk), 1)
    diff = blk + qi - kj
    in_window = (diff >= 0) & (diff <= n_back)
    lane = lax.broadcasted_iota(I32, (blk, LANES), 1)
    scale = jnp.asarray(ATT_HEAD_DIM ** -0.5, q_ref.dtype)
    low_half = lane < ATT_HEAD_DIM
    ones = jnp.ones((2 * blk, LANES), BF16)
    for sb in range(nsub):
        rows = slice(sb * blk, (sb + 1) * blk)
        q = q_ref[0, 0, rows, :] * scale
        if sb == 0:
            k = jnp.concatenate([kp_ref[0, 0], kc_ref[0, 0, rows, :]], axis=0)
            v = jnp.concatenate([vp_ref[0, 0], vc_ref[0, 0, rows, :]], axis=0)
            valid = in_window & ((kj >= blk) | (n > 0))
        else:
            k = kc_ref[0, 0, (sb - 1) * blk:(sb + 1) * blk, :]
            v = vc_ref[0, 0, (sb - 1) * blk:(sb + 1) * blk, :]
            valid = in_window
        for h in range(ATT_HEADS):
            tile = slice((h // 2) * LANES, (h // 2 + 1) * LANES)
            mine = low_half if h % 2 == 0 else ~low_half
            qh = jnp.where(mine, q[:, tile], jnp.zeros_like(q[:, tile]))
            s_s[h] = lax.dot_general(qh, k[:, tile], (((1,), (1,)), ((), ())),
                                     preferred_element_type=F32)
        m_tile = jnp.zeros((blk, LANES), F32)
        for h in range(ATT_HEADS):
            s = jnp.where(valid, s_s[h], NEG_BIG)
            m = jnp.max(s, axis=-1, keepdims=True)
            p_s[h] = jnp.exp(s - m).astype(BF16)
            m_tile = jnp.where(lane == h, m, m_tile)
        lse_tile = jnp.zeros((blk, LANES), F32)
        for pair in range(ATT_HEADS // 2):
            tile = slice(pair * LANES, (pair + 1) * LANES)
            o_pair, l_pair = [], []
            for h in (2 * pair, 2 * pair + 1):
                o_pair.append(jnp.dot(p_s[h], v[:, tile], preferred_element_type=F32))
                l_pair.append(jnp.dot(p_s[h], ones, preferred_element_type=F32))
                lse_tile = jnp.where(lane == h, m_tile + jnp.log(l_pair[-1]), lse_tile)
            o_ref[0, 0, rows, tile] = jnp.where(low_half, o_pair[0] / l_pair[0],
                                                o_pair[1] / l_pair[1]).astype(o_ref.dtype)
        lse_ref[0, 0, rows, :] = lse_tile


def _attention_group(qkv, gi):
    batch, dil, l, _ = qkv.shape
    n_back = ATT_GROUPS[gi][0] // dil
    qrows = min(ATT_QROWS, l)
    nsub = qrows // ATT_BLOCK
    cur = lambda c: pl.BlockSpec((1, 1, qrows, ATT_W), lambda b, r, n: (b, r, n, c))
    prev = lambda c: pl.BlockSpec((1, 1, ATT_BLOCK, ATT_W),
                                  lambda b, r, n: (b, r, jnp.maximum(n * nsub - 1, 0), c))
    return pl.pallas_call(
        functools.partial(_attn_body, n_back),
        grid=(batch, dil, l // qrows),
        in_specs=[cur(0), prev(1), cur(1), prev(2), cur(2)],
        out_specs=[pl.BlockSpec((1, 1, qrows, ATT_W), lambda b, r, n: (b, r, n, 0)),
                   pl.BlockSpec((1, 1, qrows, LANES), lambda b, r, n: (b, r, n, 0))],
        out_shape=[jax.ShapeDtypeStruct((batch, dil, l, ATT_W), BF16),
                   jax.ShapeDtypeStruct((batch, dil, l, LANES), F32)],
        scratch_shapes=[pltpu.VMEM((ATT_HEADS, ATT_BLOCK, 2 * ATT_BLOCK), F32),
                        pltpu.VMEM((ATT_HEADS, ATT_BLOCK, 2 * ATT_BLOCK), BF16)],
        compiler_params=_cparams(("parallel", "parallel", "arbitrary")),
        name=f"attn_g{gi}",
    )(qkv, qkv, qkv, qkv, qkv)


def _merge_body(o0_ref, o1_ref, o2_ref, l0_ref, l1_ref, l2_ref, ya_ref, ga_ref, gb_ref,
                x_ref, watt_ref, wout_ref, g_ref, b_ref, wr_ref, br_ref,
                x1_ref, x1t_ref, route_ref, cnt_ref, carry_ref, o_s, l_s):
    i = pl.program_id(0)

    @pl.when(i == 0)
    def _():
        carry_ref[...] = jnp.zeros_like(carry_ref)

    ts = x_ref.shape[0]

    def token_order(ref, scratch):
        dil = ref.shape[1]
        if dil == 1:
            return ref[0, 0].astype(F32)
        nl = ref.shape[3] // LANES
        for r in range(dil):
            v = ref[0, r].astype(F32)
            for lc in range(nl):
                scratch[lc, pl.ds(r, ts // dil, stride=dil), :] = v[:, lc * LANES:(lc + 1) * LANES]
        return jnp.concatenate([scratch[lc] for lc in range(nl)], axis=-1)

    group_o = [token_order(ref, o_s.at[gi]) for gi, ref in enumerate((o0_ref, o1_ref, o2_ref))]
    lses = [token_order(ref, l_s.at[gi]) for gi, ref in enumerate((l0_ref, l1_ref, l2_ref))]
    mx = jnp.maximum(jnp.maximum(lses[0], lses[1]), lses[2])
    es = [jnp.exp(v - mx) for v in lses]
    den = es[0] + es[1] + es[2]
    er = lax.broadcasted_iota(I32, (LANES, ATT_W), 0)
    ec = lax.broadcasted_iota(I32, (LANES, ATT_W), 1)
    expand = jnp.where(ec // ATT_HEAD_DIM == er, 1.0, 0.0).astype(BF16)
    o = jnp.zeros((ts, ATT_W), F32)
    for e, og in zip(es, group_o):
        w = e / den
        w_hi = w.astype(BF16)
        w_lo = (w - w_hi.astype(F32)).astype(BF16)
        wx = (jnp.dot(w_hi, expand, preferred_element_type=F32)
              + jnp.dot(w_lo, expand, preferred_element_type=F32))
        o = o + wx * og
    y_b = jnp.dot(o.astype(BF16), watt_ref[...], preferred_element_type=F32)
    ga = ga_ref[...].astype(F32)
    gb = gb_ref[...].astype(F32)
    merged = _sigmoid(ga) * ya_ref[...].astype(F32) + _sigmoid(gb) * y_b
    hmix = jnp.dot(merged.astype(BF16), wout_ref[...], preferred_element_type=F32)
    x1 = _layer_norm(ALPHA * x_ref[...] + hmix, g_ref[...], b_ref[...])
    x1_ref[...] = x1
    _to_row_tiles(x1t_ref, x1, ts)

    wr = wr_ref[...]
    wr_hi = wr.astype(BF16)
    wr_lo = (wr - wr_hi.astype(F32)).astype(BF16)
    x_hi = x1.astype(BF16)
    x_lo = (x1 - x_hi.astype(F32)).astype(BF16)
    logits = (jnp.dot(x_hi, wr_hi, preferred_element_type=F32)
              + jnp.dot(x_lo, wr_hi, preferred_element_type=F32)
              + jnp.dot(x_hi, wr_lo, preferred_element_type=F32)) + br_ref[...]

    el = lax.broadcasted_iota(I32, (ts, N_EXPERTS), 1)
    work = logits
    vals, idxs, hots = [], [], []
    for _ in range(TOP_K):
        m = jnp.max(work, axis=-1, keepdims=True)
        idx = jnp.min(jnp.where(work == m, el, N_EXPERTS), axis=-1, keepdims=True)
        hot = el == idx
        vals.append(m)
        idxs.append(idx)
        hots.append(hot)
        work = jnp.where(hot, NEG_BIG, work)
    exps = [jnp.exp(v - vals[0]) for v in vals]
    gden = exps[0] + exps[1] + exps[2] + exps[3]

    cnt = jnp.zeros((ts, N_EXPERTS), F32)
    for hot in hots:
        cnt = cnt + jnp.where(hot, 1.0, 0.0)
    tr = lax.broadcasted_iota(I32, (ts, ts), 0)
    tc = lax.broadcasted_iota(I32, (ts, ts), 1)
    tri = jnp.where(tc < tr, 1.0, 0.0).astype(BF16)
    before = jnp.dot(tri, cnt.astype(BF16), preferred_element_type=F32) + carry_ref[...]
    carry_ref[...] = carry_ref[...] + jnp.sum(cnt, axis=0, keepdims=True)
    cnt_ref[...] = carry_ref[...]

    lane = lax.broadcasted_iota(I32, (ts, LANES), 1)
    route = jnp.zeros((ts, LANES), I32)
    for kk in range(TOP_K):
        rank = jnp.sum(jnp.where(hots[kk], before, 0.0), axis=-1, keepdims=True).astype(I32)
        gate_bits = pltpu.bitcast(exps[kk] / gden, I32)
        route = jnp.where(lane == kk, idxs[kk], route)
        route = jnp.where(lane == TOP_K + kk, rank, route)
        route = jnp.where(lane == 2 * TOP_K + kk, gate_bits, route)
    route_ref[...] = route


def _merge(outs, lses, y_a, zd, x2d, seq, li, w_att_out, w_out, ln_g, ln_b, w_router, b_router):
    t = x2d.shape[0]
    ts = min(512, seq)
    ns = seq // ts
    rows = lambda v: v.reshape(v.shape[0], 1, v.shape[-1])
    tile = lambda w, c=0: pl.BlockSpec((ts, w), lambda i: (i, c))

    def by_residue(a):
        dil, w = a.shape[1], a.shape[3]
        return pl.BlockSpec((1, dil, ts // dil, w), lambda i: (i // ns, 0, i % ns, 0))

    return pl.pallas_call(
        _merge_body,
        grid=(t // ts,),
        in_specs=[by_residue(a) for a in (*outs, *lses)]
                 + [tile(D_MODEL), tile(D_MODEL, 2 * D_RNN // D_MODEL),
                    tile(D_MODEL, 2 * D_RNN // D_MODEL + 1), tile(D_MODEL),
                    _layer_spec(li, (ATT_W, D_MODEL)), _layer_spec(li, (D_MODEL, D_MODEL)),
                    _layer_spec(li, (1, D_MODEL)), _layer_spec(li, (1, D_MODEL)),
                    _layer_spec(li, (D_MODEL, N_EXPERTS)), _layer_spec(li, (1, N_EXPERTS))],
        out_specs=[tile(D_MODEL), pl.BlockSpec((ts * ROW_TILES, LANES), lambda i: (i, 0)),
                   tile(LANES), pl.BlockSpec((1, N_EXPERTS), lambda i: (0, 0))],
        out_shape=[jax.ShapeDtypeStruct((t, D_MODEL), F32),
                   jax.ShapeDtypeStruct((t * ROW_TILES, LANES), F32),
                   jax.ShapeDtypeStruct((t, LANES), I32),
                   jax.ShapeDtypeStruct((1, N_EXPERTS), F32)],
        scratch_shapes=[pltpu.VMEM((1, N_EXPERTS), F32),
                        pltpu.VMEM((N_GROUPS, ATT_W // LANES, ts, LANES), F32),
                        pltpu.VMEM((N_GROUPS, 1, ts, LANES), F32)],
        compiler_params=_cparams(("arbitrary",)),
        name="merge_ln1_router",
    )(*outs, *lses, y_a, zd, zd, x2d, w_att_out, w_out, rows(ln_g), rows(ln_b), w_router,
      rows(b_router))


INVERT_UNROLL = 64
ROW_BUFS = 5


def _build_row_order(dest_ref, fill_ref, n_ref, inv_ref):
    def zero_range(lo, hi):
        def z(row, carry):
            inv_ref[row] = 0
            return carry
        lax.fori_loop(lo, hi, z, 0)

    def per_expert(e, carry):
        zero_range(fill_ref[2 * e], fill_ref[2 * e + 1])
        return carry
    lax.fori_loop(0, N_EXPERTS, per_expert, 0)
    zero_range(fill_ref[2 * N_EXPERTS] * MOE_ROWS, fill_ref[2 * N_EXPERTS + 1] * MOE_ROWS)

    def chunk(c, carry):
        base = c * INVERT_UNROLL
        tok0 = c * (INVERT_UNROLL // TOP_K)
        for u in range(INVERT_UNROLL):
            inv_ref[dest_ref[base + u]] = tok0 + u // TOP_K
        return carry
    lax.fori_loop(0, n_ref[0] // INVERT_UNROLL, chunk, 0)


def _expert_body(li, n_blk, be_ref, nvb_ref, first_ref, wslot_ref, nxt_ref, dest_ref, fill_ref,
                 n_ref, x1t_hbm, wg_hbm, wu_hbm, wd_hbm, bg_ref, bu_ref, bd_ref, y_ref,
                 inv_s, wst, wg_s, wu_s, wd_s, *rest):
    j = pl.program_id(0)
    nvb = nvb_ref[0]
    weights = (wg_hbm, wu_hbm, wd_hbm)
    bufs = rest[:ROW_BUFS]
    sem, wsem = rest[ROW_BUFS:]
    ahead = ROW_BUFS - 1

    def issue(blk, k):
        base = jnp.minimum(blk, n_blk - 1) * MOE_ROWS
        for r in range(MOE_ROWS):
            tok = inv_s[base + r]
            src = x1t_hbm.at[pl.ds(pl.multiple_of(tok * ROW_TILES, ROW_TILES), ROW_TILES)]
            pltpu.make_async_copy(src, bufs[k].at[pl.ds(r * ROW_TILES, ROW_TILES)],
                                  sem.at[k]).start()

    def wait_rows(k):
        pltpu.make_async_copy(bufs[k], bufs[k], sem.at[k]).wait()

    def weight_copies(e, slot):
        return [pltpu.make_async_copy(w.at[li, e], wst.at[slot, k], wsem.at[slot])
                for k, w in enumerate(weights)]

    @pl.when(j == 0)
    def _():
        for cp in weight_copies(be_ref[0], 0):
            cp.start()
        _build_row_order(dest_ref, fill_ref, n_ref, inv_s)
        for b in range(ahead):
            issue(b, b)

    @pl.when(first_ref[j] == 1)
    def _():
        slot = wslot_ref[j]
        for cp in weight_copies(0, slot):
            cp.wait()
        wg_s[...] = wst[slot, 0].astype(BF16)
        wu_s[...] = wst[slot, 1].astype(BF16)
        wd_s[...] = wst[slot, 2].astype(BF16)

        @pl.when(nxt_ref[j] >= 0)
        def _():
            for cp in weight_copies(nxt_ref[j], 1 - slot):
                cp.start()

    def compute(k):
        issue(j + ahead, (k + ahead) % ROW_BUFS)
        wait_rows(k)
        x = _from_row_tiles(bufs[k], MOE_ROWS).astype(BF16)
        g = jnp.dot(x, wg_s[...], preferred_element_type=F32) + bg_ref[...]
        up = jnp.dot(x, wu_s[...], preferred_element_type=F32) + bu_ref[...]
        g = jnp.minimum(g, SWIGLU_LIMIT)
        up = jnp.clip(up, -SWIGLU_LIMIT, SWIGLU_LIMIT)
        hdn = (up + 1.0) * (g * _sigmoid(SWIGLU_ALPHA * g))
        y = jnp.dot(hdn.astype(BF16), wd_s[...], preferred_element_type=F32) + bd_ref[...]
        _to_row_tiles(y_ref, y, MOE_ROWS)

    for k in range(ROW_BUFS):
        @pl.when((j < nvb) & (j % ROW_BUFS == k))
        def _(k=k):
            compute(k)

        @pl.when((j >= nvb) & (j < nvb + ahead) & (j % ROW_BUFS == k))
        def _(k=k):
            wait_rows(k)

    @pl.when(j >= nvb)
    def _():
        y_ref[...] = jnp.zeros_like(y_ref)


def _experts(x1t, tables, li, w_gate, b_gate, w_up, b_up, w_down, b_down):
    n_rows = tables['n_rows']
    n_blk = n_rows // MOE_ROWS
    d = D_MODEL
    blk = (MOE_ROWS * ROW_TILES, LANES)
    prefetch = (tables['blk_expert'], tables['n_used'], tables['first'], tables['wslot'],
                tables['next_expert'], tables['dest'], tables['fill'], tables['n_assign'])
    bspec = pl.BlockSpec((None, None, 1, d),
                         lambda j, be, *_: (li, be[jnp.minimum(j, n_blk - 1)], 0, 0))
    bias = lambda v: v.reshape(v.shape[0], v.shape[1], 1, v.shape[2])
    hbm = pl.BlockSpec(memory_space=pl.ANY)
    rows = pltpu.VMEM(blk, F32)
    return pl.pallas_call(
        functools.partial(_expert_body, li, n_blk),
        grid_spec=pltpu.PrefetchScalarGridSpec(
            num_scalar_prefetch=len(prefetch), grid=(n_blk + ROW_BUFS - 1,),
            in_specs=[hbm, hbm, hbm, hbm, bspec, bspec, bspec],
            out_specs=pl.BlockSpec(blk, lambda j, *_: (jnp.minimum(j, n_blk - 1), 0)),
            scratch_shapes=[pltpu.SMEM((n_rows,), I32), pltpu.VMEM((2, 3, d, D_FF), F32),
                            pltpu.VMEM((d, D_FF), BF16), pltpu.VMEM((d, D_FF), BF16),
                            pltpu.VMEM((D_FF, d), BF16)] + [rows] * ROW_BUFS
                           + [pltpu.SemaphoreType.DMA((ROW_BUFS,)), pltpu.SemaphoreType.DMA((2,))]),
        out_shape=jax.ShapeDtypeStruct((n_rows * ROW_TILES, LANES), F32),
        compiler_params=_cparams(("arbitrary",)),
        name="moe_experts",
    )(*prefetch, x1t, w_gate, w_up, w_down, bias(b_gate), bias(b_up), bias(b_down))


COMBINE_BUFS = 3


def _combine_body(n_tiles, dest_ref, yb_hbm, route_ref, x1_ref, p_ref, wple_ref, wpg_ref, bpg_ref,
                  g2_ref, b2_ref, g3_ref, b3_ref, o_ref, buf0, buf1, buf2, sem):
    i = pl.program_id(0)
    ts = x1_ref.shape[0]
    bufs = (buf0, buf1, buf2)
    ahead = COMBINE_BUFS - 1

    def issue(tile, k):
        base = jnp.minimum(tile, n_tiles - 1) * (ts * TOP_K)
        for j in range(ts):
            for kk in range(TOP_K):
                row = dest_ref[base + j * TOP_K + kk]
                src = yb_hbm.at[pl.ds(pl.multiple_of(row * ROW_TILES, ROW_TILES), ROW_TILES)]
                pltpu.make_async_copy(src, bufs[k].at[kk, pl.ds(j * ROW_TILES, ROW_TILES)],
                                      sem.at[k]).start()

    def wait_rows(k):
        pltpu.make_async_copy(bufs[k], bufs[k], sem.at[k]).wait()

    @pl.when(i == 0)
    def _():
        for b in range(ahead):
            issue(b, b)

    def compute(k):
        issue(i + ahead, (k + ahead) % COMBINE_BUFS)
        wait_rows(k)
        route = route_ref[...]
        gates = [pltpu.bitcast(route[:, 2 * TOP_K + kk:2 * TOP_K + kk + 1], F32)
                 for kk in range(TOP_K)]
        parts = []
        for c in range(ROW_TILES):
            acc = jnp.zeros((ts, LANES), F32)
            for kk in range(TOP_K):
                acc = acc + gates[kk] * bufs[k][kk, pl.ds(c, ts, stride=ROW_TILES), :]
            parts.append(acc)
        y = jnp.concatenate(parts, axis=-1)
        x2 = _layer_norm(ALPHA * x1_ref[...] + y, g2_ref[...], b2_ref[...])
        emb = jnp.dot(p_ref[...].astype(BF16), wple_ref[...], preferred_element_type=F32)
        gate_in = jnp.dot(x2.astype(BF16), wpg_ref[...], preferred_element_type=F32) + bpg_ref[...]
        ple = emb * _sigmoid(gate_in)
        o_ref[...] = _layer_norm(ALPHA * x2 + ple, g3_ref[...], b3_ref[...])

    for k in range(COMBINE_BUFS):
        @pl.when((i < n_tiles) & (i % COMBINE_BUFS == k))
        def _(k=k):
            compute(k)

        @pl.when((i >= n_tiles) & (i % COMBINE_BUFS == k))
        def _(k=k):
            wait_rows(k)


def _combine(dest_flat, yb, route, x1, p3d, li, w_ple, w_ple_gate, b_ple_gate, ln2_g, ln2_b,
             ln3_g, ln3_b):
    t, d = x1.shape
    ts = min(256, t)
    n = t // ts
    rows = lambda v: v.reshape(v.shape[0], 1, v.shape[-1])
    tile = lambda w: pl.BlockSpec((ts, w), lambda i, dst: (jnp.minimum(i, n - 1), 0))
    buf = pltpu.VMEM((TOP_K, ts * ROW_TILES, LANES), F32)
    return pl.pallas_call(
        functools.partial(_combine_body, n),
        grid_spec=pltpu.PrefetchScalarGridSpec(
            num_scalar_prefetch=1, grid=(n + COMBINE_BUFS - 1,),
            in_specs=[pl.BlockSpec(memory_space=pl.ANY), tile(LANES), tile(d),
                      pl.BlockSpec((None, ts, PLE_DIM),
                                   lambda i, dst: (li, jnp.minimum(i, n - 1), 0)),
                      _layer_spec(li, (PLE_DIM, d)), _layer_spec(li, (d, d)),
                      _layer_spec(li, (1, d)), _layer_spec(li, (1, d)), _layer_spec(li, (1, d)),
                      _layer_spec(li, (1, d)), _layer_spec(li, (1, d))],
            out_specs=tile(d),
            scratch_shapes=[buf] * COMBINE_BUFS + [pltpu.SemaphoreType.DMA((COMBINE_BUFS,))]),
        out_shape=jax.ShapeDtypeStruct((t, d), F32),
        compiler_params=_cparams(("arbitrary",)),
        name="moe_combine_ple",
    )(dest_flat, yb, route, x1, p3d, w_ple, w_ple_gate, rows(b_ple_gate), rows(ln2_g), rows(ln2_b),
      rows(ln3_g), rows(ln3_b))


def _routing_tables(route, counts, t):
    counts = counts.reshape(N_EXPERTS).astype(I32)
    padded = (counts + MOE_ROWS - 1) // MOE_ROWS * MOE_ROWS
    pend = jnp.cumsum(padded)
    pstart = pend - padded
    top_e = route[:, 0:TOP_K]
    rank = route[:, TOP_K:2 * TOP_K]
    onehot = top_e[:, :, None] == jnp.arange(N_EXPERTS, dtype=I32)
    dest = (jnp.sum(jnp.where(onehot, pstart, 0), axis=-1) + rank).reshape(t * TOP_K)
    n_blk = t * TOP_K // MOE_ROWS + N_EXPERTS
    n_used = pend[N_EXPERTS - 1] // MOE_ROWS
    fill = jnp.concatenate([jnp.stack([pstart + counts, pend], axis=1).reshape(2 * N_EXPERTS),
                            jnp.stack([n_used, jnp.asarray(n_blk, I32)])]).astype(I32)
    n_tab = n_blk + ROW_BUFS - 1
    blk = jnp.arange(n_tab, dtype=I32)
    blk_expert = jnp.minimum(jnp.sum((pend[None, :] <= (blk * MOE_ROWS)[:, None]).astype(I32),
                                     axis=1), N_EXPERTS - 1)
    first = (blk < n_used) & ((blk == 0) | (blk_expert != jnp.roll(blk_expert, 1)))
    wslot = (jnp.cumsum(first.astype(I32)) - 1) % 2
    first_at = jnp.where(first, blk, n_tab)
    next_first = jnp.concatenate([lax.cummin(first_at[::-1])[::-1][1:],
                                  jnp.full((1,), n_tab, I32)])
    next_expert = jnp.where(next_first < n_tab,
                            blk_expert[jnp.minimum(next_first, n_tab - 1)], -1)
    return dict(dest=dest, fill=fill, blk_expert=blk_expert, n_used=n_used.reshape(1),
                first=first.astype(I32), wslot=wslot.astype(I32),
                next_expert=next_expert.astype(I32), n_assign=jnp.full((1,), t * TOP_K, I32),
                n_rows=n_blk * MOE_ROWS)


def _layer(x2d, p3d, batch, seq, li, w):
    t = batch * seq
    zd, *qkvs = _project(x2d, w['w_in'], li, batch, seq)
    y_a = _lru_branch(zd, batch, seq, li, w['conv_w'], w['conv_b'], w['w_rg'], w['b_rg'],
                      w['w_ig'], w['b_ig'], w['lru_lambda'], w['w_rnn_out'])
    outs, lses = [], []
    for gi in range(N_GROUPS):
        o, lse = _attention_group(qkvs[gi], gi)
        outs.append(o)
        lses.append(lse)
    x1, x1t, route, counts = _merge(outs, lses, y_a, zd, x2d, seq, li, w['w_att_out'], w['w_out'],
                                    w['ln1_g'], w['ln1_b'], w['w_router'], w['b_router'])
    tables = _routing_tables(route, counts, t)
    yb = _experts(x1t, tables, li, w['w_gate'], w['b_gate'], w['w_up'], w['b_up'], w['w_down'],
                  w['b_down'])
    return _combine(tables['dest'], yb, route, x1, p3d, li, w['w_ple'], w['w_ple_gate'],
                    w['b_ple_gate'],
                    w['ln2_g'], w['ln2_b'], w['ln3_g'], w['ln3_b'])


_WEIGHT_NAMES = ('w_in', 'conv_w', 'conv_b', 'w_rg', 'b_rg', 'w_ig', 'b_ig', 'lru_lambda',
                 'w_rnn_out', 'w_att_out', 'w_out', 'ln1_g', 'ln1_b', 'w_router', 'b_router',
                 'w_gate', 'b_gate', 'w_up', 'b_up', 'w_down', 'b_down', 'ln2_g', 'ln2_b',
                 'w_ple', 'w_ple_gate', 'b_ple_gate', 'ln3_g', 'ln3_b')
_BF16_WEIGHTS = ('w_in', 'w_rg', 'w_ig', 'w_rnn_out', 'w_att_out', 'w_out', 'w_ple', 'w_ple_gate')


def kernel(x, p, w_in, conv_w, conv_b, w_rg, b_rg, w_ig, b_ig, lru_lambda, w_rnn_out, w_att_out,
           w_out, ln1_g, ln1_b, w_router, b_router, w_gate, b_gate, w_up, b_up, w_down, b_down,
           ln2_g, ln2_b, w_ple, w_ple_gate, b_ple_gate, ln3_g, ln3_b):
    w = dict(zip(_WEIGHT_NAMES, (
        w_in, conv_w, conv_b, w_rg, b_rg, w_ig, b_ig, lru_lambda, w_rnn_out, w_att_out, w_out,
        ln1_g, ln1_b, w_router, b_router, w_gate, b_gate, w_up, b_up, w_down, b_down, ln2_g,
        ln2_b, w_ple, w_ple_gate, b_ple_gate, ln3_g, ln3_b)))
    for name in _BF16_WEIGHTS:
        w[name] = w[name].astype(BF16)
    batch, seq, d = x.shape
    depth = p.shape[0]
    x2d = x.reshape(batch * seq, d)
    p3d = p.reshape(depth, batch * seq, PLE_DIM)
    for li in range(depth):
        x2d = _layer(x2d, p3d, batch, seq, li, w)
    return x2d.reshape(batch, seq, d)
```
